```python
import math
import jax
import jax.numpy as jnp
from jax import lax
import numpy as np

D_MODEL = 1024
BATCH = 16
SEQ = 2048
DEPTH = 2
DEC_BATCH = 32
DEC_SEQ = 1
PAST_LEN = 16384
PAGE_SIZE = 128

HEAD_DIM = 64
N_HEADS = D_MODEL // HEAD_DIM
ATTN_WIDTH = N_HEADS * HEAD_DIM
ATTN_SCALE = HEAD_DIM ** -0.5
D_FF = 4 * D_MODEL
RMS_EPS = 1e-6
REL_BUCKETS = 32
REL_MAX_DIST = 2048
DIL_PATTERNS = ((128, 1), (512, 4), (2048, 16))
N_DIL = len(DIL_PATTERNS)
A_Q_BLOCK = 64
A_IN_COLS = N_DIL * 3 * ATTN_WIDTH
NSA_KV_HEADS = 4
NSA_GROUP = N_HEADS // NSA_KV_HEADS
CMP_LEN = 32
CMP_STRIDE = 16
CMP_HIDDEN = 2 * HEAD_DIM
SEL_BLOCK = 64
SEL_TOPK = 16
SEL_FORCE = 1e4
SLIDE_WIN = 512
B_Q_BLOCK = 32
N_KV_BRANCH = 6
N_PAGED = 4
B_IN_COLS = ATTN_WIDTH + N_KV_BRANCH * NSA_KV_HEADS * HEAD_DIM + 3 * N_HEADS
N_LAYERS_A = (DEPTH + 1) // 2
N_LAYERS_B = DEPTH // 2

kernel_name = 'hybrid_dilated_nsa_decoder_step'


def rms_norm(x, g):
    xf = x.astype(jnp.float32)
    y = xf * lax.rsqrt(jnp.mean(xf * xf, axis=-1, keepdims=True) + RMS_EPS)
    return (y * g.astype(jnp.float32)).astype(x.dtype)


def rel_bucket(dist):
    dist = jnp.maximum(dist, 0)
    max_exact = REL_BUCKETS // 2
    ratio = jnp.log(jnp.maximum(dist, 1).astype(jnp.float32) / max_exact) / math.log(REL_MAX_DIST / max_exact)
    large = jnp.minimum(max_exact + (ratio * (REL_BUCKETS - max_exact)).astype(jnp.int32), REL_BUCKETS - 1)
    return jnp.where(dist < max_exact, dist, large)


def masked_softmax(logits, valid):
    logits = jnp.where(valid, logits.astype(jnp.float32), -jnp.inf)
    m = jnp.max(logits, axis=-1, keepdims=True)
    m = jnp.where(jnp.isfinite(m), m, 0.0)
    e = jnp.where(valid, jnp.exp(logits - m), 0.0)
    return e / jnp.maximum(jnp.sum(e, axis=-1, keepdims=True), 1e-30)


def squared_relu_mlp(h, w_up, w_down):
    return jnp.square(jax.nn.relu(h @ w_up)) @ w_down


def dilated_group(q, k, v, q_idx, window, dil, table):
    n_keys = window // dil + 1
    offs = dil * jnp.arange(n_keys, dtype=jnp.int32)
    idx = q_idx[:, None] - offs[None, :]
    valid = idx >= 0
    idx = jnp.maximum(idx, 0)
    kg = k[:, idx]
    vg = v[:, idx]
    bias = table.astype(jnp.float32)[rel_bucket(offs)].T
    logits = jnp.einsum('bqhd,bqjhd->bqhj', q, kg).astype(jnp.float32) * ATTN_SCALE + bias
    logits = jnp.where(valid[None, :, None, :], logits, -jnp.inf)
    m = jnp.max(logits, axis=-1, keepdims=True)
    e = jnp.exp(logits - m)
    s = jnp.sum(e, axis=-1, keepdims=True)
    o = jnp.einsum('bqhj,bqjhd->bqhd', (e / s).astype(v.dtype), vg)
    return o, m + jnp.log(s)


def merge_by_denominator(outs, lses):
    w = jax.nn.softmax(jnp.stack(lses), axis=0)
    return jnp.sum(w * jnp.stack(outs).astype(jnp.float32), axis=0)


def mixer_a_prompt(h, w_in, w_out, table):
    bsz, seq, _ = h.shape
    qkv = (h @ w_in).reshape(bsz, seq, N_DIL, 3, N_HEADS, HEAD_DIM)
    qs = [qkv[:, :, g, 0] for g in range(N_DIL)]
    ks = [qkv[:, :, g, 1] for g in range(N_DIL)]
    vs = [qkv[:, :, g, 2] for g in range(N_DIL)]

    def block(q0):
        q_idx = q0 + jnp.arange(A_Q_BLOCK, dtype=jnp.int32)
        outs, lses = [], []
        for g, (window, dil) in enumerate(DIL_PATTERNS):
            qb = lax.dynamic_slice_in_dim(qs[g], q0, A_Q_BLOCK, axis=1)
            o, l = dilated_group(qb, ks[g], vs[g], q_idx, window, dil, table)
            outs.append(o)
            lses.append(l)
        return merge_by_denominator(outs, lses)

    starts = jnp.arange(seq // A_Q_BLOCK, dtype=jnp.int32) * A_Q_BLOCK
    o = jnp.moveaxis(lax.map(block, starts), 0, 1).reshape(bsz, seq, ATTN_WIDTH)
    y = o.astype(h.dtype) @ w_out
    bufs = [qkv[:, seq - min(window, seq):, g, 1:3] for g, (window, _) in enumerate(DIL_PATTERNS)]
    return y, bufs


def mixer_a_sample(h, bufs, w_in, w_out, table):
    bsz, nt, _ = h.shape
    qkv = (h @ w_in).reshape(bsz, nt, N_DIL, 3, N_HEADS, HEAD_DIM)
    outs, lses, new_bufs = [], [], []
    for g, (window, dil) in enumerate(DIL_PATTERNS):
        buf_len = bufs[g].shape[1]
        past_len = PAST_LEN
        kv_all = jnp.concatenate([bufs[g], qkv[:, :, g, 1:3]], axis=1)
        q_idx = buf_len + jnp.arange(nt, dtype=jnp.int32)
        o, l = dilated_group(qkv[:, :, g, 0], kv_all[:, :, 0], kv_all[:, :, 1], q_idx, window, dil, table)
        outs.append(o)
        lses.append(l)
        keep = min(window, past_len + nt)
        new_bufs.append(kv_all[:, kv_all.shape[1] - keep:])
    o = merge_by_denominator(outs, lses).reshape(bsz, nt, ATTN_WIDTH)
    return o.astype(h.dtype) @ w_out, new_bufs


def split_b(proj, bsz, nt):
    n_kv = N_KV_BRANCH * NSA_KV_HEADS * HEAD_DIM
    q = proj[..., :ATTN_WIDTH].reshape(bsz, nt, N_HEADS, HEAD_DIM)
    kv = proj[..., ATTN_WIDTH:ATTN_WIDTH + n_kv].reshape(bsz, nt, N_KV_BRANCH, NSA_KV_HEADS, HEAD_DIM)
    gate = jax.nn.sigmoid(proj[..., ATTN_WIDTH + n_kv:].astype(jnp.float32)).reshape(bsz, nt, N_HEADS, 3)
    return q, kv, gate


def compress(rows, pe, w1, w2):
    bsz, t = rows.shape[:2]
    n = (t - CMP_LEN) // CMP_STRIDE + 1
    r = CMP_LEN // CMP_STRIDE
    chunks = rows[:, :CMP_STRIDE * (n + r - 1)].reshape(bsz, n + r - 1, CMP_STRIDE, NSA_KV_HEADS, HEAD_DIM)
    w1r = w1.reshape(r, CMP_STRIDE, HEAD_DIM, CMP_HIDDEN)
    pe_r = pe.reshape(r, CMP_STRIDE, HEAD_DIM)
    hid = None
    for i in range(r):
        part = jnp.einsum('bnsgd,sdh->bngh', chunks[:, i:i + n] + pe_r[i][:, None, :], w1r[i])
        hid = part if hid is None else hid + part
    return jnp.einsum('bngh,hd->bngd', jax.nn.relu(hid), w2)


def nsa_attend(q, gate, kc, vc, k_sel, v_sel, k_win, v_win, q_pos, win_start, table):
    bsz, nq = q.shape[:2]
    G, R = NSA_KV_HEADS, NSA_GROUP
    qg = q.reshape(bsz, nq, G, R, HEAD_DIM)
    tbl = table.astype(jnp.float32).reshape(REL_BUCKETS, G, R)

    n_cmp = kc.shape[1]
    c_end = jnp.arange(n_cmp, dtype=jnp.int32) * CMP_STRIDE + (CMP_LEN - 1)
    dist_c = q_pos[:, None] - c_end[None, :]
    bias_c = jnp.transpose(tbl[rel_bucket(dist_c)], (0, 2, 3, 1))
    logit_c = jnp.einsum('bqgrd,bngd->bqgrn', qg, kc).astype(jnp.float32) * ATTN_SCALE + bias_c[None]
    p_c = masked_softmax(logit_c, (dist_c >= 0)[None, :, None, None, :])
    o_c = jnp.einsum('bqgrn,bngd->bqgrd', p_c.astype(vc.dtype), vc)

    n_keys = k_sel.shape[1]
    n_blk = -(-n_keys // SEL_BLOCK)
    ratio = SEL_BLOCK // CMP_STRIDE
    span = CMP_LEN // CMP_STRIDE
    imp = jnp.pad(jnp.sum(p_c, axis=3), ((0, 0), (0, 0), (0, 0), (span - 1, ratio * n_blk - n_cmp)))
    imp_blk = jnp.zeros((bsz, nq, G, n_blk), jnp.float32)
    for m in range(ratio):
        for n in range(span):
            off = m - n + span - 1
            imp_blk = imp_blk + imp[..., off:off + ratio * n_blk:ratio]
    blk = jnp.arange(n_blk, dtype=jnp.int32)
    cur = q_pos // SEL_BLOCK
    forced = (blk[None, :] == 0) | (blk[None, :] == cur[:, None]) | (blk[None, :] == cur[:, None] - 1)
    causal_blk = blk[None, :] * SEL_BLOCK <= q_pos[:, None]
    score = jnp.where(forced[None, :, None, :], SEL_FORCE,
                      jnp.where(causal_blk[None, :, None, :], imp_blk, -1.0))
    n_top = min(SEL_TOPK, n_blk)
    _, sel = lax.top_k(score, n_top)
    pad = n_blk * SEL_BLOCK - n_keys

    def to_blocks(x):
        x = jnp.pad(x, ((0, 0), (0, pad), (0, 0), (0, 0)))
        return jnp.transpose(x.reshape(bsz, n_blk, SEL_BLOCK, G, HEAD_DIM), (0, 3, 1, 2, 4))

    sel_g = jnp.transpose(sel, (0, 2, 1, 3)).reshape(bsz, G, nq * n_top)
    bi = jnp.arange(bsz)[:, None, None]
    gi = jnp.arange(G)[None, :, None]
    ks_g = to_blocks(k_sel)[bi, gi, sel_g].reshape(bsz, G, nq, n_top * SEL_BLOCK, HEAD_DIM)
    vs_g = to_blocks(v_sel)[bi, gi, sel_g].reshape(bsz, G, nq, n_top * SEL_BLOCK, HEAD_DIM)
    pos_s = (sel[..., None] * SEL_BLOCK + jnp.arange(SEL_BLOCK, dtype=jnp.int32)).reshape(bsz, nq, G, n_top * SEL_BLOCK)
    dist_s = q_pos[None, :, None, None] - pos_s
    bias_s = jnp.moveaxis(tbl[rel_bucket(dist_s), jnp.arange(G)[None, None, :, None]], 4, 3)
    logit_s = jnp.einsum('bqgrd,bgqkd->bqgrk', qg, ks_g).astype(jnp.float32) * ATTN_SCALE + bias_s
    p_s = masked_softmax(logit_s, (dist_s >= 0)[:, :, :, None, :])
    o_s = jnp.einsum('bqgrk,bgqkd->bqgrd', p_s.astype(vs_g.dtype), vs_g)

    n_w = k_win.shape[1]
    w_pos = win_start + jnp.arange(n_w, dtype=jnp.int32)
    dist_w = q_pos[:, None] - w_pos[None, :]
    valid_w = (dist_w >= 0) & (dist_w < SLIDE_WIN) & (w_pos[None, :] >= 0)
    bias_w = jnp.transpose(tbl[rel_bucket(dist_w)], (0, 2, 3, 1))
    logit_w = jnp.einsum('bqgrd,bsgd->bqgrs', qg, k_win).astype(jnp.float32) * ATTN_SCALE + bias_w[None]
    p_w = masked_softmax(logit_w, valid_w[None, :, None, None, :])
    o_w = jnp.einsum('bqgrs,bsgd->bqgrd', p_w.astype(v_win.dtype), v_win)

    gate = gate.reshape(bsz, nq, G, R, 3).astype(q.dtype)
    o = gate[..., 0:1] * o_c + gate[..., 1:2] * o_s + gate[..., 2:3] * o_w
    return o.reshape(bsz, nq, ATTN_WIDTH)


def mixer_b_prompt(h, w_in, w_out, phi_pe, phi_w1, phi_w2, table):
    bsz, seq, _ = h.shape
    q, kv, gate = split_b(h @ w_in, bsz, seq)
    kc = compress(kv[:, :, 0], phi_pe[0], phi_w1[0], phi_w2[0])
    vc = compress(kv[:, :, 1], phi_pe[1], phi_w1[1], phi_w2[1])
    k_sel, v_sel = kv[:, :, 2], kv[:, :, 3]
    win_pad = jnp.pad(kv[:, :, N_PAGED:], ((0, 0), (SLIDE_WIN, 0), (0, 0), (0, 0), (0, 0)))

    def block(q0):
        qb = lax.dynamic_slice_in_dim(q, q0, B_Q_BLOCK, axis=1)
        gb = lax.dynamic_slice_in_dim(gate, q0, B_Q_BLOCK, axis=1)
        wb = lax.dynamic_slice_in_dim(win_pad, q0, SLIDE_WIN + B_Q_BLOCK, axis=1)
        q_pos = q0 + jnp.arange(B_Q_BLOCK, dtype=jnp.int32)
        return nsa_attend(qb, gb, kc, vc, k_sel, v_sel, wb[:, :, 0], wb[:, :, 1], q_pos, q0 - SLIDE_WIN, table)

    starts = jnp.arange(seq // B_Q_BLOCK, dtype=jnp.int32) * B_Q_BLOCK
    o = jnp.moveaxis(lax.map(block, starts), 0, 1).reshape(bsz, seq, ATTN_WIDTH)
    y = o.astype(h.dtype) @ w_out
    keep = min(SLIDE_WIN, seq)
    return y, kv[:, :, :N_PAGED], kv[:, seq - keep:, N_PAGED:]


def mixer_b_sample(h, cache_kv, cache_win, layer, page_table, w_in, w_out, phi_pe, phi_w1, phi_w2, table):
    bsz, nt, _ = h.shape
    q, kv, gate = split_b(h @ w_in, bsz, nt)
    past_len = page_table.shape[1] * cache_kv.shape[2]

    def rows(kind):
        past = cache_kv[layer, page_table, :, kind].reshape(bsz, past_len, NSA_KV_HEADS, HEAD_DIM)
        return jnp.concatenate([past, kv[:, :, kind]], axis=1)

    kc = compress(rows(0), phi_pe[0], phi_w1[0], phi_w2[0])
    vc = compress(rows(1), phi_pe[1], phi_w1[1], phi_w2[1])
    win_all = jnp.concatenate([cache_win[layer], kv[:, :, N_PAGED:]], axis=1)
    q_pos = past_len + jnp.arange(nt, dtype=jnp.int32)
    o = nsa_attend(q, gate, kc, vc, rows(2), rows(3), win_all[:, :, 0], win_all[:, :, 1],
                   q_pos, past_len - cache_win.shape[2], table)
    y = o.astype(h.dtype) @ w_out
    keep = min(SLIDE_WIN, past_len + nt)
    return y, kv[:, :, :N_PAGED], win_all[:, win_all.shape[1] - keep:]


def setup_inputs(seed: int = 0) -> dict:
    key = jax.random.key(seed)
    ks = jax.random.split(key, 24)
    n_pages = PAST_LEN // PAGE_SIZE
    n_used = DEC_BATCH * n_pages
    n_pool = n_used + max(1, n_used // 4)

    def nrm(k, shape, s):
        return jax.random.normal(k, shape, jnp.float32) * s

    inp = {}
    inp['x_prompt'] = nrm(ks[0], (BATCH, SEQ, D_MODEL), 1.0)
    inp['x_sample'] = nrm(ks[1], (DEC_BATCH, DEC_SEQ, D_MODEL), 1.0)
    for g, (window, _) in enumerate(DIL_PATTERNS):
        inp[f'cache_a_win{g}'] = nrm(ks[2 + g], (N_LAYERS_A, DEC_BATCH, min(window, PAST_LEN), 2, N_HEADS, HEAD_DIM), 1.0)
    inp['cache_b_kv'] = nrm(ks[5], (N_LAYERS_B, n_pool, PAGE_SIZE, N_PAGED, NSA_KV_HEADS, HEAD_DIM), 1.0)
    inp['cache_b_win'] = nrm(ks[6], (N_LAYERS_B, DEC_BATCH, min(SLIDE_WIN, PAST_LEN), 2, NSA_KV_HEADS, HEAD_DIM), 1.0)
    inp['page_table'] = jax.random.permutation(ks[7], n_pool)[:n_used].reshape(DEC_BATCH, n_pages).astype(jnp.int32)
    inp['rel_bias'] = nrm(ks[8], (REL_BUCKETS, N_HEADS), 0.5)
    inp['norm_mix'] = 1.0 + nrm(ks[9], (DEPTH, D_MODEL), 0.05)
    inp['norm_mlp'] = 1.0 + nrm(ks[10], (DEPTH, D_MODEL), 0.05)
    inp['norm_final'] = 1.0 + nrm(ks[11], (D_MODEL,), 0.05)
    inp['a_w_in'] = nrm(ks[12], (N_LAYERS_A, D_MODEL, A_IN_COLS), D_MODEL ** -0.5)
    inp['a_w_out'] = nrm(ks[13], (N_LAYERS_A, ATTN_WIDTH, D_MODEL), ATTN_WIDTH ** -0.5)
    inp['b_w_in'] = nrm(ks[14], (N_LAYERS_B, D_MODEL, B_IN_COLS), D_MODEL ** -0.5)
    inp['b_w_out'] = nrm(ks[15], (N_LAYERS_B, ATTN_WIDTH, D_MODEL), ATTN_WIDTH ** -0.5)
    inp['b_phi_pe'] = nrm(ks[16], (N_LAYERS_B, 2, CMP_LEN, HEAD_DIM), 0.5)
    inp['b_phi_w1'] = nrm(ks[17], (N_LAYERS_B, 2, CMP_LEN * HEAD_DIM, CMP_HIDDEN), (CMP_LEN * HEAD_DIM) ** -0.5)
    inp['b_phi_w2'] = nrm(ks[18], (N_LAYERS_B, 2, CMP_HIDDEN, HEAD_DIM), (CMP_HIDDEN / 2) ** -0.5)
    inp['mlp_w_up'] = nrm(ks[19], (DEPTH, D_MODEL, D_FF), D_MODEL ** -0.5)
    inp['mlp_w_down'] = nrm(ks[20], (DEPTH, D_FF, D_MODEL), D_FF ** -0.5)
    return inp


def reference(x_prompt, x_sample, cache_a_win0, cache_a_win1, cache_a_win2, cache_b_kv, cache_b_win,
              page_table, rel_bias, norm_mix, norm_mlp, norm_final, a_w_in, a_w_out, b_w_in, b_w_out,
              b_phi_pe, b_phi_w1, b_phi_w2, mlp_w_up, mlp_w_down):
    a_caches = (cache_a_win0, cache_a_win1, cache_a_win2)
    hp, hs = x_prompt, x_sample
    a_new_p = [[] for _ in range(N_DIL)]
    a_new_s = [[] for _ in range(N_DIL)]
    b_kv_p, b_kv_s, b_win_p, b_win_s = [], [], [], []
    for i in range(DEPTH):
        j = i // 2
        hn_p = rms_norm(hp, norm_mix[i])
        hn_s = rms_norm(hs, norm_mix[i])
        if i % 2 == 0:
            yp, bufs_p = mixer_a_prompt(hn_p, a_w_in[j], a_w_out[j], rel_bias)
            ys, bufs_s = mixer_a_sample(hn_s, [c[j] for c in a_caches], a_w_in[j], a_w_out[j], rel_bias)
            for g in range(N_DIL):
                a_new_p[g].append(bufs_p[g])
                a_new_s[g].append(bufs_s[g])
        else:
            yp, kvp, winp = mixer_b_prompt(hn_p, b_w_in[j], b_w_out[j], b_phi_pe[j], b_phi_w1[j], b_phi_w2[j], rel_bias)
            ys, kvs, wins = mixer_b_sample(hn_s, cache_b_kv, cache_b_win, j, page_table, b_w_in[j], b_w_out[j],
                                           b_phi_pe[j], b_phi_w1[j], b_phi_w2[j], rel_bias)
            b_kv_p.append(kvp)
            b_kv_s.append(kvs)
            b_win_p.append(winp)
            b_win_s.append(wins)
        hp = hp + yp
        hs = hs + ys
        hp = hp + squared_relu_mlp(rms_norm(hp, norm_mlp[i]), mlp_w_up[i], mlp_w_down[i])
        hs = hs + squared_relu_mlp(rms_norm(hs, norm_mlp[i]), mlp_w_up[i], mlp_w_down[i])
    y_prompt = rms_norm(hp, norm_final)
    y_sample = rms_norm(hs, norm_final)
    a_win0_prompt = jnp.stack(a_new_p[0])
    a_win0_sample = jnp.stack(a_new_s[0])
    a_win1_prompt = jnp.stack(a_new_p[1])
    a_win1_sample = jnp.stack(a_new_s[1])
    a_win2_prompt = jnp.stack(a_new_p[2])
    a_win2_sample = jnp.stack(a_new_s[2])
    b_kv_prompt = jnp.stack(b_kv_p)
    b_kv_sample = jnp.stack(b_kv_s)
    b_win_prompt = jnp.stack(b_win_p)
    b_win_sample = jnp.stack(b_win_s)
    return (y_prompt, y_sample, a_win0_prompt, a_win0_sample, a_win1_prompt, a_win1_sample,
            a_win2_prompt, a_win2_sample, b_kv_prompt, b_kv_sample, b_win_prompt, b_win_sample)
```

```python
import functools
import math

import numpy as np
import jax
import jax.numpy as jnp
from jax import lax
from jax.experimental import pallas as pl
from jax.experimental.pallas import tpu as pltpu

F32 = jnp.float32
BF16 = jnp.bfloat16

D_MODEL = 1024
HEAD_DIM = 64
N_HEADS = 16
ATTN_SCALE = HEAD_DIM ** -0.5
RMS_EPS = 1e-6
REL_BUCKETS = 32
REL_MAX_DIST = 2048
DIL_PATTERNS = ((128, 1), (512, 4), (2048, 16))
A_KEYS = 128
N_GROUPS = 4
GROUP_HEADS = N_HEADS // N_GROUPS
CMP_LEN = 32
CMP_STRIDE = 16
CMP_HIDDEN = 128
SEL_BLOCK = 64
SEL_TOPK = 16
SEL_FORCE = 1e4
SLIDE_WIN = 512

LANES = 128
VMEM_LIMIT = 56 * 1024 * 1024
NEG = -1e30

NT_DIMS = (((1,), (1,)), ((), ()))


def _cparams(*sem):
    return pltpu.CompilerParams(dimension_semantics=sem, vmem_limit_bytes=VMEM_LIMIT)


def _dot(a, b):
    return jnp.dot(a, b, preferred_element_type=F32)


def _dot_nt(a, b):
    return lax.dot_general(a, b, NT_DIMS, preferred_element_type=F32)


def _split_dot(x, w):
    hi = x.astype(BF16)
    lo = (x - hi.astype(F32)).astype(BF16)
    return _dot(hi, w) + _dot(lo, w)


def _split_dot_nt(w, x):
    hi = x.astype(BF16)
    lo = (x - hi.astype(F32)).astype(BF16)
    return _dot_nt(w, hi) + _dot_nt(w, lo)


def _rms(x, g):
    ms = jnp.mean(x * x, axis=-1, keepdims=True)
    return x * lax.rsqrt(ms + RMS_EPS) * g


def _bucket_np(dist):
    dist = np.maximum(np.asarray(dist, np.int64), 0)
    max_exact = REL_BUCKETS // 2
    ratio = (np.log(np.maximum(dist, 1).astype(np.float32) / np.float32(max_exact))
             / np.float32(math.log(REL_MAX_DIST / max_exact)))
    large = np.minimum(max_exact + (ratio * (REL_BUCKETS - max_exact)).astype(np.int32), REL_BUCKETS - 1)
    return np.where(dist < max_exact, dist, large).astype(np.int32)


def _bias_from_dist(rel_bias, dist, valid):
    idx = jnp.asarray(_bucket_np(dist))
    b = jnp.take(rel_bias.astype(F32).T, idx, axis=1)
    return jnp.where(jnp.asarray(valid)[None], b, NEG)


def _norm_matmul_kernel(x_ref, g_ref, *refs, n_w, acts):
    w_refs, o_refs, xn_ref = refs[:n_w], refs[n_w:2 * n_w], refs[2 * n_w]

    @pl.when(pl.program_id(1) == 0)
    def _():
        xn_ref[...] = _rms(x_ref[...], g_ref[...]).astype(BF16)

    xn = xn_ref[...]
    for w_ref, o_ref, act in zip(w_refs, o_refs, acts):
        y = _dot(xn, w_ref[...])
        if act == "sigmoid":
            y = jax.nn.sigmoid(y)
        o_ref[...] = y


def _norm_matmul(x, g, ws, acts, tm, nj):
    m, d = x.shape
    kern = functools.partial(_norm_matmul_kernel, n_w=len(ws), acts=tuple(acts))
    in_specs = [pl.BlockSpec((tm, d), lambda i, j: (i, 0)), pl.BlockSpec((1, d), lambda i, j: (0, 0))]
    in_specs += [pl.BlockSpec((d, w.shape[1] // nj), lambda i, j: (0, j)) for w in ws]
    out_specs = [pl.BlockSpec((tm, w.shape[1] // nj), lambda i, j: (i, j)) for w in ws]
    return pl.pallas_call(
        kern,
        grid=(m // tm, nj),
        in_specs=in_specs,
        out_specs=out_specs,
        out_shape=[jax.ShapeDtypeStruct((m, w.shape[1]), F32) for w in ws],
        scratch_shapes=[pltpu.VMEM((tm, d), BF16)],
        compiler_params=_cparams("parallel", "arbitrary"),
        name="norm_matmul",
    )(x, g.reshape(1, d), *ws)


def _matmul_res_kernel(a_ref, x_ref, w_ref, o_ref):
    o_ref[...] = x_ref[...] + _dot(a_ref[...].astype(BF16), w_ref[...])


def _matmul_res(a, x, w, tm):
    m, d = x.shape
    return pl.pallas_call(
        _matmul_res_kernel,
        grid=(m // tm,),
        in_specs=[pl.BlockSpec((tm, a.shape[1]), lambda i: (i, 0)),
                  pl.BlockSpec((tm, d), lambda i: (i, 0)),
                  pl.BlockSpec(w.shape, lambda i: (0, 0))],
        out_specs=pl.BlockSpec((tm, d), lambda i: (i, 0)),
        out_shape=jax.ShapeDtypeStruct((m, d), F32),
        compiler_params=_cparams("parallel"),
        name="matmul_res",
    )(a, x, w)


def _merge_proj_kernel(o0, o1, o2, l0, l1, l2, e_ref, x_ref, w_ref, out_ref):
    ls = [l0[...], l1[...], l2[...]]
    mx = jnp.maximum(jnp.maximum(ls[0], ls[1]), ls[2])
    es = [jnp.exp(l - mx) for l in ls]
    tot = es[0] + es[1] + es[2]
    merged = None
    for e, o in zip(es, (o0, o1, o2)):
        part = _split_dot(e / tot, e_ref[...]) * o[...]
        merged = part if merged is None else merged + part
    out_ref[...] = x_ref[...] + _dot(merged.astype(BF16), w_ref[...])


def _merge_proj(os_, ls, x, w, tm):
    m, d = x.shape
    expand = jnp.asarray(np.kron(np.eye(N_HEADS), np.ones((1, HEAD_DIM))), BF16)
    row = lambda n: pl.BlockSpec((tm, n), lambda i: (i, 0))
    return pl.pallas_call(
        _merge_proj_kernel,
        grid=(m // tm,),
        in_specs=[row(d)] * 3 + [row(N_HEADS)] * 3 + [
            pl.BlockSpec(expand.shape, lambda i: (0, 0)), row(d), pl.BlockSpec(w.shape, lambda i: (0, 0))],
        out_specs=row(d),
        out_shape=jax.ShapeDtypeStruct((m, d), F32),
        compiler_params=_cparams("parallel"),
        name="merge_proj",
    )(*os_, *ls, expand, x, w)


def _mlp_kernel(x_ref, g_ref, wu_ref, wd_ref, gf_ref, o_ref, xn_ref, acc_ref, *, final_norm):
    f = pl.program_id(1)

    @pl.when(f == 0)
    def _():
        xn_ref[...] = _rms(x_ref[...], g_ref[...]).astype(BF16)
        acc_ref[...] = jnp.zeros_like(acc_ref)

    h = jnp.square(jnp.maximum(_dot(xn_ref[...], wu_ref[...]), 0.0))
    acc_ref[...] += _dot(h.astype(BF16), wd_ref[...])

    @pl.when(f == pl.num_programs(1) - 1)
    def _():
        y = x_ref[...] + acc_ref[...]
        if final_norm:
            y = _rms(y, gf_ref[...])
        o_ref[...] = y


def _mlp(x, g, wu, wd, gf, final_norm, tm, tf):
    m, d = x.shape
    dff = wu.shape[1]
    return pl.pallas_call(
        functools.partial(_mlp_kernel, final_norm=final_norm),
        grid=(m // tm, dff // tf),
        in_specs=[pl.BlockSpec((tm, d), lambda i, f: (i, 0)),
                  pl.BlockSpec((1, d), lambda i, f: (0, 0)),
                  pl.BlockSpec((d, tf), lambda i, f: (0, f)),
                  pl.BlockSpec((tf, d), lambda i, f: (f, 0)),
                  pl.BlockSpec((1, d), lambda i, f: (0, 0))],
        out_specs=pl.BlockSpec((tm, d), lambda i, f: (i, 0)),
        out_shape=jax.ShapeDtypeStruct((m, d), F32),
        scratch_shapes=[pltpu.VMEM((tm, d), BF16), pltpu.VMEM((tm, d), F32)],
        compiler_params=_cparams("parallel", "arbitrary"),
        name="mlp",
    )(x, g.reshape(1, d), wu, wd, gf.reshape(1, d))


def _swa_kernel(q_ref, kc_ref, kp_ref, vc_ref, vp_ref, b_ref, o_ref, lse_ref):
    t = q_ref.shape[1]
    lane = lax.broadcasted_iota(jnp.int32, (t, LANES), 1)
    low = lane < HEAD_DIM
    hcol = lax.broadcasted_iota(jnp.int32, (t, N_HEADS), 1)
    lse_all = jnp.zeros((t, N_HEADS), F32)
    for hp in range(N_HEADS // 2):
        sl = slice(hp * LANES, (hp + 1) * LANES)
        q2 = (q_ref[0, :, sl] * ATTN_SCALE).astype(BF16)
        k2 = jnp.concatenate([kp_ref[0, :, sl], kc_ref[0, :, sl]], axis=0).astype(BF16)
        v2 = jnp.concatenate([vp_ref[0, :, sl], vc_ref[0, :, sl]], axis=0).astype(BF16)
        outs = []
        for a in range(2):
            h = 2 * hp + a
            qm = jnp.where(low if a == 0 else jnp.logical_not(low), q2, jnp.zeros_like(q2))
            s = _dot_nt(qm, k2) + b_ref[0, h]
            m = jnp.max(s, axis=-1, keepdims=True)
            e = jnp.exp(s - m)
            l = jnp.sum(e, axis=-1, keepdims=True)
            outs.append(_dot(e.astype(BF16), v2) / l)
            lse_all = jnp.where(hcol == h, m + jnp.log(l), lse_all)
        o_ref[0, :, sl] = jnp.where(low, outs[0], outs[1])
    lse_ref[0, 0] = lse_all


def _swa_bias(rel_bias, dil, t):
    i = np.arange(t)[:, None]
    c = np.arange(2 * t)[None, :]
    prev = c < t
    dist = np.where(prev, t + i - c, i - (c - t))
    valid = np.where(prev, c >= i, (c - t) <= i)
    later = _bias_from_dist(rel_bias, dil * dist, valid)
    first = _bias_from_dist(rel_bias, dil * dist, valid & ~prev)
    return jnp.stack([first, later])


def _swa_group(q, kv, rel_bias, bsz, seq, dil):
    t = A_KEYS
    ln = seq // dil
    d = D_MODEL
    qv = q.reshape(bsz, ln, dil * d)
    kvv = kv.reshape(bsz, ln, dil * 2 * d)
    bias = _swa_bias(rel_bias, dil, t)
    prev = lambda qi: jnp.maximum(qi - 1, 0)
    o, lse = pl.pallas_call(
        _swa_kernel,
        grid=(bsz, dil, ln // t),
        in_specs=[pl.BlockSpec((1, t, d), lambda b, r, qi: (b, qi, r)),
                  pl.BlockSpec((1, t, d), lambda b, r, qi: (b, qi, 2 * r)),
                  pl.BlockSpec((1, t, d), lambda b, r, qi: (b, prev(qi), 2 * r)),
                  pl.BlockSpec((1, t, d), lambda b, r, qi: (b, qi, 2 * r + 1)),
                  pl.BlockSpec((1, t, d), lambda b, r, qi: (b, prev(qi), 2 * r + 1)),
                  pl.BlockSpec((1, N_HEADS, t, 2 * t), lambda b, r, qi: (jnp.minimum(qi, 1), 0, 0, 0))],
        out_specs=[pl.BlockSpec((1, t, d), lambda b, r, qi: (b, qi, r)),
                   pl.BlockSpec((1, 1, t, N_HEADS), lambda b, r, qi: (b, r, qi, 0))],
        out_shape=[jax.ShapeDtypeStruct((bsz, ln, dil * d), F32),
                   jax.ShapeDtypeStruct((bsz, dil, ln, N_HEADS), F32)],
        compiler_params=_cparams("parallel", "parallel", "arbitrary"),
        name="swa_attn",
    )(qv, kvv, kvv, kvv, kvv, bias)
    lse = jnp.transpose(lse, (0, 2, 1, 3)).reshape(bsz * seq, N_HEADS)
    return o.reshape(bsz * seq, d), lse


def _head_diag(rows, width):
    row = lax.broadcasted_iota(jnp.int32, (rows, width), 0)
    lane = lax.broadcasted_iota(jnp.int32, (rows, width), 1)
    return (lane // HEAD_DIM) == row


def _samp_a_kernel(q0, q1, q2, n0, n1, n2, c0, c1, c2, bc_ref, b0_ref, o_ref):
    d = D_MODEL
    diag = _head_diag(N_HEADS, d)
    outs, lses = [], []
    for g, (q_ref, n_ref, c_ref) in enumerate(((q0, n0, c0), (q1, n1, c1), (q2, n2, c2))):
        qe = jnp.where(diag, q_ref[0] * ATTN_SCALE, 0.0)
        kc = c_ref[0, :, :d].astype(BF16)
        vc = c_ref[0, :, d:].astype(BF16)
        kn = n_ref[0][:, :d]
        vn = n_ref[0][:, d:]
        s = _dot_nt(qe.astype(BF16), kc) + bc_ref[g]
        sn = jnp.sum(qe * kn, axis=-1, keepdims=True) + b0_ref[g][:, :1]
        m = jnp.maximum(jnp.max(s, axis=-1, keepdims=True), sn)
        e = jnp.exp(s - m)
        en = jnp.exp(sn - m)
        l = jnp.sum(e, axis=-1, keepdims=True) + en
        outs.append((_dot(e.astype(BF16), vc) + en * vn) / l)
        lses.append(m + jnp.log(l))
    mx = jnp.maximum(jnp.maximum(lses[0], lses[1]), lses[2])
    es = [jnp.exp(l - mx) for l in lses]
    tot = es[0] + es[1] + es[2]
    merged = (es[0] / tot) * outs[0] + (es[1] / tot) * outs[1] + (es[2] / tot) * outs[2]
    o_ref[0] = jnp.sum(jnp.where(diag, merged, 0.0), axis=0, keepdims=True)


def _samp_a(qs, kvs, caches, rel_bias):
    bsz, d = qs[0].shape
    views, bcs, b0s = [], [], []
    for (window, dil), c in zip(DIL_PATTERNS, caches):
        views.append(c.reshape(bsz, window // dil, dil * 2 * d))
        dist = window - dil * np.arange(window // dil)
        bcs.append(_bias_from_dist(rel_bias, dist, np.ones_like(dist, bool)))
        b0s.append(_bias_from_dist(rel_bias, np.zeros((LANES,), np.int64), np.ones((LANES,), bool)))
    vec = lambda n: pl.BlockSpec((1, 1, n), lambda b: (b, 0, 0))
    full = lambda a: pl.BlockSpec(a.shape, lambda b: (0,) * a.ndim)
    bc, b0 = jnp.stack(bcs), jnp.stack(b0s)
    out = pl.pallas_call(
        _samp_a_kernel,
        grid=(bsz,),
        in_specs=[vec(d)] * 3 + [vec(2 * d)] * 3
        + [pl.BlockSpec((1, A_KEYS, 2 * d), lambda b: (b, 0, 0))] * 3 + [full(bc), full(b0)],
        out_specs=vec(d),
        out_shape=jax.ShapeDtypeStruct((bsz, 1, d), F32),
        compiler_params=_cparams("parallel"),
        name="sample_dilated_attn",
    )(*[q.reshape(bsz, 1, d) for q in qs], *[kv.reshape(bsz, 1, 2 * d) for kv in kvs], *views, bc, b0)
    return out.reshape(bsz, d)


SHIFT_CHUNKS = 8


def _shift_kernel(c_ref, n_ref, o_ref, sem, *, window, bchunk):
    copies = []
    for i in range(SHIFT_CHUNKS):
        rows = pl.ds(i * bchunk, bchunk)
        copies.append(pltpu.make_async_copy(c_ref.at[rows, pl.ds(1, window - 1)],
                                            o_ref.at[rows, pl.ds(0, window - 1)], sem.at[i]))
    copies.append(pltpu.make_async_copy(n_ref, o_ref.at[:, pl.ds(window - 1, 1)], sem.at[SHIFT_CHUNKS]))
    for cp in copies:
        cp.start()
    for cp in copies:
        cp.wait()


def _shift_append(cache, new):
    bsz, window, c = cache.shape
    shape4 = (bsz, window, c // LANES, LANES)
    out = pl.pallas_call(
        functools.partial(_shift_kernel, window=window, bchunk=bsz // SHIFT_CHUNKS),
        in_specs=[pl.BlockSpec(memory_space=pl.ANY)] * 2,
        out_specs=pl.BlockSpec(memory_space=pl.ANY),
        out_shape=jax.ShapeDtypeStruct(shape4, F32),
        scratch_shapes=[pltpu.SemaphoreType.DMA((SHIFT_CHUNKS + 1,))],
        name="cache_shift_append",
    )(cache.reshape(shape4), new.reshape(bsz, 1, c // LANES, LANES))
    return out.reshape(bsz, window, c)


def _cmp_weights(phi_pe, phi_w1, phi_w2):
    r = CMP_LEN // CMP_STRIDE
    eye = jnp.eye(N_GROUPS, dtype=F32)
    w1r = phi_w1.reshape(2, r, CMP_STRIDE, HEAD_DIM, CMP_HIDDEN)
    w1bd = jnp.einsum("kpsdh,gG->kpsgdGh", w1r, eye).reshape(
        2, r, CMP_STRIDE, N_GROUPS * HEAD_DIM, N_GROUPS * CMP_HIDDEN).astype(BF16)
    w2bd = jnp.einsum("khd,gG->kghGd", phi_w2, eye).reshape(
        2, N_GROUPS * CMP_HIDDEN, N_GROUPS * HEAD_DIM).astype(BF16)
    w2ex = jnp.einsum("khd,gG,r->kghGrd", phi_w2, eye, jnp.ones((GROUP_HEADS,), F32)).reshape(
        2, N_GROUPS * CMP_HIDDEN, D_MODEL).astype(BF16)
    pe = jnp.tile(phi_pe.reshape(2, r, CMP_STRIDE, 1, HEAD_DIM), (1, 1, 1, N_GROUPS, 1)).reshape(
        2, r, CMP_STRIDE, N_GROUPS * HEAD_DIM)
    return w1bd, w2bd, w2ex, pe


def _cmp_prompt_kernel(x_ref, w1_ref, pe_ref, w2_ref, o_ref, *, kind):
    nch = x_ref.shape[1]
    gw = N_GROUPS * HEAD_DIM
    hid = []
    for part in range(CMP_LEN // CMP_STRIDE):
        acc = jnp.zeros((nch, N_GROUPS * CMP_HIDDEN), F32)
        for s in range(CMP_STRIDE):
            c0 = s * 4 * gw + kind * gw
            x = (x_ref[0, :, c0:c0 + gw] + pe_ref[part, s:s + 1, :]).astype(BF16)
            acc = acc + _dot(x, w1_ref[part, s])
        hid.append(acc)
    h = hid[0] + pltpu.roll(hid[1], nch - 1, 0)
    o_ref[0] = _dot(jnp.maximum(h, 0.0).astype(BF16), w2_ref[...])


def _cmp_prompt(kvp, bsz, seq, kind, w1bd, pe, w2bd):
    nch = seq // CMP_STRIDE
    xv = kvp.reshape(bsz, nch, CMP_STRIDE * kvp.shape[1])
    full = lambda a: pl.BlockSpec(a.shape, lambda b: (0,) * a.ndim)
    w1, p, w2 = w1bd[kind], pe[kind], w2bd[kind]
    return pl.pallas_call(
        functools.partial(_cmp_prompt_kernel, kind=kind),
        grid=(bsz,),
        in_specs=[pl.BlockSpec((1, nch, xv.shape[2]), lambda b: (b, 0, 0)), full(w1), full(p), full(w2)],
        out_specs=pl.BlockSpec((1, nch, w2.shape[1]), lambda b: (b, 0, 0)),
        out_shape=jax.ShapeDtypeStruct((bsz, nch, w2.shape[1]), F32),
        compiler_params=_cparams("parallel"),
        name="compress_prompt",
    )(xv, w1, p, w2)


def _masked_softmax(s, valid):
    m = jnp.max(s, axis=-1, keepdims=True)
    m = jnp.where(m > 0.5 * NEG, m, 0.0)
    e = jnp.where(valid, jnp.exp(s - m), 0.0)
    return e / jnp.maximum(jnp.sum(e, axis=-1, keepdims=True), 1e-30)


def _topk_mask_t(score, j, n_iter):
    rank = jnp.zeros(score.shape, F32)
    for i in range(n_iter):
        si = score[i:i + 1, :]
        beats = jnp.logical_or(si > score, jnp.logical_and(si == score, j > i))
        rank = rank + jnp.where(beats, 1.0, 0.0)
    return jnp.where(rank < SEL_TOPK, 1.0, 0.0)


def _nsa_kernel(q_ref, gate_ref, kc_ref, vc_ref, ks_ref, vs_ref, kw_ref, vw_ref, bc_ref, u_ref, uw_ref,
                at_ref, eb_ref, eye_ref, o_ref, selb_ref, m_ref, l_ref, acc_ref, *, seq, tq, tk):
    g = pl.program_id(0)
    qi = pl.program_id(2)
    q0 = qi * tq
    par = g % 2
    nb = seq // SEL_BLOCK
    lane = lax.broadcasted_iota(jnp.int32, (tq, LANES), 1)
    keep = (lane // HEAD_DIM) == par
    low = lane < HEAD_DIM

    qs = []
    for r in range(GROUP_HEADS):
        qh = q_ref[0, :, (r // 2) * LANES:(r // 2 + 1) * LANES] * ATTN_SCALE
        qh = jnp.where(par == (r % 2), qh, pltpu.roll(qh, HEAD_DIM, 1))
        qs.append(jnp.where(keep, qh, 0.0).astype(BF16))
    q4 = jnp.concatenate(qs, axis=0)

    bc = bc_ref[...].reshape(GROUP_HEADS * tq, LANES)
    pc = _masked_softmax(_dot_nt(q4, kc_ref[0].astype(BF16)) + bc, bc > 0.5 * NEG)
    oc = _dot(pc.astype(BF16), vc_ref[0].astype(BF16))
    imp = pc[0:tq] + pc[tq:2 * tq] + pc[2 * tq:3 * tq] + pc[3 * tq:4 * tq]

    imp_t = _split_dot_nt(at_ref[...], imp)
    j = lax.broadcasted_iota(jnp.int32, (nb, tq), 0)
    qpos = q0 + lax.broadcasted_iota(jnp.int32, (nb, tq), 1)
    cur = qpos // SEL_BLOCK
    forced = jnp.logical_or(j == 0, jnp.logical_or(j == cur, j == cur - 1))
    score = jnp.where(forced, SEL_FORCE, jnp.where(j * SEL_BLOCK <= qpos, imp_t, -1.0))
    sel_t = _topk_mask_t(score, j, nb).astype(BF16)
    sel = _dot_nt(eye_ref[...], sel_t)
    selb_ref[...] = (_dot(sel.astype(BF16), eb_ref[...]) - 1.0) * (-NEG)

    def flash(lo, hi, tile, kv_tile, bias_tile):
        m_ref[...] = jnp.full(m_ref.shape, NEG, F32)
        l_ref[...] = jnp.zeros(l_ref.shape, F32)
        acc_ref[...] = jnp.zeros(acc_ref.shape, F32)

        def body(t, carry):
            k2, v2 = kv_tile(t)
            for r in range(GROUP_HEADS):
                s = _dot_nt(qs[r], k2) + bias_tile(t, r)
                m_prev = m_ref[r]
                m_new = jnp.maximum(m_prev, jnp.max(s, axis=-1, keepdims=True))
                alpha = jnp.exp(m_prev - m_new)
                e = jnp.exp(s - m_new)
                l_ref[r] = alpha * l_ref[r] + jnp.sum(e, axis=-1, keepdims=True)
                acc_ref[r] = alpha * acc_ref[r] + _dot(e.astype(BF16), v2)
                m_ref[r] = m_new
            return carry

        lax.fori_loop(lo, hi, body, 0)
        return [acc_ref[r] / l_ref[r] for r in range(GROUP_HEADS)]

    def sel_kv(t):
        k0 = pl.multiple_of(t * tk, tk)
        return ks_ref[0, pl.ds(k0, tk), :].astype(BF16), vs_ref[0, pl.ds(k0, tk), :].astype(BF16)

    def sel_bias(t, r):
        k0 = pl.multiple_of(t * tk, tk)
        u0 = pl.multiple_of(k0 - q0 + (seq - tq), LANES)
        return u_ref[r, :, pl.ds(u0, tk)] + selb_ref[:, pl.ds(k0, tk)]

    osel = flash(0, (q0 + tq + tk - 1) // tk, tk, sel_kv, sel_bias)

    def win_kv(t):
        w0 = pl.multiple_of(q0 - SLIDE_WIN + t * tq, tq)
        return kw_ref[0, pl.ds(w0, tq), :].astype(BF16), vw_ref[0, pl.ds(w0, tq), :].astype(BF16)

    def win_bias(t, r):
        return uw_ref[r, :, pl.ds(pl.multiple_of(t * tq, tq), tq)]

    nwin = SLIDE_WIN // tq
    owin = flash(jnp.maximum(nwin - qi, 0), nwin + 1, tq, win_kv, win_bias)

    gate = gate_ref[0]
    pieces = []
    for r in range(GROUP_HEADS):
        o = (gate[:, 3 * r:3 * r + 1] * oc[r * tq:(r + 1) * tq]
             + gate[:, 3 * r + 1:3 * r + 2] * osel[r] + gate[:, 3 * r + 2:3 * r + 3] * owin[r])
        pieces.append(jnp.where(par == (r % 2), o, pltpu.roll(o, HEAD_DIM, 1)))
    for hh in range(GROUP_HEADS // 2):
        o_ref[0, :, hh * LANES:(hh + 1) * LANES] = jnp.where(low, pieces[2 * hh], pieces[2 * hh + 1])


def _imp_to_block_matrix(n_cmp_pad, n_cmp, n_blk):
    ratio = SEL_BLOCK // CMP_STRIDE
    span = CMP_LEN // CMP_STRIDE
    a = np.zeros((n_blk, n_cmp_pad), np.float32)
    for m in range(ratio):
        for n in range(span):
            for jb in range(n_blk):
                c = ratio * jb + m - n
                if 0 <= c < n_cmp:
                    a[jb, c] += 1.0
    return a


def _nsa_prompt(q, gate, kc, vc, kvp, kvw, rel_bias, bsz, seq):
    tq, tk = 128, 256
    d = D_MODEL
    nb = seq // SEL_BLOCK
    ncp = seq // CMP_STRIDE
    n_cmp = (seq - CMP_LEN) // CMP_STRIDE + 1
    assert ncp == LANES and seq % tk == 0
    qv = q.reshape(bsz, seq, d)
    gv = gate.reshape(bsz, seq, N_GROUPS * LANES)
    kvpv = kvp.reshape(bsz, seq, kvp.shape[1])
    kvwv = kvw.reshape(bsz, seq, kvw.shape[1])

    pos = np.arange(seq)[:, None]
    n = np.arange(ncp)[None, :]
    dist_c = pos - (n * CMP_STRIDE + CMP_LEN - 1)
    bias_c = _bias_from_dist(rel_bias, dist_c, (dist_c >= 0) & (n < n_cmp))
    i = np.arange(tq)[:, None]
    wu = seq - tq + tk
    dist_u = i - np.arange(wu)[None, :] + (seq - tq)
    strip = _bias_from_dist(rel_bias, dist_u, dist_u >= 0)
    ww = SLIDE_WIN + tq
    dist_w = i + SLIDE_WIN - np.arange(ww)[None, :]
    strip_w = _bias_from_dist(rel_bias, dist_w, (dist_w >= 0) & (dist_w < SLIDE_WIN))

    a_t = jnp.asarray(_imp_to_block_matrix(ncp, n_cmp, nb), BF16)
    e_b = jnp.asarray(np.kron(np.eye(nb), np.ones((1, SEL_BLOCK))), BF16)
    eye = jnp.eye(tq, dtype=BF16)

    half = lambda g: g // 2
    kv_spec = lambda w, off: pl.BlockSpec((1, seq, LANES), lambda g, b, qi: (b, 0, off + half(g)))
    full = lambda a: pl.BlockSpec(a.shape, lambda g, b, qi: (0,) * a.ndim)
    out = pl.pallas_call(
        functools.partial(_nsa_kernel, seq=seq, tq=tq, tk=tk),
        grid=(N_GROUPS, bsz, seq // tq),
        in_specs=[pl.BlockSpec((1, tq, GROUP_HEADS * HEAD_DIM), lambda g, b, qi: (b, qi, g)),
                  pl.BlockSpec((1, tq, LANES), lambda g, b, qi: (b, qi, g)),
                  pl.BlockSpec((1, ncp, LANES), lambda g, b, qi: (b, 0, half(g))),
                  pl.BlockSpec((1, ncp, LANES), lambda g, b, qi: (b, 0, half(g))),
                  kv_spec(kvpv, 4), kv_spec(kvpv, 6), kv_spec(kvwv, 0), kv_spec(kvwv, 2),
                  pl.BlockSpec((GROUP_HEADS, tq, ncp), lambda g, b, qi: (g, qi, 0)),
                  pl.BlockSpec((GROUP_HEADS, tq, wu), lambda g, b, qi: (g, 0, 0)),
                  pl.BlockSpec((GROUP_HEADS, tq, ww), lambda g, b, qi: (g, 0, 0)),
                  full(a_t), full(e_b), full(eye)],
        out_specs=pl.BlockSpec((1, tq, GROUP_HEADS * HEAD_DIM), lambda g, b, qi: (b, qi, g)),
        out_shape=jax.ShapeDtypeStruct((bsz, seq, d), F32),
        scratch_shapes=[pltpu.VMEM((tq, seq), F32),
                        pltpu.VMEM((GROUP_HEADS, tq, 1), F32),
                        pltpu.VMEM((GROUP_HEADS, tq, 1), F32),
                        pltpu.VMEM((GROUP_HEADS, tq, LANES), F32)],
        compiler_params=_cparams("parallel", "parallel", "arbitrary"),
        name="nsa_prompt_attn",
    )(qv, gv, kc, vc, kvpv, kvpv, kvwv, kvwv, bias_c, strip, strip_w, a_t, e_b, eye)
    return out.reshape(bsz * seq, d)


CMP_PAGES = 16


def _scmp_kernel(pt_ref, *refs):
    del pt_ref
    pages = refs[:CMP_PAGES]
    w1_ref, pe_ref, o_ref, x_scr = refs[CMP_PAGES:]
    cpp = pages[0].shape[1]
    gw = N_GROUPS * HEAD_DIM
    nh = N_GROUPS * CMP_HIDDEN
    for kind in range(2):
        for i, p_ref in enumerate(pages):
            for s in range(CMP_STRIDE):
                c0 = s * 4 * gw + kind * gw
                x_scr[s, i * cpp:(i + 1) * cpp, :] = p_ref[0, :, c0:c0 + gw]
        for part in range(CMP_LEN // CMP_STRIDE):
            acc = jnp.zeros((x_scr.shape[1], nh), F32)
            for s in range(CMP_STRIDE):
                x = (x_scr[s] + pe_ref[kind, part, s:s + 1, :]).astype(BF16)
                acc = acc + _dot(x, w1_ref[kind, part, s])
            o_ref[kind, 0, :, part * nh:(part + 1) * nh] = acc


def _scmp(cache, page_table, w1bd, pe):
    bsz, n_pages = page_table.shape
    n_pool, page, width = cache.shape
    cpp = page // CMP_STRIDE
    nh = N_GROUPS * CMP_HIDDEN
    n_chunks = n_pages * cpp
    chunked = cache.reshape(n_pool, cpp, CMP_STRIDE * width)
    page_spec = lambda i: pl.BlockSpec((1, cpp, CMP_STRIDE * width), lambda b, j, pt: (pt[b, j * CMP_PAGES + i], 0, 0))
    const = lambda a: pl.BlockSpec(a.shape, lambda b, j, pt: (0,) * a.ndim, pipeline_mode=pl.Buffered(1))
    grid_spec = pltpu.PrefetchScalarGridSpec(
        num_scalar_prefetch=1,
        grid=(bsz, n_pages // CMP_PAGES),
        in_specs=[page_spec(i) for i in range(CMP_PAGES)] + [const(w1bd), const(pe)],
        out_specs=pl.BlockSpec((2, 1, CMP_PAGES * cpp, 2 * nh), lambda b, j, pt: (0, b, j, 0)),
        scratch_shapes=[pltpu.VMEM((CMP_STRIDE, CMP_PAGES * cpp, N_GROUPS * HEAD_DIM), F32)])
    return pl.pallas_call(
        _scmp_kernel,
        grid_spec=grid_spec,
        out_shape=jax.ShapeDtypeStruct((2, bsz, n_chunks, 2 * nh), F32),
        compiler_params=_cparams("parallel", "arbitrary"),
        name="compress_sample",
    )(page_table, *([chunked] * CMP_PAGES), w1bd, pe)


def _s1_kernel(ab_ref, w2_ref, q_ref, bc_ref, at_ref, gs_ref, oc_ref, it_ref):
    nck = ab_ref.shape[2]
    nh = N_GROUPS * CMP_HIDDEN
    kv = []
    for kind in range(2):
        h = ab_ref[kind, 0, :, :nh] + pltpu.roll(ab_ref[kind, 0, :, nh:], nck - 1, 0)
        kv.append(_dot(jnp.maximum(h, 0.0).astype(BF16), w2_ref[kind]).astype(BF16))
    diag = _head_diag(N_HEADS, D_MODEL)
    qe = jnp.where(diag, q_ref[0] * ATTN_SCALE, 0.0).astype(BF16)
    bc = bc_ref[...]
    pc = _masked_softmax(_dot_nt(qe, kv[0]) + bc, bc > 0.5 * NEG)
    o = _dot(pc.astype(BF16), kv[1])
    oc_ref[0] = jnp.sum(jnp.where(diag, o, 0.0), axis=0, keepdims=True)
    hi = pc.astype(BF16)
    lo = (pc - hi.astype(F32)).astype(BF16)
    imp = _dot(gs_ref[...], hi) + _dot(gs_ref[...], lo)
    it_ref[0] = _split_dot_nt(at_ref[...], imp)


def _s1(ab, w2ex, q, rel_bias, past):
    _, bsz, nck, _ = ab.shape
    d = D_MODEL
    n_cmp = (past + 1 - CMP_LEN) // CMP_STRIDE + 1
    n_blk = -(-(past + 1) // SEL_BLOCK)
    nbp = -(-n_blk // 8) * 8
    n = np.arange(nck)
    dist_c = past - (n * CMP_STRIDE + CMP_LEN - 1)
    bias_c = _bias_from_dist(rel_bias, dist_c, (dist_c >= 0) & (n < n_cmp))
    a_t = jnp.asarray(_imp_to_block_matrix(nck, n_cmp, nbp)[:nbp] * (np.arange(nbp)[:, None] < n_blk), BF16)
    gsum = jnp.asarray(np.kron(np.eye(8, N_GROUPS), np.ones((1, GROUP_HEADS))), BF16)
    full = lambda a: pl.BlockSpec(a.shape, lambda b: (0,) * a.ndim)
    oc, imp_t = pl.pallas_call(
        _s1_kernel,
        grid=(bsz,),
        in_specs=[pl.BlockSpec((2, 1, nck, ab.shape[3]), lambda b: (0, b, 0, 0)), full(w2ex),
                  pl.BlockSpec((1, 1, d), lambda b: (b, 0, 0)), full(bias_c), full(a_t), full(gsum)],
        out_specs=[pl.BlockSpec((1, 1, d), lambda b: (b, 0, 0)), pl.BlockSpec((1, nbp, 8), lambda b: (b, 0, 0))],
        out_shape=[jax.ShapeDtypeStruct((bsz, 1, d), F32), jax.ShapeDtypeStruct((bsz, nbp, 8), F32)],
        compiler_params=_cparams("parallel"),
        name="sample_compressed_attn",
    )(ab, w2ex, q.reshape(bsz, 1, d), bias_c, a_t, gsum)
    return oc.reshape(bsz, d), imp_t


def _s2_kernel(imp_ref, idx_ref, s_scr, *, n_blk, cur):
    shape = imp_ref.shape
    j = lax.broadcasted_iota(jnp.int32, shape, 0)
    forced = jnp.logical_or(j == 0, jnp.logical_or(j == cur, j == cur - 1))
    score = jnp.where(forced, SEL_FORCE, jnp.where(j < n_blk, imp_ref[...], -2.0))
    s_scr[...] = score

    def body(i, rank):
        si = s_scr[pl.ds(i, 1), :]
        beats = jnp.logical_or(si > score, jnp.logical_and(si == score, j > i))
        return rank + jnp.where(beats, 1.0, 0.0)

    rank = lax.fori_loop(0, n_blk, body, jnp.zeros(shape, F32))
    for k in range(SEL_TOPK):
        idx_ref[k:k + 1, :] = jnp.sum(jnp.where(rank == k, j, 0), axis=0, keepdims=True)


def _s2(imp_t, past):
    n_blk = -(-(past + 1) // SEL_BLOCK)
    return pl.pallas_call(
        functools.partial(_s2_kernel, n_blk=n_blk, cur=past // SEL_BLOCK),
        out_shape=jax.ShapeDtypeStruct((SEL_TOPK, imp_t.shape[1]), jnp.int32),
        scratch_shapes=[pltpu.VMEM(imp_t.shape, F32)],
        name="sample_block_topk",
    )(imp_t)


def _bucket_thresholds(max_dist):
    b = _bucket_np(np.arange(max_dist + 1))
    return [int(np.argmax(b >= k)) if (b >= k).any() else max_dist + 1 for k in range(REL_BUCKETS)]


def _attend_rows(qm, keys, bias, valid, kn, vn_full, bias_new):
    s = _dot_nt(qm.astype(BF16), keys) + bias
    if valid is not None:
        s = jnp.where(valid, s, NEG)
    sn = jnp.sum(qm * kn, axis=-1, keepdims=True) + bias_new
    m = jnp.maximum(jnp.max(s, axis=-1, keepdims=True), sn)
    e = jnp.exp(s - m)
    en = jnp.exp(sn - m)
    l = jnp.sum(e, axis=-1, keepdims=True) + en
    return (_dot(e.astype(BF16), keys) + en * vn_full) / l


def _s3_kernel(idx_ref, row_ref, *refs, past, thresholds):
    del row_ref
    blocks = refs[:SEL_TOPK]
    (qm_ref, qw_ref, kvn_ref, kwn_ref, cw_ref, tb_ref, bw_ref, px_ref, oc_ref, ge_ref,
     o_ref, acc_ref, ow_ref) = refs[SEL_TOPK:]
    b = pl.program_id(0)
    g = pl.program_id(1)
    d = D_MODEL
    gw = N_GROUPS * HEAD_DIM
    diag = _head_diag(N_HEADS, d)
    row = lax.broadcasted_iota(jnp.int32, (N_HEADS, d), 0)

    def expand(o):
        return _split_dot(o, px_ref[...])

    @pl.when(g == 0)
    def _():
        kw = cw_ref[0].astype(BF16)
        qw = qw_ref[0, 0] * ATTN_SCALE
        o = _attend_rows(qw, kw, bw_ref[...], None, kwn_ref[0], kwn_ref[0], tb_ref[0][:, :1])
        ow_ref[...] = jnp.sum(jnp.where(diag, expand(o[:, gw:]), 0.0), axis=0, keepdims=True)
        acc_ref[...] = jnp.zeros_like(acc_ref)

    keys = jnp.concatenate([blk[0] for blk in blocks], axis=0).astype(BF16)
    nk = SEL_TOPK * SEL_BLOCK
    lane = lax.broadcasted_iota(jnp.int32, (1, nk), 1)
    pos = jnp.zeros((1, nk), jnp.int32)
    for k in range(SEL_TOPK):
        blk_idx = idx_ref[(b * N_GROUPS + g) * SEL_TOPK + k]
        pos = jnp.where(lane // SEL_BLOCK == k, blk_idx * SEL_BLOCK + lane % SEL_BLOCK, pos)
    dist = past - pos
    bias = jnp.broadcast_to(tb_ref[0][:, :1], (N_HEADS, nk))
    for k in range(1, REL_BUCKETS):
        bias = jnp.where(dist >= thresholds[k], tb_ref[k][:, :1], bias)
    qm = qm_ref[0, 0] * ATTN_SCALE
    o = _attend_rows(qm, keys, bias, pos < past, kvn_ref[0], kvn_ref[0], tb_ref[0][:, :1])
    mine = jnp.logical_and(diag, row // GROUP_HEADS == g)
    acc_ref[...] += jnp.sum(jnp.where(mine, expand(o[:, gw:]), 0.0), axis=0, keepdims=True)

    @pl.when(g == pl.num_programs(1) - 1)
    def _():
        o_ref[0] = ge_ref[0, 0] * oc_ref[0] + ge_ref[0, 1] * acc_ref[...] + ge_ref[0, 2] * ow_ref[...]


def _s3(sel_idx, cache, page_table, q, gate, kvp_new, kvw_new, cache_win, oc, rel_bias, past):
    bsz, d = q.shape
    gw = N_GROUPS * HEAD_DIM
    page = cache.shape[1]
    bpp = page // SEL_BLOCK
    cur = past // SEL_BLOCK
    cached = jnp.minimum(sel_idx, cur - 1)
    rows = jnp.take_along_axis(page_table, (cached // bpp).reshape(bsz, -1), axis=1).reshape(sel_idx.shape)
    rows = (rows * bpp + cached % bpp).reshape(-1)
    cache_blocks = cache.reshape(cache.shape[0] * bpp, SEL_BLOCK, cache.shape[2])

    q3 = q.reshape(bsz, N_HEADS, HEAD_DIM)
    onehot = jnp.asarray(np.kron(np.eye(N_GROUPS), np.ones((GROUP_HEADS, 1))), F32)
    qw = jnp.einsum("bhd,hg->bhgd", q3, onehot).reshape(bsz, N_HEADS, gw)
    qw = jnp.pad(qw, ((0, 0), (0, 0), (0, gw)))
    grp = jnp.asarray(np.eye(N_GROUPS), F32)
    qm = jnp.einsum("bhc,hg,gG->bGhc", qw, onehot, grp)

    tb = jnp.broadcast_to(rel_bias.astype(F32)[:, :, None], (REL_BUCKETS, N_HEADS, LANES))
    wlen = cache_win.shape[1]
    dist_w = wlen - np.arange(wlen)
    bw = _bias_from_dist(rel_bias, dist_w, dist_w < SLIDE_WIN)
    px = jnp.asarray(np.kron(np.kron(np.eye(N_GROUPS), np.ones((1, GROUP_HEADS))), np.eye(HEAD_DIM)), BF16)
    ge = jnp.repeat(jnp.transpose(gate.reshape(bsz, N_HEADS, 3), (0, 2, 1)), HEAD_DIM, axis=2)
    ge = ge.reshape(bsz, 3, 1, d)

    blk_spec = lambda k: pl.BlockSpec(
        (1, SEL_BLOCK, 2 * gw), lambda b, g, idx, row: (row[(b * N_GROUPS + g) * SEL_TOPK + k], 0, 1))
    full = lambda a: pl.BlockSpec(a.shape, lambda b, g, idx, row: (0,) * a.ndim)
    per_b = lambda *shape: pl.BlockSpec((1,) + shape, lambda b, g, idx, row: (b,) + (0,) * len(shape))
    grid_spec = pltpu.PrefetchScalarGridSpec(
        num_scalar_prefetch=2,
        grid=(bsz, N_GROUPS),
        in_specs=[blk_spec(k) for k in range(SEL_TOPK)] + [
            pl.BlockSpec((1, 1, N_HEADS, 2 * gw), lambda b, g, idx, row: (b, g, 0, 0)),
            per_b(1, N_HEADS, 2 * gw),
            pl.BlockSpec((1, 1, 2 * gw), lambda b, g, idx, row: (b, 0, 1)),
            per_b(1, 2 * gw), per_b(wlen, 2 * gw), full(tb), full(bw), full(px),
            per_b(1, d), per_b(3, 1, d)],
        out_specs=per_b(1, d),
        scratch_shapes=[pltpu.VMEM((1, d), F32), pltpu.VMEM((1, d), F32)])
    out = pl.pallas_call(
        functools.partial(_s3_kernel, past=past, thresholds=tuple(_bucket_thresholds(past))),
        grid_spec=grid_spec,
        out_shape=jax.ShapeDtypeStruct((bsz, 1, d), F32),
        compiler_params=_cparams("parallel", "arbitrary"),
        name="sample_selected_window_attn",
    )(sel_idx.reshape(-1), rows, *([cache_blocks] * SEL_TOPK), qm, qw.reshape(bsz, 1, N_HEADS, 2 * gw),
      kvp_new.reshape(bsz, 1, 4 * gw), kvw_new.reshape(bsz, 1, 2 * gw), cache_win, tb, bw, px,
      oc.reshape(bsz, 1, d), ge)
    return out.reshape(bsz, d)


def _split_a_weights(w_in):
    d = D_MODEL
    w = w_in.astype(BF16)
    qs = [w[:, g * 3 * d:g * 3 * d + d] for g in range(len(DIL_PATTERNS))]
    kvs = [w[:, g * 3 * d + d:(g + 1) * 3 * d] for g in range(len(DIL_PATTERNS))]
    return qs + kvs


def _split_b_weights(w_in):
    d = D_MODEL
    gw = N_GROUPS * HEAD_DIM
    w = w_in.astype(BF16)
    wg = w[:, d + 6 * gw:].reshape(d, N_GROUPS, GROUP_HEADS * 3)
    wg = jnp.pad(wg, ((0, 0), (0, 0), (0, LANES - GROUP_HEADS * 3))).reshape(d, N_GROUPS * LANES)
    return [w[:, :d], w[:, d:d + 4 * gw], w[:, d + 4 * gw:d + 6 * gw], wg]


def _unblock_gate(gate):
    m = gate.shape[0]
    return gate.reshape(m, N_GROUPS, LANES)[:, :, :GROUP_HEADS * 3].reshape(m, N_HEADS * 3)


def kernel(x_prompt, x_sample, cache_a_win0, cache_a_win1, cache_a_win2, cache_b_kv, cache_b_win, page_table,
           rel_bias, norm_mix, norm_mlp, norm_final, a_w_in, a_w_out, b_w_in, b_w_out, b_phi_pe, b_phi_w1,
           b_phi_w2, mlp_w_up, mlp_w_down):
    bsz, seq, d = x_prompt.shape
    sb = x_sample.shape[0]
    past = page_table.shape[1] * cache_b_kv.shape[2]
    tm_p, tm_s = 512, sb
    xp = x_prompt.reshape(bsz * seq, d)
    xs = x_sample.reshape(sb, d)
    acts4 = ("none",) * 4
    acts6 = ("none",) * 6

    wa = _split_a_weights(a_w_in[0])
    wa_out = a_w_out[0].astype(BF16)
    pa = _norm_matmul(xp, norm_mix[0], wa, acts6, tm_p, 4)
    sa = _norm_matmul(xs, norm_mix[0], wa, acts6, tm_s, 4)
    os_, ls = [], []
    for gi, (_, dil) in enumerate(DIL_PATTERNS):
        o, lse = _swa_group(pa[gi], pa[3 + gi], rel_bias, bsz, seq, dil)
        os_.append(o)
        ls.append(lse)
    xp = _merge_proj(os_, ls, xp, wa_out, tm_p)
    a_caches = [c[0].reshape(sb, c.shape[2], 2 * d) for c in (cache_a_win0, cache_a_win1, cache_a_win2)]
    oa = _samp_a(sa[:3], sa[3:], a_caches, rel_bias)
    xs = _matmul_res(oa, xs, wa_out, tm_s)
    a_new_p = [pa[3 + gi].reshape(bsz, seq, 2 * d)[:, seq - min(w, seq):] for gi, (w, _) in enumerate(DIL_PATTERNS)]
    a_new_s = [_shift_append(c, kv) for c, kv in zip(a_caches, sa[3:])]

    wu0, wd0 = mlp_w_up[0].astype(BF16), mlp_w_down[0].astype(BF16)
    xp = _mlp(xp, norm_mlp[0], wu0, wd0, norm_final, False, tm_p, 1024)
    xs = _mlp(xs, norm_mlp[0], wu0, wd0, norm_final, False, tm_s, 1024)

    wb = _split_b_weights(b_w_in[0])
    wb_out = b_w_out[0].astype(BF16)
    acts_b = ("none", "none", "none", "sigmoid")
    qp, kvp_p, kvw_p, gate_p = _norm_matmul(xp, norm_mix[1], wb, acts_b, tm_p, 1)
    qs_, kvp_s, kvw_s, gate_s = _norm_matmul(xs, norm_mix[1], wb, acts_b, tm_s, 1)
    w1bd, w2bd, w2ex, pe = _cmp_weights(b_phi_pe[0], b_phi_w1[0], b_phi_w2[0])
    kc = _cmp_prompt(kvp_p, bsz, seq, 0, w1bd, pe, w2bd)
    vc = _cmp_prompt(kvp_p, bsz, seq, 1, w1bd, pe, w2bd)
    ob = _nsa_prompt(qp, gate_p, kc, vc, kvp_p, kvw_p, rel_bias, bsz, seq)
    xp = _matmul_res(ob, xp, wb_out, tm_p)

    cache = cache_b_kv[0].reshape(cache_b_kv.shape[1], cache_b_kv.shape[2], -1)
    cwin = cache_b_win[0].reshape(sb, cache_b_win.shape[2], -1)
    ab = _scmp(cache, page_table, w1bd, pe)
    oc, imp_t = _s1(ab, w2ex, qs_, rel_bias, past)
    imp_t = jnp.transpose(imp_t[:, :, :N_GROUPS], (1, 0, 2)).reshape(imp_t.shape[1], sb * N_GROUPS)
    sel_idx = jnp.transpose(_s2(imp_t, past)).reshape(sb, N_GROUPS, SEL_TOPK)
    obs = _s3(sel_idx, cache, page_table, qs_, _unblock_gate(gate_s), kvp_s, kvw_s, cwin, oc, rel_bias, past)
    xs = _matmul_res(obs, xs, wb_out, tm_s)
    b_win_s = _shift_append(cwin, kvw_s)

    wu1, wd1 = mlp_w_up[1].astype(BF16), mlp_w_down[1].astype(BF16)
    yp = _mlp(xp, norm_mlp[1], wu1, wd1, norm_final, True, tm_p, 1024)
    ys = _mlp(xs, norm_mlp[1], wu1, wd1, norm_final, True, tm_s, 1024)

    gw = N_GROUPS * HEAD_DIM
    keep = min(SLIDE_WIN, seq)
    cache_shape = lambda n: (1, -1, n, 2, N_HEADS, HEAD_DIM)
    outs = [yp.reshape(bsz, seq, d), ys.reshape(sb, 1, d)]
    for p, s in zip(a_new_p, a_new_s):
        outs.append(p.reshape(cache_shape(p.shape[1])))
        outs.append(s.reshape(cache_shape(s.shape[1])))
    outs.append(kvp_p.reshape(1, bsz, seq, 4, N_GROUPS, HEAD_DIM))
    outs.append(kvp_s.reshape(1, sb, 1, 4, N_GROUPS, HEAD_DIM))
    outs.append(kvw_p.reshape(bsz, seq, 2 * gw)[:, seq - keep:].reshape(1, bsz, keep, 2, N_GROUPS, HEAD_DIM))
    outs.append(b_win_s.reshape(1, sb, -1, 2, N_GROUPS, HEAD_DIM))
    return tuple(outs)
```

```python
import functools
import math

import numpy as np
import jax
import jax.numpy as jnp
from jax import lax
from jax.experimental import pallas as pl
from jax.experimental.pallas import tpu as pltpu

F32 = jnp.float32
BF16 = jnp.bfloat16

D_MODEL = 1024
HEAD_DIM = 64
N_HEADS = 16
ATTN_SCALE = HEAD_DIM ** -0.5
RMS_EPS = 1e-6
REL_BUCKETS = 32
REL_MAX_DIST = 2048
DIL_PATTERNS = ((128, 1), (512, 4), (2048, 16))
A_KEYS = 128
N_GROUPS = 4
GROUP_HEADS = N_HEADS // N_GROUPS
CMP_LEN = 32
CMP_STRIDE = 16
CMP_HIDDEN = 128
SEL_BLOCK = 64
SEL_TOPK = 16
SEL_FORCE = 1e4
SLIDE_WIN = 512

LANES = 128
VMEM_LIMIT = 56 * 1024 * 1024
NEG = -1e30

NT_DIMS = (((1,), (1,)), ((), ()))


def _cparams(*sem):
    return pltpu.CompilerParams(dimension_semantics=sem, vmem_limit_bytes=VMEM_LIMIT)


def _dot(a, b):
    return jnp.dot(a, b, preferred_element_type=F32)


def _dot_nt(a, b):
    return lax.dot_general(a, b, NT_DIMS, preferred_element_type=F32)


def _split_dot(x, w):
    hi = x.astype(BF16)
    lo = (x - hi.astype(F32)).astype(BF16)
    return _dot(hi, w) + _dot(lo, w)


def _split_dot_nt(w, x):
    hi = x.astype(BF16)
    lo = (x - hi.astype(F32)).astype(BF16)
    return _dot_nt(w, hi) + _dot_nt(w, lo)


def _rms(x, g):
    ms = jnp.mean(x * x, axis=-1, keepdims=True)
    return x * lax.rsqrt(ms + RMS_EPS) * g


def _bucket_np(dist):
    dist = np.maximum(np.asarray(dist, np.int64), 0)
    max_exact = REL_BUCKETS // 2
    ratio = (np.log(np.maximum(dist, 1).astype(np.float32) / np.float32(max_exact))
             / np.float32(math.log(REL_MAX_DIST / max_exact)))
    large = np.minimum(max_exact + (ratio * (REL_BUCKETS - max_exact)).astype(np.int32), REL_BUCKETS - 1)
    return np.where(dist < max_exact, dist, large).astype(np.int32)


def _bias_from_dist(rel_bias, dist, valid):
    idx = jnp.asarray(_bucket_np(dist))
    b = jnp.take(rel_bias.astype(F32).T, idx, axis=1)
    return jnp.where(jnp.asarray(valid)[None], b, NEG)


def _norm_matmul_kernel(x_ref, g_ref, *refs, n_w, acts):
    w_refs, o_refs, xn_ref = refs[:n_w], refs[n_w:2 * n_w], refs[2 * n_w]

    @pl.when(pl.program_id(1) == 0)
    def _():
        xn_ref[...] = _rms(x_ref[...], g_ref[...]).astype(BF16)

    xn = xn_ref[...]
    for w_ref, o_ref, act in zip(w_refs, o_refs, acts):
        y = _dot(xn, w_ref[...])
        if act == "sigmoid":
            y = jax.nn.sigmoid(y)
        o_ref[...] = y


def _norm_matmul(x, g, ws, acts, tm, nj):
    m, d = x.shape
    kern = functools.partial(_norm_matmul_kernel, n_w=len(ws), acts=tuple(acts))
    in_specs = [pl.BlockSpec((tm, d), lambda i, j: (i, 0)), pl.BlockSpec((1, d), lambda i, j: (0, 0))]
    in_specs += [pl.BlockSpec((d, w.shape[1] // nj), lambda i, j: (0, j)) for w in ws]
    out_specs = [pl.BlockSpec((tm, w.shape[1] // nj), lambda i, j: (i, j)) for w in ws]
    return pl.pallas_call(
        kern,
        grid=(m // tm, nj),
        in_specs=in_specs,
        out_specs=out_specs,
        out_shape=[jax.ShapeDtypeStruct((m, w.shape[1]), F32) for w in ws],
        scratch_shapes=[pltpu.VMEM((tm, d), BF16)],
        compiler_params=_cparams("parallel", "arbitrary"),
        name="norm_matmul",
    )(x, g.reshape(1, d), *ws)


def _matmul_res_kernel(a_ref, x_ref, w_ref, o_ref):
    o_ref[...] = x_ref[...] + _dot(a_ref[...].astype(BF16), w_ref[...])


def _matmul_res(a, x, w, tm):
    m, d = x.shape
    return pl.pallas_call(
        _matmul_res_kernel,
        grid=(m // tm,),
        in_specs=[pl.BlockSpec((tm, a.shape[1]), lambda i: (i, 0)),
                  pl.BlockSpec((tm, d), lambda i: (i, 0)),
                  pl.BlockSpec(w.shape, lambda i: (0, 0))],
        out_specs=pl.BlockSpec((tm, d), lambda i: (i, 0)),
        out_shape=jax.ShapeDtypeStruct((m, d), F32),
        compiler_params=_cparams("parallel"),
        name="matmul_res",
    )(a, x, w)


def _merge_proj_kernel(o0, o1, o2, l0, l1, l2, e_ref, x_ref, w_ref, out_ref):
    ls = [l0[...], l1[...], l2[...]]
    mx = jnp.maximum(jnp.maximum(ls[0], ls[1]), ls[2])
    es = [jnp.exp(l - mx) for l in ls]
    tot = es[0] + es[1] + es[2]
    merged = None
    for e, o in zip(es, (o0, o1, o2)):
        part = _split_dot(e / tot, e_ref[...]) * o[...]
        merged = part if merged is None else merged + part
    out_ref[...] = x_ref[...] + _dot(merged.astype(BF16), w_ref[...])


def _merge_proj(os_, ls, x, w, tm):
    m, d = x.shape
    expand = jnp.asarray(np.kron(np.eye(N_HEADS), np.ones((1, HEAD_DIM))), BF16)
    row = lambda n: pl.BlockSpec((tm, n), lambda i: (i, 0))
    return pl.pallas_call(
        _merge_proj_kernel,
        grid=(m // tm,),
        in_specs=[row(d)] * 3 + [row(N_HEADS)] * 3 + [
            pl.BlockSpec(expand.shape, lambda i: (0, 0)), row(d), pl.BlockSpec(w.shape, lambda i: (0, 0))],
        out_specs=row(d),
        out_shape=jax.ShapeDtypeStruct((m, d), F32),
        compiler_params=_cparams("parallel"),
        name="merge_proj",
    )(*os_, *ls, expand, x, w)


def _mlp_kernel(x_ref, g_ref, wu_ref, wd_ref, gf_ref, o_ref, xn_ref, acc_ref, *, final_norm):
    f = pl.program_id(1)

    @pl.when(f == 0)
    def _():
        xn_ref[...] = _rms(x_ref[...], g_ref[...]).astype(BF16)
        acc_ref[...] = jnp.zeros_like(acc_ref)

    h = jnp.square(jnp.maximum(_dot(xn_ref[...], wu_ref[...]), 0.0))
    acc_ref[...] += _dot(h.astype(BF16), wd_ref[...])

    @pl.when(f == pl.num_programs(1) - 1)
    def _():
        y = x_ref[...] + acc_ref[...]
        if final_norm:
            y = _rms(y, gf_ref[...])
        o_ref[...] = y


def _mlp(x, g, wu, wd, gf, final_norm, tm, tf):
    m, d = x.shape
    dff = wu.shape[1]
    return pl.pallas_call(
        functools.partial(_mlp_kernel, final_norm=final_norm),
        grid=(m // tm, dff // tf),
        in_specs=[pl.BlockSpec((tm, d), lambda i, f: (i, 0)),
                  pl.BlockSpec((1, d), lambda i, f: (0, 0)),
                  pl.BlockSpec((d, tf), lambda i, f: (0, f)),
                  pl.BlockSpec((tf, d), lambda i, f: (f, 0)),
                  pl.BlockSpec((1, d), lambda i, f: (0, 0))],
        out_specs=pl.BlockSpec((tm, d), lambda i, f: (i, 0)),
        out_shape=jax.ShapeDtypeStruct((m, d), F32),
        scratch_shapes=[pltpu.VMEM((tm, d), BF16), pltpu.VMEM((tm, d), F32)],
        compiler_params=_cparams("parallel", "arbitrary"),
        name="mlp",
    )(x, g.reshape(1, d), wu, wd, gf.reshape(1, d))


def _swa_kernel(q_ref, kc_ref, kp_ref, vc_ref, vp_ref, b_ref, o_ref, lse_ref):
    t = q_ref.shape[1]
    lane = lax.broadcasted_iota(jnp.int32, (t, LANES), 1)
    low = lane < HEAD_DIM
    hcol = lax.broadcasted_iota(jnp.int32, (t, N_HEADS), 1)
    lse_all = jnp.zeros((t, N_HEADS), F32)
    for hp in range(N_HEADS // 2):
        sl = slice(hp * LANES, (hp + 1) * LANES)
        q2 = (q_ref[0, :, sl] * ATTN_SCALE).astype(BF16)
        k2 = jnp.concatenate([kp_ref[0, :, sl], kc_ref[0, :, sl]], axis=0).astype(BF16)
        v2 = jnp.concatenate([vp_ref[0, :, sl], vc_ref[0, :, sl]], axis=0).astype(BF16)
        outs = []
        for a in range(2):
            h = 2 * hp + a
            qm = jnp.where(low if a == 0 else jnp.logical_not(low), q2, jnp.zeros_like(q2))
            s = _dot_nt(qm, k2) + b_ref[0, h]
            m = jnp.max(s, axis=-1, keepdims=True)
            e = jnp.exp(s - m)
            l = jnp.sum(e, axis=-1, keepdims=True)
            outs.append(_dot(e.astype(BF16), v2) / l)
            lse_all = jnp.where(hcol == h, m + jnp.log(l), lse_all)
        o_ref[0, :, sl] = jnp.where(low, outs[0], outs[1])
    lse_ref[0, 0] = lse_all


def _swa_bias(rel_bias, dil, t):
    i = np.arange(t)[:, None]
    c = np.arange(2 * t)[None, :]
    prev = c < t
    dist = np.where(prev, t + i - c, i - (c - t))
    valid = np.where(prev, c >= i, (c - t) <= i)
    later = _bias_from_dist(rel_bias, dil * dist, valid)
    first = _bias_from_dist(rel_bias, dil * dist, valid & ~prev)
    return jnp.stack([first, later])


def _swa_group(q, kv, rel_bias, bsz, seq, dil):
    t = A_KEYS
    ln = seq // dil
    d = D_MODEL
    qv = q.reshape(bsz, ln, dil * d)
    kvv = kv.reshape(bsz, ln, dil * 2 * d)
    bias = _swa_bias(rel_bias, dil, t)
    prev = lambda qi: jnp.maximum(qi - 1, 0)
    o, lse = pl.pallas_call(
        _swa_kernel,
        grid=(bsz, dil, ln // t),
        in_specs=[pl.BlockSpec((1, t, d), lambda b, r, qi: (b, qi, r)),
                  pl.BlockSpec((1, t, d), lambda b, r, qi: (b, qi, 2 * r)),
                  pl.BlockSpec((1, t, d), lambda b, r, qi: (b, prev(qi), 2 * r)),
                  pl.BlockSpec((1, t, d), lambda b, r, qi: (b, qi, 2 * r + 1)),
                  pl.BlockSpec((1, t, d), lambda b, r, qi: (b, prev(qi), 2 * r + 1)),
                  pl.BlockSpec((1, N_HEADS, t, 2 * t), lambda b, r, qi: (jnp.minimum(qi, 1), 0, 0, 0))],
        out_specs=[pl.BlockSpec((1, t, d), lambda b, r, qi: (b, qi, r)),
                   pl.BlockSpec((1, 1, t, N_HEADS), lambda b, r, qi: (b, r, qi, 0))],
        out_shape=[jax.ShapeDtypeStruct((bsz, ln, dil * d), F32),
                   jax.ShapeDtypeStruct((bsz, dil, ln, N_HEADS), F32)],
        compiler_params=_cparams("parallel", "parallel", "arbitrary"),
        name="swa_attn",
    )(qv, kvv, kvv, kvv, kvv, bias)
    lse = jnp.transpose(lse, (0, 2, 1, 3)).reshape(bsz * seq, N_HEADS)
    return o.reshape(bsz * seq, d), lse


def _head_diag(rows, width):
    row = lax.broadcasted_iota(jnp.int32, (rows, width), 0)
    lane = lax.broadcasted_iota(jnp.int32, (rows, width), 1)
    return (lane // HEAD_DIM) == row


def _rows_last(cache):
    nd = cache.ndim
    t = jnp.transpose(cache, (0,) + tuple(range(2, nd)) + (1,))
    bsz, window = cache.shape[0], cache.shape[1]
    lead = int(np.prod(cache.shape[2:nd - 2]))
    return t.reshape(bsz, lead, cache.shape[nd - 2] * cache.shape[nd - 1], window)


def _rows_second(cache_t, feat):
    bsz, _, _, window = cache_t.shape
    nd = len(feat) + 2
    t = cache_t.reshape((bsz,) + tuple(feat) + (window,))
    return jnp.transpose(t, (0, nd - 1) + tuple(range(1, nd - 1)))


def _shift_in(x, col):
    w = x.shape[1]
    lane = lax.broadcasted_iota(jnp.int32, x.shape, 1)
    return jnp.where(lane == w - 1, col, pltpu.roll(x, w - 1, 1))


A_HEAD_CHUNK = 4


def _samp_a_kernel(q0, q1, q2, n0, n1, n2, m0, m1, m2, c0, c1, c2, b0, b1, b2, bs_ref, o_ref, u0, u1, u2):
    hw = A_HEAD_CHUNK * HEAD_DIM
    diag = _head_diag(8, hw)
    outs, lses = [], []
    for q_ref, n_ref, m_ref, c_ref, b_ref, u_ref in (
            (q0, n0, m0, c0, b0, u0), (q1, n1, m1, c1, b1, u1), (q2, n2, m2, c2, b2, u2)):
        qe = jnp.where(diag, q_ref[0] * ATTN_SCALE, 0.0)
        kt = c_ref[0, 0]
        vt = c_ref[0, 1]
        kn = n_ref[0, 0:1, :]
        vn = n_ref[0, 1:2, :]
        s = _dot(qe.astype(BF16), kt.astype(BF16)) + b_ref[0]
        sn = jnp.sum(qe * kn, axis=-1, keepdims=True) + bs_ref[0][:, :1]
        m = jnp.maximum(jnp.max(s, axis=-1, keepdims=True), sn)
        e = jnp.exp(s - m)
        en = jnp.exp(sn - m)
        l = jnp.sum(e, axis=-1, keepdims=True) + en
        outs.append((_dot_nt(e.astype(BF16), vt.astype(BF16)) + en * vn) / l)
        lses.append(m + jnp.log(l))
        u_ref[0, 0] = _shift_in(kt, m_ref[0, 0])
        u_ref[0, 1] = _shift_in(vt, m_ref[0, 1])
    mx = jnp.maximum(jnp.maximum(lses[0], lses[1]), lses[2])
    es = [jnp.exp(l - mx) for l in lses]
    tot = es[0] + es[1] + es[2]
    merged = (es[0] / tot) * outs[0] + (es[1] / tot) * outs[1] + (es[2] / tot) * outs[2]
    o_ref[0] = jnp.sum(jnp.where(diag, merged, 0.0), axis=0, keepdims=True)


def _samp_a(qs, kvs, caches_t, rel_bias):
    bsz, d = qs[0].shape
    hw = A_HEAD_CHUNK * HEAD_DIM
    nhc = N_HEADS // A_HEAD_CHUNK
    pad_rows = lambda a: jnp.pad(a.reshape(nhc, A_HEAD_CHUNK, -1), ((0, 0), (0, 8 - A_HEAD_CHUNK), (0, 0)))
    biases = []
    for window, dil in DIL_PATTERNS:
        dist = window - np.arange(window)
        biases.append(pad_rows(_bias_from_dist(rel_bias, dist, dist % dil == 0)))
    bself = pad_rows(_bias_from_dist(rel_bias, np.zeros((LANES,), np.int64), np.ones((LANES,), bool)))
    q_spec = pl.BlockSpec((1, 1, hw), lambda b, c: (b, 0, c))
    n_spec = pl.BlockSpec((1, 2, hw), lambda b, c: (b, 0, c))
    m_spec = pl.BlockSpec((1, 2, hw, 1), lambda b, c: (b, 0, c, 0))
    c_spec = lambda w: pl.BlockSpec((1, 2, hw, w), lambda b, c: (b, 0, c, 0))
    b_spec = lambda w: pl.BlockSpec((1, 8, w), lambda b, c: (c, 0, 0))
    windows = [w for w, _ in DIL_PATTERNS]
    res = pl.pallas_call(
        _samp_a_kernel,
        grid=(bsz, nhc),
        in_specs=[q_spec] * 3 + [n_spec] * 3 + [m_spec] * 3 + [c_spec(w) for w in windows]
        + [b_spec(w) for w in windows] + [b_spec(LANES)],
        out_specs=[q_spec] + [c_spec(w) for w in windows],
        out_shape=[jax.ShapeDtypeStruct((bsz, 1, d), F32)] + [jax.ShapeDtypeStruct(c.shape, F32) for c in caches_t],
        compiler_params=_cparams("parallel", "parallel"),
        name="sample_dilated_attn",
    )(*[q.reshape(bsz, 1, d) for q in qs], *[kv.reshape(bsz, 2, d) for kv in kvs],
      *[kv.reshape(bsz, 2, d, 1) for kv in kvs], *caches_t, *biases, bself)
    return res[0].reshape(bsz, d), res[1:]


def _cmp_weights(phi_pe, phi_w1, phi_w2):
    r = CMP_LEN // CMP_STRIDE
    eye = jnp.eye(N_GROUPS, dtype=F32)
    w1r = phi_w1.reshape(2, r, CMP_STRIDE, HEAD_DIM, CMP_HIDDEN)
    w1bd = jnp.einsum("kpsdh,gG->kpsgdGh", w1r, eye).reshape(
        2, r, CMP_STRIDE, N_GROUPS * HEAD_DIM, N_GROUPS * CMP_HIDDEN).astype(BF16)
    w2bd = jnp.einsum("khd,gG->kghGd", phi_w2, eye).reshape(
        2, N_GROUPS * CMP_HIDDEN, N_GROUPS * HEAD_DIM).astype(BF16)
    w2ex = jnp.einsum("khd,gG,r->kghGrd", phi_w2, eye, jnp.ones((GROUP_HEADS,), F32)).reshape(
        2, N_GROUPS * CMP_HIDDEN, D_MODEL).astype(BF16)
    pe = jnp.tile(phi_pe.reshape(2, r, CMP_STRIDE, 1, HEAD_DIM), (1, 1, 1, N_GROUPS, 1)).reshape(
        2, r, CMP_STRIDE, N_GROUPS * HEAD_DIM)
    return w1bd, w2bd, w2ex, pe


def _cmp_prompt_kernel(x_ref, w1_ref, pe_ref, w2_ref, o_ref, *, kind):
    nch = x_ref.shape[1]
    gw = N_GROUPS * HEAD_DIM
    hid = []
    for part in range(CMP_LEN // CMP_STRIDE):
        acc = jnp.zeros((nch, N_GROUPS * CMP_HIDDEN), F32)
        for s in range(CMP_STRIDE):
            c0 = s * 4 * gw + kind * gw
            x = (x_ref[0, :, c0:c0 + gw] + pe_ref[part, s:s + 1, :]).astype(BF16)
            acc = acc + _dot(x, w1_ref[part, s])
        hid.append(acc)
    h = hid[0] + pltpu.roll(hid[1], nch - 1, 0)
    o_ref[0] = _dot(jnp.maximum(h, 0.0).astype(BF16), w2_ref[...])


def _cmp_prompt(kvp, bsz, seq, kind, w1bd, pe, w2bd):
    nch = seq // CMP_STRIDE
    xv = kvp.reshape(bsz, nch, CMP_STRIDE * kvp.shape[1])
    full = lambda a: pl.BlockSpec(a.shape, lambda b: (0,) * a.ndim)
    w1, p, w2 = w1bd[kind], pe[kind], w2bd[kind]
    return pl.pallas_call(
        functools.partial(_cmp_prompt_kernel, kind=kind),
        grid=(bsz,),
        in_specs=[pl.BlockSpec((1, nch, xv.shape[2]), lambda b: (b, 0, 0)), full(w1), full(p), full(w2)],
        out_specs=pl.BlockSpec((1, nch, w2.shape[1]), lambda b: (b, 0, 0)),
        out_shape=jax.ShapeDtypeStruct((bsz, nch, w2.shape[1]), F32),
        compiler_params=_cparams("parallel"),
        name="compress_prompt",
    )(xv, w1, p, w2)


def _masked_softmax(s, valid):
    m = jnp.max(s, axis=-1, keepdims=True)
    m = jnp.where(m > 0.5 * NEG, m, 0.0)
    e = jnp.where(valid, jnp.exp(s - m), 0.0)
    return e / jnp.maximum(jnp.sum(e, axis=-1, keepdims=True), 1e-30)


def _topk_mask_t(score, j, n_iter):
    rank = jnp.zeros(score.shape, F32)
    for i in range(n_iter):
        si = score[i:i + 1, :]
        beats = jnp.logical_or(si > score, jnp.logical_and(si == score, j > i))
        rank = rank + jnp.where(beats, 1.0, 0.0)
    return jnp.where(rank < SEL_TOPK, 1.0, 0.0)


def _nsa_kernel(q_ref, gate_ref, kc_ref, vc_ref, ks_ref, vs_ref, kw_ref, vw_ref, bc_ref, u_ref, uw_ref,
                at_ref, eb_ref, eye_ref, o_ref, selb_ref, m_ref, l_ref, acc_ref, *, seq, tq, tk):
    g = pl.program_id(0)
    qi = pl.program_id(2)
    q0 = qi * tq
    par = g % 2
    nb = seq // SEL_BLOCK
    lane = lax.broadcasted_iota(jnp.int32, (tq, LANES), 1)
    keep = (lane // HEAD_DIM) == par
    low = lane < HEAD_DIM

    qs = []
    for r in range(GROUP_HEADS):
        qh = q_ref[0, :, (r // 2) * LANES:(r // 2 + 1) * LANES] * ATTN_SCALE
        qh = jnp.where(par == (r % 2), qh, pltpu.roll(qh, HEAD_DIM, 1))
        qs.append(jnp.where(keep, qh, 0.0).astype(BF16))
    q4 = jnp.concatenate(qs, axis=0)

    bc = bc_ref[...].reshape(GROUP_HEADS * tq, LANES)
    pc = _masked_softmax(_dot_nt(q4, kc_ref[0].astype(BF16)) + bc, bc > 0.5 * NEG)
    oc = _dot(pc.astype(BF16), vc_ref[0].astype(BF16))
    imp = pc[0:tq] + pc[tq:2 * tq] + pc[2 * tq:3 * tq] + pc[3 * tq:4 * tq]

    imp_t = _split_dot_nt(at_ref[...], imp)
    j = lax.broadcasted_iota(jnp.int32, (nb, tq), 0)
    qpos = q0 + lax.broadcasted_iota(jnp.int32, (nb, tq), 1)
    cur = qpos // SEL_BLOCK
    forced = jnp.logical_or(j == 0, jnp.logical_or(j == cur, j == cur - 1))
    score = jnp.where(forced, SEL_FORCE, jnp.where(j * SEL_BLOCK <= qpos, imp_t, -1.0))
    sel_t = _topk_mask_t(score, j, nb).astype(BF16)
    sel = _dot_nt(eye_ref[...], sel_t)
    selb_ref[...] = (_dot(sel.astype(BF16), eb_ref[...]) - 1.0) * (-NEG)

    def flash(lo, hi, tile, kv_tile, bias_tile):
        m_ref[...] = jnp.full(m_ref.shape, NEG, F32)
        l_ref[...] = jnp.zeros(l_ref.shape, F32)
        acc_ref[...] = jnp.zeros(acc_ref.shape, F32)

        def body(t, carry):
            k2, v2 = kv_tile(t)
            for r in range(GROUP_HEADS):
                s = _dot_nt(qs[r], k2) + bias_tile(t, r)
                m_prev = m_ref[r]
                m_new = jnp.maximum(m_prev, jnp.max(s, axis=-1, keepdims=True))
                alpha = jnp.exp(m_prev - m_new)
                e = jnp.exp(s - m_new)
                l_ref[r] = alpha * l_ref[r] + jnp.sum(e, axis=-1, keepdims=True)
                acc_ref[r] = alpha * acc_ref[r] + _dot(e.astype(BF16), v2)
                m_ref[r] = m_new
            return carry

        lax.fori_loop(lo, hi, body, 0)
        return [acc_ref[r] / l_ref[r] for r in range(GROUP_HEADS)]

    def sel_kv(t):
        k0 = pl.multiple_of(t * tk, tk)
        return ks_ref[0, pl.ds(k0, tk), :].astype(BF16), vs_ref[0, pl.ds(k0, tk), :].astype(BF16)

    def sel_bias(t, r):
        k0 = pl.multiple_of(t * tk, tk)
        u0 = pl.multiple_of(k0 - q0 + (seq - tq), LANES)
        return u_ref[r, :, pl.ds(u0, tk)] + selb_ref[:, pl.ds(k0, tk)]

    osel = flash(0, (q0 + tq + tk - 1) // tk, tk, sel_kv, sel_bias)

    def win_kv(t):
        w0 = pl.multiple_of(q0 - SLIDE_WIN + t * tq, tq)
        return kw_ref[0, pl.ds(w0, tq), :].astype(BF16), vw_ref[0, pl.ds(w0, tq), :].astype(BF16)

    def win_bias(t, r):
        return uw_ref[r, :, pl.ds(pl.multiple_of(t * tq, tq), tq)]

    nwin = SLIDE_WIN // tq
    owin = flash(jnp.maximum(nwin - qi, 0), nwin + 1, tq, win_kv, win_bias)

    gate = gate_ref[0]
    pieces = []
    for r in range(GROUP_HEADS):
        o = (gate[:, 3 * r:3 * r + 1] * oc[r * tq:(r + 1) * tq]
             + gate[:, 3 * r + 1:3 * r + 2] * osel[r] + gate[:, 3 * r + 2:3 * r + 3] * owin[r])
        pieces.append(jnp.where(par == (r % 2), o, pltpu.roll(o, HEAD_DIM, 1)))
    for hh in range(GROUP_HEADS // 2):
        o_ref[0, :, hh * LANES:(hh + 1) * LANES] = jnp.where(low, pieces[2 * hh], pieces[2 * hh + 1])


def _imp_to_block_matrix(n_cmp_pad, n_cmp, n_blk):
    ratio = SEL_BLOCK // CMP_STRIDE
    span = CMP_LEN // CMP_STRIDE
    a = np.zeros((n_blk, n_cmp_pad), np.float32)
    for m in range(ratio):
        for n in range(span):
            for jb in range(n_blk):
                c = ratio * jb + m - n
                if 0 <= c < n_cmp:
                    a[jb, c] += 1.0
    return a


def _nsa_prompt(q, gate, kc, vc, kvp, kvw, rel_bias, bsz, seq):
    tq, tk = 128, 256
    d = D_MODEL
    nb = seq // SEL_BLOCK
    ncp = seq // CMP_STRIDE
    n_cmp = (seq - CMP_LEN) // CMP_STRIDE + 1
    assert ncp == LANES and seq % tk == 0
    qv = q.reshape(bsz, seq, d)
    gv = gate.reshape(bsz, seq, N_GROUPS * LANES)
    kvpv = kvp.reshape(bsz, seq, kvp.shape[1])
    kvwv = kvw.reshape(bsz, seq, kvw.shape[1])

    pos = np.arange(seq)[:, None]
    n = np.arange(ncp)[None, :]
    dist_c = pos - (n * CMP_STRIDE + CMP_LEN - 1)
    bias_c = _bias_from_dist(rel_bias, dist_c, (dist_c >= 0) & (n < n_cmp))
    i = np.arange(tq)[:, None]
    wu = seq - tq + tk
    dist_u = i - np.arange(wu)[None, :] + (seq - tq)
    strip = _bias_from_dist(rel_bias, dist_u, dist_u >= 0)
    ww = SLIDE_WIN + tq
    dist_w = i + SLIDE_WIN - np.arange(ww)[None, :]
    strip_w = _bias_from_dist(rel_bias, dist_w, (dist_w >= 0) & (dist_w < SLIDE_WIN))

    a_t = jnp.asarray(_imp_to_block_matrix(ncp, n_cmp, nb), BF16)
    e_b = jnp.asarray(np.kron(np.eye(nb), np.ones((1, SEL_BLOCK))), BF16)
    eye = jnp.eye(tq, dtype=BF16)

    half = lambda g: g // 2
    kv_spec = lambda w, off: pl.BlockSpec((1, seq, LANES), lambda g, b, qi: (b, 0, off + half(g)))
    full = lambda a: pl.BlockSpec(a.shape, lambda g, b, qi: (0,) * a.ndim)
    out = pl.pallas_call(
        functools.partial(_nsa_kernel, seq=seq, tq=tq, tk=tk),
        grid=(N_GROUPS, bsz, seq // tq),
        in_specs=[pl.BlockSpec((1, tq, GROUP_HEADS * HEAD_DIM), lambda g, b, qi: (b, qi, g)),
                  pl.BlockSpec((1, tq, LANES), lambda g, b, qi: (b, qi, g)),
                  pl.BlockSpec((1, ncp, LANES), lambda g, b, qi: (b, 0, half(g))),
                  pl.BlockSpec((1, ncp, LANES), lambda g, b, qi: (b, 0, half(g))),
                  kv_spec(kvpv, 4), kv_spec(kvpv, 6), kv_spec(kvwv, 0), kv_spec(kvwv, 2),
                  pl.BlockSpec((GROUP_HEADS, tq, ncp), lambda g, b, qi: (g, qi, 0)),
                  pl.BlockSpec((GROUP_HEADS, tq, wu), lambda g, b, qi: (g, 0, 0)),
                  pl.BlockSpec((GROUP_HEADS, tq, ww), lambda g, b, qi: (g, 0, 0)),
                  full(a_t), full(e_b), full(eye)],
        out_specs=pl.BlockSpec((1, tq, GROUP_HEADS * HEAD_DIM), lambda g, b, qi: (b, qi, g)),
        out_shape=jax.ShapeDtypeStruct((bsz, seq, d), F32),
        scratch_shapes=[pltpu.VMEM((tq, seq), F32),
                        pltpu.VMEM((GROUP_HEADS, tq, 1), F32),
                        pltpu.VMEM((GROUP_HEADS, tq, 1), F32),
                        pltpu.VMEM((GROUP_HEADS, tq, LANES), F32)],
        compiler_params=_cparams("parallel", "parallel", "arbitrary"),
        name="nsa_prompt_attn",
    )(qv, gv, kc, vc, kvpv, kvpv, kvwv, kvwv, bias_c, strip, strip_w, a_t, e_b, eye)
    return out.reshape(bsz * seq, d)


CMP_PAGES = 16


def _scmp_kernel(pt_ref, *refs):
    del pt_ref
    pages = refs[:CMP_PAGES]
    w1_ref, pe_ref, o_ref, t_scr, x_scr = refs[CMP_PAGES:]
    cpp = pages[0].shape[3] // CMP_STRIDE
    nh = N_GROUPS * CMP_HIDDEN
    for kind in range(2):
        for i, p_ref in enumerate(pages):
            x = p_ref[0, kind].T
            for hh in range(x.shape[1] // LANES):
                t_scr[i, hh] = x[:, hh * LANES:(hh + 1) * LANES]
            for s in range(CMP_STRIDE):
                for hh in range(x.shape[1] // LANES):
                    x_scr[s, i * cpp:(i + 1) * cpp, hh * LANES:(hh + 1) * LANES] = (
                        t_scr[i, hh, pl.ds(s, cpp, stride=CMP_STRIDE), :])
        for part in range(CMP_LEN // CMP_STRIDE):
            acc = jnp.zeros((x_scr.shape[1], nh), F32)
            for s in range(CMP_STRIDE):
                x = (x_scr[s] + pe_ref[kind, part, s:s + 1, :]).astype(BF16)
                acc = acc + _dot(x, w1_ref[kind, part, s])
            o_ref[kind, 0, :, part * nh:(part + 1) * nh] = acc


def _scmp(cache, page_table, w1bd, pe):
    bsz, n_pages = page_table.shape
    _, _, gw, page = cache.shape
    cpp = page // CMP_STRIDE
    nh = N_GROUPS * CMP_HIDDEN
    n_chunks = n_pages * cpp
    page_spec = lambda i: pl.BlockSpec((1, 2, gw, page), lambda b, j, pt: (pt[b, j * CMP_PAGES + i], 0, 0, 0))
    const = lambda a: pl.BlockSpec(a.shape, lambda b, j, pt: (0,) * a.ndim, pipeline_mode=pl.Buffered(1))
    grid_spec = pltpu.PrefetchScalarGridSpec(
        num_scalar_prefetch=1,
        grid=(bsz, n_pages // CMP_PAGES),
        in_specs=[page_spec(i) for i in range(CMP_PAGES)] + [const(w1bd), const(pe)],
        out_specs=pl.BlockSpec((2, 1, CMP_PAGES * cpp, 2 * nh), lambda b, j, pt: (0, b, j, 0)),
        scratch_shapes=[pltpu.VMEM((CMP_PAGES, gw // LANES, page, LANES), F32),
                        pltpu.VMEM((CMP_STRIDE, CMP_PAGES * cpp, gw), F32)])
    return pl.pallas_call(
        _scmp_kernel,
        grid_spec=grid_spec,
        out_shape=jax.ShapeDtypeStruct((2, bsz, n_chunks, 2 * nh), F32),
        compiler_params=_cparams("parallel", "arbitrary"),
        name="compress_sample",
    )(page_table, *([cache] * CMP_PAGES), w1bd, pe)


def _s1_kernel(ab_ref, w2_ref, q_ref, bc_ref, at_ref, gs_ref, oc_ref, it_ref):
    nck = ab_ref.shape[2]
    nh = N_GROUPS * CMP_HIDDEN
    kv = []
    for kind in range(2):
        h = ab_ref[kind, 0, :, :nh] + pltpu.roll(ab_ref[kind, 0, :, nh:], nck - 1, 0)
        kv.append(_dot(jnp.maximum(h, 0.0).astype(BF16), w2_ref[kind]).astype(BF16))
    diag = _head_diag(N_HEADS, D_MODEL)
    qe = jnp.where(diag, q_ref[0] * ATTN_SCALE, 0.0).astype(BF16)
    bc = bc_ref[...]
    pc = _masked_softmax(_dot_nt(qe, kv[0]) + bc, bc > 0.5 * NEG)
    o = _dot(pc.astype(BF16), kv[1])
    oc_ref[0] = jnp.sum(jnp.where(diag, o, 0.0), axis=0, keepdims=True)
    hi = pc.astype(BF16)
    lo = (pc - hi.astype(F32)).astype(BF16)
    imp = _dot(gs_ref[...], hi) + _dot(gs_ref[...], lo)
    it_ref[0] = _split_dot_nt(at_ref[...], imp)


def _s1(ab, w2ex, q, rel_bias, past):
    _, bsz, nck, _ = ab.shape
    d = D_MODEL
    n_cmp = (past + 1 - CMP_LEN) // CMP_STRIDE + 1
    n_blk = -(-(past + 1) // SEL_BLOCK)
    nbp = -(-n_blk // 8) * 8
    n = np.arange(nck)
    dist_c = past - (n * CMP_STRIDE + CMP_LEN - 1)
    bias_c = _bias_from_dist(rel_bias, dist_c, (dist_c >= 0) & (n < n_cmp))
    a_t = jnp.asarray(_imp_to_block_matrix(nck, n_cmp, nbp)[:nbp] * (np.arange(nbp)[:, None] < n_blk), BF16)
    gsum = jnp.asarray(np.kron(np.eye(8, N_GROUPS), np.ones((1, GROUP_HEADS))), BF16)
    full = lambda a: pl.BlockSpec(a.shape, lambda b: (0,) * a.ndim)
    oc, imp_t = pl.pallas_call(
        _s1_kernel,
        grid=(bsz,),
        in_specs=[pl.BlockSpec((2, 1, nck, ab.shape[3]), lambda b: (0, b, 0, 0)), full(w2ex),
                  pl.BlockSpec((1, 1, d), lambda b: (b, 0, 0)), full(bias_c), full(a_t), full(gsum)],
        out_specs=[pl.BlockSpec((1, 1, d), lambda b: (b, 0, 0)), pl.BlockSpec((1, nbp, 8), lambda b: (b, 0, 0))],
        out_shape=[jax.ShapeDtypeStruct((bsz, 1, d), F32), jax.ShapeDtypeStruct((bsz, nbp, 8), F32)],
        compiler_params=_cparams("parallel"),
        name="sample_compressed_attn",
    )(ab, w2ex, q.reshape(bsz, 1, d), bias_c, a_t, gsum)
    return oc.reshape(bsz, d), imp_t


def _s2_kernel(imp_ref, idx_ref, s_scr, *, n_blk, cur):
    shape = imp_ref.shape
    j = lax.broadcasted_iota(jnp.int32, shape, 0)
    forced = jnp.logical_or(j == 0, jnp.logical_or(j == cur, j == cur - 1))
    score = jnp.where(forced, SEL_FORCE, jnp.where(j < n_blk, imp_ref[...], -2.0))
    s_scr[...] = score

    def body(i, rank):
        si = s_scr[pl.ds(i, 1), :]
        beats = jnp.logical_or(si > score, jnp.logical_and(si == score, j > i))
        return rank + jnp.where(beats, 1.0, 0.0)

    rank = lax.fori_loop(0, n_blk, body, jnp.zeros(shape, F32))
    for k in range(SEL_TOPK):
        idx_ref[k:k + 1, :] = jnp.sum(jnp.where(rank == k, j, 0), axis=0, keepdims=True)


def _s2(imp_t, past):
    n_blk = -(-(past + 1) // SEL_BLOCK)
    return pl.pallas_call(
        functools.partial(_s2_kernel, n_blk=n_blk, cur=past // SEL_BLOCK),
        out_shape=jax.ShapeDtypeStruct((SEL_TOPK, imp_t.shape[1]), jnp.int32),
        scratch_shapes=[pltpu.VMEM(imp_t.shape, F32)],
        name="sample_block_topk",
    )(imp_t)


def _bucket_thresholds(max_dist):
    b = _bucket_np(np.arange(max_dist + 1))
    return [int(np.argmax(b >= k)) if (b >= k).any() else max_dist + 1 for k in range(REL_BUCKETS)]


def _attend_t(q, kt, vt, bias, valid, kn, vn, bias_new):
    s = _dot(q.astype(BF16), kt.astype(BF16)) + bias
    if valid is not None:
        s = jnp.where(valid, s, NEG)
    sn = jnp.sum(q * kn, axis=-1, keepdims=True) + bias_new
    m = jnp.maximum(jnp.max(s, axis=-1, keepdims=True), sn)
    e = jnp.exp(s - m)
    en = jnp.exp(sn - m)
    l = jnp.sum(e, axis=-1, keepdims=True) + en
    return (_dot_nt(e.astype(BF16), vt.astype(BF16)) + en * vn) / l


def _ssel_kernel(idx_ref, page_ref, *refs, past, thresholds):
    del page_ref
    blocks = refs[:SEL_TOPK]
    q_ref, n_ref, tb_ref, o_ref = refs[SEL_TOPK:]
    b = pl.program_id(0)
    g = pl.program_id(1)
    page = blocks[0].shape[3]
    kt = jnp.concatenate([blk[0, 0] for blk in blocks], axis=1)
    vt = jnp.concatenate([blk[0, 1] for blk in blocks], axis=1)
    nk = SEL_TOPK * page
    lane = lax.broadcasted_iota(jnp.int32, (1, nk), 1)
    row = lane % page
    pos = jnp.zeros((1, nk), jnp.int32)
    inside = jnp.zeros((1, nk), jnp.int32)
    bpp = page // SEL_BLOCK
    for k in range(SEL_TOPK):
        blk_idx = idx_ref[(b * N_GROUPS + g) * SEL_TOPK + k]
        mine = lane // page == k
        pos = jnp.where(mine, (blk_idx // bpp) * page + row, pos)
        inside = jnp.where(mine, jnp.where(row // SEL_BLOCK == blk_idx % bpp, 1, 0), inside)
    valid = jnp.logical_and(inside == 1, pos < past)
    dist = past - pos
    bias = jnp.broadcast_to(tb_ref[0, 0][:, :1], (8, nk))
    for k in range(1, REL_BUCKETS):
        bias = jnp.where(dist >= thresholds[k], tb_ref[0, k][:, :1], bias)
    q = q_ref[0, 0] * ATTN_SCALE
    o_ref[0, 0] = _attend_t(q, kt, vt, bias, valid, n_ref[0, 0, 0], n_ref[0, 1, 0], tb_ref[0, 0][:, :1])


def _ssel(sel_idx, cache_t, page_table, q, kvp_new, rel_bias, past):
    bsz = q.shape[0]
    page = cache_t.shape[3]
    bpp = page // SEL_BLOCK
    cur = past // SEL_BLOCK
    cached = jnp.minimum(sel_idx, cur - 1)
    pages = jnp.take_along_axis(page_table, (cached // bpp).reshape(bsz, -1), axis=1).reshape(-1)
    q4 = jnp.pad(q.reshape(bsz, N_GROUPS, GROUP_HEADS, HEAD_DIM), ((0, 0), (0, 0), (0, 8 - GROUP_HEADS), (0, 0)))
    new = kvp_new.reshape(bsz, 4, N_GROUPS, 1, HEAD_DIM)
    tb = jnp.transpose(rel_bias.astype(F32).reshape(REL_BUCKETS, N_GROUPS, GROUP_HEADS), (1, 0, 2))
    tb = jnp.pad(tb, ((0, 0), (0, 0), (0, 8 - GROUP_HEADS)))
    tb = jnp.broadcast_to(tb[..., None], tb.shape + (LANES,))
    blk_spec = lambda k: pl.BlockSpec(
        (1, 2, HEAD_DIM, page), lambda b, g, idx, pg: (pg[(b * N_GROUPS + g) * SEL_TOPK + k], 1, g, 0))
    grid_spec = pltpu.PrefetchScalarGridSpec(
        num_scalar_prefetch=2,
        grid=(bsz, N_GROUPS),
        in_specs=[blk_spec(k) for k in range(SEL_TOPK)] + [
            pl.BlockSpec((1, 1, 8, HEAD_DIM), lambda b, g, idx, pg: (b, g, 0, 0)),
            pl.BlockSpec((1, 2, 1, 1, HEAD_DIM), lambda b, g, idx, pg: (b, 1, g, 0, 0)),
            pl.BlockSpec((1, REL_BUCKETS, 8, LANES), lambda b, g, idx, pg: (g, 0, 0, 0))],
        out_specs=pl.BlockSpec((1, 1, 8, HEAD_DIM), lambda b, g, idx, pg: (b, g, 0, 0)))
    out = pl.pallas_call(
        functools.partial(_ssel_kernel, past=past, thresholds=tuple(_bucket_thresholds(past))),
        grid_spec=grid_spec,
        out_shape=jax.ShapeDtypeStruct((bsz, N_GROUPS, 8, HEAD_DIM), F32),
        compiler_params=_cparams("parallel", "parallel"),
        name="sample_selected_attn",
    )(sel_idx.reshape(-1), pages, *([cache_t] * SEL_TOPK), q4, new, tb)
    return out[:, :, :GROUP_HEADS].reshape(bsz, N_HEADS, HEAD_DIM)


def _swin_kernel(c_ref, q_ref, n_ref, m_ref, bw_ref, b0_ref, o_ref, u_ref):
    kt = c_ref[0, 0]
    vt = c_ref[0, 1]
    q = q_ref[0] * ATTN_SCALE
    o = _attend_t(q, kt, vt, bw_ref[...], None, n_ref[0, 0:1, :], n_ref[0, 1:2, :], b0_ref[...][:, :1])
    grp = lax.broadcasted_iota(jnp.int32, (N_HEADS, HEAD_DIM), 0) // GROUP_HEADS
    out = jnp.zeros((N_HEADS, HEAD_DIM), F32)
    for g in range(N_GROUPS):
        out = jnp.where(grp == g, o[:, g * HEAD_DIM:(g + 1) * HEAD_DIM], out)
    o_ref[0] = out
    u_ref[0, 0] = _shift_in(kt, m_ref[0, 0])
    u_ref[0, 1] = _shift_in(vt, m_ref[0, 1])


def _swin(cwin_t, q, kvw_new, rel_bias):
    bsz = q.shape[0]
    _, _, gw, wlen = cwin_t.shape
    onehot = jnp.asarray(np.kron(np.eye(N_GROUPS), np.ones((GROUP_HEADS, 1))), F32)
    qbd = jnp.einsum("bhd,hg->bhgd", q.reshape(bsz, N_HEADS, HEAD_DIM), onehot).reshape(bsz, N_HEADS, gw)
    dist_w = wlen - np.arange(wlen)
    bw = _bias_from_dist(rel_bias, dist_w, dist_w < SLIDE_WIN)
    b0 = _bias_from_dist(rel_bias, np.zeros((LANES,), np.int64), np.ones((LANES,), bool))
    full = lambda a: pl.BlockSpec(a.shape, lambda b: (0,) * a.ndim)
    c_spec = pl.BlockSpec((1, 2, gw, wlen), lambda b: (b, 0, 0, 0))
    o_spec = pl.BlockSpec((1, N_HEADS, HEAD_DIM), lambda b: (b, 0, 0))
    return pl.pallas_call(
        _swin_kernel,
        grid=(bsz,),
        in_specs=[c_spec, pl.BlockSpec((1, N_HEADS, gw), lambda b: (b, 0, 0)),
                  pl.BlockSpec((1, 2, gw), lambda b: (b, 0, 0)),
                  pl.BlockSpec((1, 2, gw, 1), lambda b: (b, 0, 0, 0)), full(bw), full(b0)],
        out_specs=[o_spec, c_spec],
        out_shape=[jax.ShapeDtypeStruct((bsz, N_HEADS, HEAD_DIM), F32), jax.ShapeDtypeStruct(cwin_t.shape, F32)],
        compiler_params=_cparams("parallel"),
        name="sample_window_attn",
    )(cwin_t, qbd, kvw_new.reshape(bsz, 2, gw), kvw_new.reshape(bsz, 2, gw, 1), bw, b0)


def _gate_sum_kernel(g_ref, oc_ref, os_ref, ow_ref, o_ref):
    g = g_ref[...]
    o_ref[...] = g[:, :, 0:1] * oc_ref[...] + g[:, :, 1:2] * os_ref[...] + g[:, :, 2:3] * ow_ref[...]


def _gate_sum(gate, oc, os_, ow):
    return pl.pallas_call(
        _gate_sum_kernel,
        out_shape=jax.ShapeDtypeStruct(oc.shape, F32),
        name="sample_gate_sum",
    )(gate, oc, os_, ow)


def _split_a_weights(w_in):
    d = D_MODEL
    w = w_in.astype(BF16)
    qs = [w[:, g * 3 * d:g * 3 * d + d] for g in range(len(DIL_PATTERNS))]
    kvs = [w[:, g * 3 * d + d:(g + 1) * 3 * d] for g in range(len(DIL_PATTERNS))]
    return qs + kvs


def _split_b_weights(w_in):
    d = D_MODEL
    gw = N_GROUPS * HEAD_DIM
    w = w_in.astype(BF16)
    wg = w[:, d + 6 * gw:].reshape(d, N_GROUPS, GROUP_HEADS * 3)
    wg = jnp.pad(wg, ((0, 0), (0, 0), (0, LANES - GROUP_HEADS * 3))).reshape(d, N_GROUPS * LANES)
    return [w[:, :d], w[:, d:d + 4 * gw], w[:, d + 4 * gw:d + 6 * gw], wg]


def _unblock_gate(gate):
    m = gate.shape[0]
    return gate.reshape(m, N_GROUPS, LANES)[:, :, :GROUP_HEADS * 3].reshape(m, N_HEADS * 3)


def kernel(x_prompt, x_sample, cache_a_win0, cache_a_win1, cache_a_win2, cache_b_kv, cache_b_win, page_table,
           rel_bias, norm_mix, norm_mlp, norm_final, a_w_in, a_w_out, b_w_in, b_w_out, b_phi_pe, b_phi_w1,
           b_phi_w2, mlp_w_up, mlp_w_down):
    bsz, seq, d = x_prompt.shape
    sb = x_sample.shape[0]
    past = page_table.shape[1] * cache_b_kv.shape[2]
    tm_p, tm_s = 512, sb
    xp = x_prompt.reshape(bsz * seq, d)
    xs = x_sample.reshape(sb, d)
    acts4 = ("none",) * 4
    acts6 = ("none",) * 6

    wa = _split_a_weights(a_w_in[0])
    wa_out = a_w_out[0].astype(BF16)
    pa = _norm_matmul(xp, norm_mix[0], wa, acts6, tm_p, 4)
    sa = _norm_matmul(xs, norm_mix[0], wa, acts6, tm_s, 4)
    os_, ls = [], []
    for gi, (_, dil) in enumerate(DIL_PATTERNS):
        o, lse = _swa_group(pa[gi], pa[3 + gi], rel_bias, bsz, seq, dil)
        os_.append(o)
        ls.append(lse)
    xp = _merge_proj(os_, ls, xp, wa_out, tm_p)
    a_caches = [_rows_last(c[0]) for c in (cache_a_win0, cache_a_win1, cache_a_win2)]
    oa, a_new_t = _samp_a(sa[:3], sa[3:], a_caches, rel_bias)
    xs = _matmul_res(oa, xs, wa_out, tm_s)
    a_new_p = [pa[3 + gi].reshape(bsz, seq, 2 * d)[:, seq - min(w, seq):] for gi, (w, _) in enumerate(DIL_PATTERNS)]
    a_new_s = [_rows_second(c, (2, N_HEADS, HEAD_DIM)) for c in a_new_t]

    wu0, wd0 = mlp_w_up[0].astype(BF16), mlp_w_down[0].astype(BF16)
    xp = _mlp(xp, norm_mlp[0], wu0, wd0, norm_final, False, tm_p, 1024)
    xs = _mlp(xs, norm_mlp[0], wu0, wd0, norm_final, False, tm_s, 1024)

    wb = _split_b_weights(b_w_in[0])
    wb_out = b_w_out[0].astype(BF16)
    acts_b = ("none", "none", "none", "sigmoid")
    qp, kvp_p, kvw_p, gate_p = _norm_matmul(xp, norm_mix[1], wb, acts_b, tm_p, 1)
    qs_, kvp_s, kvw_s, gate_s = _norm_matmul(xs, norm_mix[1], wb, acts_b, tm_s, 1)
    w1bd, w2bd, w2ex, pe = _cmp_weights(b_phi_pe[0], b_phi_w1[0], b_phi_w2[0])
    kc = _cmp_prompt(kvp_p, bsz, seq, 0, w1bd, pe, w2bd)
    vc = _cmp_prompt(kvp_p, bsz, seq, 1, w1bd, pe, w2bd)
    ob = _nsa_prompt(qp, gate_p, kc, vc, kvp_p, kvw_p, rel_bias, bsz, seq)
    xp = _matmul_res(ob, xp, wb_out, tm_p)

    cache_t = _rows_last(cache_b_kv[0])
    cwin_t = _rows_last(cache_b_win[0])
    ab = _scmp(cache_t, page_table, w1bd, pe)
    oc, imp_t = _s1(ab, w2ex, qs_, rel_bias, past)
    imp_t = jnp.transpose(imp_t[:, :, :N_GROUPS], (1, 0, 2)).reshape(imp_t.shape[1], sb * N_GROUPS)
    sel_idx = jnp.transpose(_s2(imp_t, past)).reshape(sb, N_GROUPS, SEL_TOPK)
    osel = _ssel(sel_idx, cache_t, page_table, qs_, kvp_s, rel_bias, past)
    owin, b_win_t = _swin(cwin_t, qs_, kvw_s, rel_bias)
    obs = _gate_sum(_unblock_gate(gate_s).reshape(sb, N_HEADS, 3), oc.reshape(sb, N_HEADS, HEAD_DIM), osel, owin)
    xs = _matmul_res(obs.reshape(sb, d), xs, wb_out, tm_s)
    b_win_s = _rows_second(b_win_t, (2, N_GROUPS, HEAD_DIM))

    wu1, wd1 = mlp_w_up[1].astype(BF16), mlp_w_down[1].astype(BF16)
    yp = _mlp(xp, norm_mlp[1], wu1, wd1, norm_final, True, tm_p, 1024)
    ys = _mlp(xs, norm_mlp[1], wu1, wd1, norm_final, True, tm_s, 1024)

    gw = N_GROUPS * HEAD_DIM
    keep = min(SLIDE_WIN, seq)
    cache_shape = lambda n: (1, -1, n, 2, N_HEADS, HEAD_DIM)
    outs = [yp.reshape(bsz, seq, d), ys.reshape(sb, 1, d)]
    for p, s in zip(a_new_p, a_new_s):
        outs.append(p.reshape(cache_shape(p.shape[1])))
        outs.append(s[None])
    outs.append(kvp_p.reshape(1, bsz, seq, 4, N_GROUPS, HEAD_DIM))
    outs.append(kvp_s.reshape(1, sb, 1, 4, N_GROUPS, HEAD_DIM))
    outs.append(kvw_p.reshape(bsz, seq, 2 * gw)[:, seq - keep:].reshape(1, bsz, keep, 2, N_GROUPS, HEAD_DIM))
    outs.append(b_win_s[None])
    return tuple(outs)
```

```python
import functools
import math

import numpy as np
import jax
import jax.numpy as jnp
from jax import lax
from jax.experimental import pallas as pl
from jax.experimental.pallas import tpu as pltpu

F32 = jnp.float32
BF16 = jnp.bfloat16

D_MODEL = 1024
HEAD_DIM = 64
N_HEADS = 16
ATTN_SCALE = HEAD_DIM ** -0.5
RMS_EPS = 1e-6
REL_BUCKETS = 32
REL_MAX_DIST = 2048
DIL_PATTERNS = ((128, 1), (512, 4), (2048, 16))
A_KEYS = 128
N_GROUPS = 4
GROUP_HEADS = N_HEADS // N_GROUPS
CMP_LEN = 32
CMP_STRIDE = 16
CMP_HIDDEN = 128
SEL_BLOCK = 64
SEL_TOPK = 16
SEL_FORCE = 1e4
SLIDE_WIN = 512

LANES = 128
VMEM_LIMIT = 56 * 1024 * 1024
NEG = -1e30

NT_DIMS = (((1,), (1,)), ((), ()))


def _cparams(*sem):
    return pltpu.CompilerParams(dimension_semantics=sem, vmem_limit_bytes=VMEM_LIMIT)


def _dot(a, b):
    return jnp.dot(a, b, preferred_element_type=F32)


def _dot_nt(a, b):
    return lax.dot_general(a, b, NT_DIMS, preferred_element_type=F32)


def _split_dot(x, w):
    hi = x.astype(BF16)
    lo = (x - hi.astype(F32)).astype(BF16)
    return _dot(hi, w) + _dot(lo, w)


def _split_dot_nt(w, x):
    hi = x.astype(BF16)
    lo = (x - hi.astype(F32)).astype(BF16)
    return _dot_nt(w, hi) + _dot_nt(w, lo)


def _rms(x, g):
    ms = jnp.mean(x * x, axis=-1, keepdims=True)
    return x * lax.rsqrt(ms + RMS_EPS) * g


def _bucket_np(dist):
    dist = np.maximum(np.asarray(dist, np.int64), 0)
    max_exact = REL_BUCKETS // 2
    ratio = (np.log(np.maximum(dist, 1).astype(np.float32) / np.float32(max_exact))
             / np.float32(math.log(REL_MAX_DIST / max_exact)))
    large = np.minimum(max_exact + (ratio * (REL_BUCKETS - max_exact)).astype(np.int32), REL_BUCKETS - 1)
    return np.where(dist < max_exact, dist, large).astype(np.int32)


def _bias_from_dist(rel_bias, dist, valid):
    idx = jnp.asarray(_bucket_np(dist))
    b = jnp.take(rel_bias.astype(F32).T, idx, axis=1)
    return jnp.where(jnp.asarray(valid)[None], b, NEG)


def _toeplitz(w, n, m):
    length = n + m
    lead = w.shape[:-1]
    t = jnp.tile(w, (1,) * len(lead) + (n,))[..., :n * (length - 1)].reshape(lead + (n, length - 1))
    return t[..., n - 1:n - 1 + m]


def _toeplitz_bias(rel_bias, n, m, dist_of, valid_of):
    x = np.arange(n + m) - (n - 1)
    return _toeplitz(_bias_from_dist(rel_bias, dist_of(x), valid_of(x)), n, m)


def _norm_matmul_kernel(x_ref, g_ref, *refs, n_w, acts):
    w_refs, o_refs, xn_ref = refs[:n_w], refs[n_w:2 * n_w], refs[2 * n_w]

    @pl.when(pl.program_id(1) == 0)
    def _():
        xn_ref[...] = _rms(x_ref[...], g_ref[...]).astype(BF16)

    xn = xn_ref[...]
    for w_ref, o_ref, act in zip(w_refs, o_refs, acts):
        y = _dot(xn, w_ref[...])
        if act == "sigmoid":
            y = jax.nn.sigmoid(y)
        o_ref[...] = y


def _norm_matmul(x, g, ws, acts, tm, nj):
    m, d = x.shape
    kern = functools.partial(_norm_matmul_kernel, n_w=len(ws), acts=tuple(acts))
    in_specs = [pl.BlockSpec((tm, d), lambda i, j: (i, 0)), pl.BlockSpec((1, d), lambda i, j: (0, 0))]
    in_specs += [pl.BlockSpec((d, w.shape[1] // nj), lambda i, j: (0, j)) for w in ws]
    out_specs = [pl.BlockSpec((tm, w.shape[1] // nj), lambda i, j: (i, j)) for w in ws]
    return pl.pallas_call(
        kern,
        grid=(m // tm, nj),
        in_specs=in_specs,
        out_specs=out_specs,
        out_shape=[jax.ShapeDtypeStruct((m, w.shape[1]), F32) for w in ws],
        scratch_shapes=[pltpu.VMEM((tm, d), BF16)],
        compiler_params=_cparams("parallel", "arbitrary"),
        name="norm_matmul",
    )(x, g.reshape(1, d), *ws)


def _matmul_res_kernel(a_ref, x_ref, w_ref, o_ref):
    o_ref[...] = x_ref[...] + _dot(a_ref[...].astype(BF16), w_ref[...])


def _matmul_res(a, x, w, tm):
    m, d = x.shape
    return pl.pallas_call(
        _matmul_res_kernel,
        grid=(m // tm,),
        in_specs=[pl.BlockSpec((tm, a.shape[1]), lambda i: (i, 0)),
                  pl.BlockSpec((tm, d), lambda i: (i, 0)),
                  pl.BlockSpec(w.shape, lambda i: (0, 0))],
        out_specs=pl.BlockSpec((tm, d), lambda i: (i, 0)),
        out_shape=jax.ShapeDtypeStruct((m, d), F32),
        compiler_params=_cparams("parallel"),
        name="matmul_res",
    )(a, x, w)


def _merge_proj_kernel(o0, o1, o2, l0, l1, l2, e_ref, x_ref, w_ref, out_ref):
    ls = [l0[...], l1[...], l2[...]]
    mx = jnp.maximum(jnp.maximum(ls[0], ls[1]), ls[2])
    es = [jnp.exp(l - mx) for l in ls]
    tot = es[0] + es[1] + es[2]
    merged = None
    for e, o in zip(es, (o0, o1, o2)):
        part = _split_dot(e / tot, e_ref[...]) * o[...]
        merged = part if merged is None else merged + part
    out_ref[...] = x_ref[...] + _dot(merged.astype(BF16), w_ref[...])


def _merge_proj(os_, ls, x, w, tm):
    m, d = x.shape
    expand = jnp.asarray(np.kron(np.eye(N_HEADS), np.ones((1, HEAD_DIM))), BF16)
    row = lambda n: pl.BlockSpec((tm, n), lambda i: (i, 0))
    return pl.pallas_call(
        _merge_proj_kernel,
        grid=(m // tm,),
        in_specs=[row(d)] * 3 + [row(N_HEADS)] * 3 + [
            pl.BlockSpec(expand.shape, lambda i: (0, 0)), row(d), pl.BlockSpec(w.shape, lambda i: (0, 0))],
        out_specs=row(d),
        out_shape=jax.ShapeDtypeStruct((m, d), F32),
        compiler_params=_cparams("parallel"),
        name="merge_proj",
    )(*os_, *ls, expand, x, w)


def _mlp_kernel(x_ref, g_ref, wu_ref, wd_ref, gf_ref, o_ref, xn_ref, acc_ref, *, final_norm):
    f = pl.program_id(1)

    @pl.when(f == 0)
    def _():
        xn_ref[...] = _rms(x_ref[...], g_ref[...]).astype(BF16)
        acc_ref[...] = jnp.zeros_like(acc_ref)

    h = jnp.square(jnp.maximum(_dot(xn_ref[...], wu_ref[...]), 0.0))
    acc_ref[...] += _dot(h.astype(BF16), wd_ref[...])

    @pl.when(f == pl.num_programs(1) - 1)
    def _():
        y = x_ref[...] + acc_ref[...]
        if final_norm:
            y = _rms(y, gf_ref[...])
        o_ref[...] = y


def _mlp(x, g, wu, wd, gf, final_norm, tm, tf):
    m, d = x.shape
    dff = wu.shape[1]
    return pl.pallas_call(
        functools.partial(_mlp_kernel, final_norm=final_norm),
        grid=(m // tm, dff // tf),
        in_specs=[pl.BlockSpec((tm, d), lambda i, f: (i, 0)),
                  pl.BlockSpec((1, d), lambda i, f: (0, 0)),
                  pl.BlockSpec((d, tf), lambda i, f: (0, f)),
                  pl.BlockSpec((tf, d), lambda i, f: (f, 0)),
                  pl.BlockSpec((1, d), lambda i, f: (0, 0))],
        out_specs=pl.BlockSpec((tm, d), lambda i, f: (i, 0)),
        out_shape=jax.ShapeDtypeStruct((m, d), F32),
        scratch_shapes=[pltpu.VMEM((tm, d), BF16), pltpu.VMEM((tm, d), F32)],
        compiler_params=_cparams("parallel", "arbitrary"),
        name="mlp",
    )(x, g.reshape(1, d), wu, wd, gf.reshape(1, d))


def _swa_kernel(q_ref, kc_ref, kp_ref, vc_ref, vp_ref, b_ref, o_ref, lse_ref):
    t = q_ref.shape[1]
    lane = lax.broadcasted_iota(jnp.int32, (t, LANES), 1)
    low = lane < HEAD_DIM
    hcol = lax.broadcasted_iota(jnp.int32, (t, N_HEADS), 1)
    lse_all = jnp.zeros((t, N_HEADS), F32)
    for hp in range(N_HEADS // 2):
        sl = slice(hp * LANES, (hp + 1) * LANES)
        q2 = (q_ref[0, :, sl] * ATTN_SCALE).astype(BF16)
        k2 = jnp.concatenate([kp_ref[0, :, sl], kc_ref[0, :, sl]], axis=0).astype(BF16)
        v2 = jnp.concatenate([vp_ref[0, :, sl], vc_ref[0, :, sl]], axis=0).astype(BF16)
        outs = []
        for a in range(2):
            h = 2 * hp + a
            qm = jnp.where(low if a == 0 else jnp.logical_not(low), q2, jnp.zeros_like(q2))
            s = _dot_nt(qm, k2) + b_ref[0, h]
            m = jnp.max(s, axis=-1, keepdims=True)
            e = jnp.exp(s - m)
            l = jnp.sum(e, axis=-1, keepdims=True)
            outs.append(_dot(e.astype(BF16), v2) / l)
            lse_all = jnp.where(hcol == h, m + jnp.log(l), lse_all)
        o_ref[0, :, sl] = jnp.where(low, outs[0], outs[1])
    lse_ref[0, 0] = lse_all


def _swa_bias(rel_bias, dil, t):
    later = _toeplitz_bias(rel_bias, t, 2 * t, lambda x: dil * (t - x), lambda x: (t - x >= 0) & (t - x <= t))
    first = jnp.where(jnp.asarray(np.arange(2 * t) >= t)[None, None, :], later, NEG)
    return jnp.stack([first, later])


def _swa_group(q, kv, rel_bias, bsz, seq, dil):
    t = A_KEYS
    ln = seq // dil
    d = D_MODEL
    qv = q.reshape(bsz, ln, dil * d)
    kvv = kv.reshape(bsz, ln, dil * 2 * d)
    bias = _swa_bias(rel_bias, dil, t)
    prev = lambda qi: jnp.maximum(qi - 1, 0)
    o, lse = pl.pallas_call(
        _swa_kernel,
        grid=(bsz, dil, ln // t),
        in_specs=[pl.BlockSpec((1, t, d), lambda b, r, qi: (b, qi, r)),
                  pl.BlockSpec((1, t, d), lambda b, r, qi: (b, qi, 2 * r)),
                  pl.BlockSpec((1, t, d), lambda b, r, qi: (b, prev(qi), 2 * r)),
                  pl.BlockSpec((1, t, d), lambda b, r, qi: (b, qi, 2 * r + 1)),
                  pl.BlockSpec((1, t, d), lambda b, r, qi: (b, prev(qi), 2 * r + 1)),
                  pl.BlockSpec((1, N_HEADS, t, 2 * t), lambda b, r, qi: (jnp.minimum(qi, 1), 0, 0, 0))],
        out_specs=[pl.BlockSpec((1, t, d), lambda b, r, qi: (b, qi, r)),
                   pl.BlockSpec((1, 1, t, N_HEADS), lambda b, r, qi: (b, r, qi, 0))],
        out_shape=[jax.ShapeDtypeStruct((bsz, ln, dil * d), F32),
                   jax.ShapeDtypeStruct((bsz, dil, ln, N_HEADS), F32)],
        compiler_params=_cparams("parallel", "parallel", "arbitrary"),
        name="swa_attn",
    )(qv, kvv, kvv, kvv, kvv, bias)
    lse = jnp.transpose(lse, (0, 2, 1, 3)).reshape(bsz * seq, N_HEADS)
    return o.reshape(bsz * seq, d), lse


def _head_diag(rows, width):
    row = lax.broadcasted_iota(jnp.int32, (rows, width), 0)
    lane = lax.broadcasted_iota(jnp.int32, (rows, width), 1)
    return (lane // HEAD_DIM) == row


def _rows_last(cache):
    nd = cache.ndim
    t = jnp.transpose(cache, (0,) + tuple(range(2, nd)) + (1,))
    bsz, window = cache.shape[0], cache.shape[1]
    lead = int(np.prod(cache.shape[2:nd - 2]))
    return t.reshape(bsz, lead, cache.shape[nd - 2] * cache.shape[nd - 1], window)


def _rows_second(cache_t, feat):
    bsz, _, _, window = cache_t.shape
    nd = len(feat) + 2
    t = cache_t.reshape((bsz,) + tuple(feat) + (window,))
    return jnp.transpose(t, (0, nd - 1) + tuple(range(1, nd - 1)))


def _shift_in(x, col):
    w = x.shape[1]
    lane = lax.broadcasted_iota(jnp.int32, x.shape, 1)
    return jnp.where(lane == w - 1, col, pltpu.roll(x, w - 1, 1))


A_HEAD_CHUNK = 4


def _samp_a_kernel(q0, q1, q2, n0, n1, n2, m0, m1, m2, c0, c1, c2, b0, b1, b2, bs_ref, o_ref, u0, u1, u2):
    hw = A_HEAD_CHUNK * HEAD_DIM
    diag = _head_diag(8, hw)
    outs, lses = [], []
    for q_ref, n_ref, m_ref, c_ref, b_ref, u_ref in (
            (q0, n0, m0, c0, b0, u0), (q1, n1, m1, c1, b1, u1), (q2, n2, m2, c2, b2, u2)):
        qe = jnp.where(diag, q_ref[0] * ATTN_SCALE, 0.0)
        kt = c_ref[0, 0]
        vt = c_ref[0, 1]
        kn = n_ref[0, 0:1, :]
        vn = n_ref[0, 1:2, :]
        s = _dot(qe.astype(BF16), kt.astype(BF16)) + b_ref[0]
        sn = jnp.sum(qe * kn, axis=-1, keepdims=True) + bs_ref[0][:, :1]
        m = jnp.maximum(jnp.max(s, axis=-1, keepdims=True), sn)
        e = jnp.exp(s - m)
        en = jnp.exp(sn - m)
        l = jnp.sum(e, axis=-1, keepdims=True) + en
        outs.append((_dot_nt(e.astype(BF16), vt.astype(BF16)) + en * vn) / l)
        lses.append(m + jnp.log(l))
        u_ref[0, 0] = _shift_in(kt, m_ref[0, 0])
        u_ref[0, 1] = _shift_in(vt, m_ref[0, 1])
    mx = jnp.maximum(jnp.maximum(lses[0], lses[1]), lses[2])
    es = [jnp.exp(l - mx) for l in lses]
    tot = es[0] + es[1] + es[2]
    merged = (es[0] / tot) * outs[0] + (es[1] / tot) * outs[1] + (es[2] / tot) * outs[2]
    o_ref[0] = jnp.sum(jnp.where(diag, merged, 0.0), axis=0, keepdims=True)


def _samp_a(qs, kvs, caches_t, rel_bias):
    bsz, d = qs[0].shape
    hw = A_HEAD_CHUNK * HEAD_DIM
    nhc = N_HEADS // A_HEAD_CHUNK
    pad_rows = lambda a: jnp.pad(a.reshape(nhc, A_HEAD_CHUNK, -1), ((0, 0), (0, 8 - A_HEAD_CHUNK), (0, 0)))
    biases = []
    for window, dil in DIL_PATTERNS:
        dist = window - np.arange(window)
        biases.append(pad_rows(_bias_from_dist(rel_bias, dist, dist % dil == 0)))
    bself = pad_rows(_bias_from_dist(rel_bias, np.zeros((LANES,), np.int64), np.ones((LANES,), bool)))
    q_spec = pl.BlockSpec((1, 1, hw), lambda b, c: (b, 0, c))
    n_spec = pl.BlockSpec((1, 2, hw), lambda b, c: (b, 0, c))
    m_spec = pl.BlockSpec((1, 2, hw, 1), lambda b, c: (b, 0, c, 0))
    c_spec = lambda w: pl.BlockSpec((1, 2, hw, w), lambda b, c: (b, 0, c, 0))
    b_spec = lambda w: pl.BlockSpec((1, 8, w), lambda b, c: (c, 0, 0))
    windows = [w for w, _ in DIL_PATTERNS]
    res = pl.pallas_call(
        _samp_a_kernel,
        grid=(bsz, nhc),
        in_specs=[q_spec] * 3 + [n_spec] * 3 + [m_spec] * 3 + [c_spec(w) for w in windows]
        + [b_spec(w) for w in windows] + [b_spec(LANES)],
        out_specs=[q_spec] + [c_spec(w) for w in windows],
        out_shape=[jax.ShapeDtypeStruct((bsz, 1, d), F32)] + [jax.ShapeDtypeStruct(c.shape, F32) for c in caches_t],
        compiler_params=_cparams("parallel", "parallel"),
        name="sample_dilated_attn",
    )(*[q.reshape(bsz, 1, d) for q in qs], *[kv.reshape(bsz, 2, d) for kv in kvs],
      *[kv.reshape(bsz, 2, d, 1) for kv in kvs], *caches_t, *biases, bself)
    return res[0].reshape(bsz, d), res[1:]


def _cmp_weights(phi_pe, phi_w1, phi_w2):
    r = CMP_LEN // CMP_STRIDE
    eye = jnp.eye(N_GROUPS, dtype=F32)
    w1r = phi_w1.reshape(2, r, CMP_STRIDE, HEAD_DIM, CMP_HIDDEN)
    w1bd = jnp.einsum("kpsdh,gG->kpsgdGh", w1r, eye).reshape(
        2, r, CMP_STRIDE, N_GROUPS * HEAD_DIM, N_GROUPS * CMP_HIDDEN).astype(BF16)
    w2bd = jnp.einsum("khd,gG->kghGd", phi_w2, eye).reshape(
        2, N_GROUPS * CMP_HIDDEN, N_GROUPS * HEAD_DIM).astype(BF16)
    w2ex = jnp.einsum("khd,gG,r->kghGrd", phi_w2, eye, jnp.ones((GROUP_HEADS,), F32)).reshape(
        2, N_GROUPS * CMP_HIDDEN, D_MODEL).astype(BF16)
    pe = jnp.tile(phi_pe.reshape(2, r, CMP_STRIDE, 1, HEAD_DIM), (1, 1, 1, N_GROUPS, 1)).reshape(
        2, r, CMP_STRIDE, N_GROUPS * HEAD_DIM)
    return w1bd, w2bd, w2ex, pe


def _cmp_prompt_kernel(x_ref, w1_ref, pe_ref, w2_ref, o_ref, *, kind, transposed):
    nch = x_ref.shape[1]
    gw = N_GROUPS * HEAD_DIM
    hid = []
    for part in range(CMP_LEN // CMP_STRIDE):
        acc = jnp.zeros((nch, N_GROUPS * CMP_HIDDEN), F32)
        for s in range(CMP_STRIDE):
            c0 = s * 4 * gw + kind * gw
            x = (x_ref[0, :, c0:c0 + gw] + pe_ref[part, s:s + 1, :]).astype(BF16)
            acc = acc + _dot(x, w1_ref[part, s])
        hid.append(acc)
    h = hid[0] + pltpu.roll(hid[1], nch - 1, 0)
    h = jnp.maximum(h, 0.0).astype(BF16)
    o_ref[0] = _dot_nt(w2_ref[...], h) if transposed else _dot(h, w2_ref[...])


def _cmp_prompt(kvp, bsz, seq, kind, w1bd, pe, w2bd, transposed):
    nch = seq // CMP_STRIDE
    xv = kvp.reshape(bsz, nch, CMP_STRIDE * kvp.shape[1])
    full = lambda a: pl.BlockSpec(a.shape, lambda b: (0,) * a.ndim)
    w1, p, w2 = w1bd[kind], pe[kind], w2bd[kind]
    out_dims = (w2.shape[1], nch) if transposed else (nch, w2.shape[1])
    if transposed:
        w2 = w2.T
    return pl.pallas_call(
        functools.partial(_cmp_prompt_kernel, kind=kind, transposed=transposed),
        grid=(bsz,),
        in_specs=[pl.BlockSpec((1, nch, xv.shape[2]), lambda b: (b, 0, 0)), full(w1), full(p), full(w2)],
        out_specs=pl.BlockSpec((1,) + out_dims, lambda b: (b, 0, 0)),
        out_shape=jax.ShapeDtypeStruct((bsz,) + out_dims, F32),
        compiler_params=_cparams("parallel"),
        name="compress_prompt",
    )(xv, w1, p, w2)


def _masked_softmax(s, valid, axis=-1):
    m = jnp.max(s, axis=axis, keepdims=True)
    m = jnp.where(m > 0.5 * NEG, m, 0.0)
    e = jnp.where(valid, jnp.exp(s - m), 0.0)
    return e / jnp.maximum(jnp.sum(e, axis=axis, keepdims=True), 1e-30)


def _topk_mask_t(score, j, n_iter):
    rank = jnp.zeros(score.shape, F32)
    for i in range(n_iter):
        si = score[i:i + 1, :]
        beats = jnp.logical_or(si > score, jnp.logical_and(si == score, j > i))
        rank = rank + jnp.where(beats, 1.0, 0.0)
    return jnp.where(rank < SEL_TOPK, 1.0, 0.0)


def _nsa_kernel(q_ref, gate_ref, kc_ref, vct_ref, ks_ref, vs_ref, kw_ref, vw_ref, bct_ref, ut_ref, uwt_ref,
                a_ref, ebt_ref, o_ref, ksb, vst, kwb, vwt, selb, m_ref, l_ref, acc_ref, *, seq, tq, tc):
    g = pl.program_id(0)
    qi = pl.program_id(2)
    q0 = qi * tq
    par = g % 2
    nb = seq // SEL_BLOCK
    lane = lax.broadcasted_iota(jnp.int32, (tq, LANES), 1)
    keep = (lane // HEAD_DIM) == par

    @pl.when(qi == 0)
    def _():
        ksb[...] = ks_ref[0].astype(BF16)
        kwb[...] = kw_ref[0].astype(BF16)
        for c in range(seq // (2 * LANES)):
            sl = slice(c * 2 * LANES, (c + 1) * 2 * LANES)
            vst[:, sl] = vs_ref[0, sl, :].T.astype(BF16)
            vwt[:, sl] = vw_ref[0, sl, :].T.astype(BF16)

    qs = []
    for r in range(GROUP_HEADS):
        qh = q_ref[0, :, (r // 2) * LANES:(r // 2 + 1) * LANES] * ATTN_SCALE
        qh = jnp.where(par == (r % 2), qh, pltpu.roll(qh, HEAD_DIM, 1))
        qs.append(jnp.where(keep, qh, 0.0).astype(BF16))
    q4 = jnp.concatenate(qs, axis=0)

    bct = bct_ref[0, 0]
    pc = _masked_softmax(_dot_nt(kc_ref[0].astype(BF16), q4) + bct, bct > 0.5 * NEG, axis=0)
    oc = _dot(vct_ref[0].astype(BF16), pc.astype(BF16))
    imp = pc[:, 0:tq] + pc[:, tq:2 * tq] + pc[:, 2 * tq:3 * tq] + pc[:, 3 * tq:4 * tq]

    hi = imp.astype(BF16)
    lo = (imp - hi.astype(F32)).astype(BF16)
    imp_t = _dot(a_ref[...], hi) + _dot(a_ref[...], lo)
    j = lax.broadcasted_iota(jnp.int32, (nb, tq), 0)
    qpos = q0 + lax.broadcasted_iota(jnp.int32, (nb, tq), 1)
    cur = qpos // SEL_BLOCK
    forced = jnp.logical_or(j == 0, jnp.logical_or(j == cur, j == cur - 1))
    score = jnp.where(forced, SEL_FORCE, jnp.where(j * SEL_BLOCK <= qpos, imp_t, -1.0))
    sel_t = _topk_mask_t(score, j, nb).astype(BF16)
    sb = (_dot(ebt_ref[...], sel_t) - 1.0) * (-NEG)
    selb[...] = sb

    m_ref[...] = jnp.full(m_ref.shape, NEG, F32)
    l_ref[...] = jnp.zeros(l_ref.shape, F32)
    acc_ref[...] = jnp.zeros(acc_ref.shape, F32)

    def sel_chunk(t, carry):
        k0 = pl.multiple_of(t * tc, tc)
        u0 = pl.multiple_of(k0 - q0 + (seq - tq), tq)
        sbk = selb[pl.ds(k0, tc), :]
        s = (_dot_nt(ksb[pl.ds(k0, tc), :], q4) + ut_ref[0, pl.ds(u0, tc), :]
             + jnp.concatenate([sbk] * GROUP_HEADS, axis=1))
        m_prev = m_ref[...]
        m_new = jnp.maximum(m_prev, jnp.max(s, axis=0, keepdims=True))
        alpha = jnp.exp(m_prev - m_new)
        e = jnp.exp(s - m_new)
        l_ref[...] = alpha * l_ref[...] + jnp.sum(e, axis=0, keepdims=True)
        acc_ref[...] = alpha * acc_ref[...] + _dot(vst[:, pl.ds(k0, tc)], e.astype(BF16))
        m_ref[...] = m_new
        return carry

    lax.fori_loop(0, (q0 + tq + tc - 1) // tc, sel_chunk, 0)
    osel = acc_ref[...] / l_ref[...]

    ww = SLIDE_WIN + tq
    w0 = pl.multiple_of(jnp.maximum(q0 - SLIDE_WIN, 0), tq)
    uw0 = pl.multiple_of(w0 - (q0 - SLIDE_WIN), tq)
    s = _dot_nt(kwb[pl.ds(w0, ww), :], q4) + uwt_ref[0, pl.ds(uw0, ww), :]
    e = jnp.exp(s - jnp.max(s, axis=0, keepdims=True))
    owin = _dot(vwt[:, pl.ds(w0, ww)], e.astype(BF16)) / jnp.sum(e, axis=0, keepdims=True)

    gt = gate_ref[0].T
    rows = []
    for r in range(GROUP_HEADS):
        cs = slice(r * tq, (r + 1) * tq)
        o = (gt[3 * r:3 * r + 1, :] * oc[:, cs] + gt[3 * r + 1:3 * r + 2, :] * osel[:, cs]
             + gt[3 * r + 2:3 * r + 3, :] * owin[:, cs])
        rows.append(jnp.where(par == 0, o[:HEAD_DIM], o[HEAD_DIM:]))
    o_ref[0] = jnp.concatenate(rows, axis=0).T


def _imp_to_block_matrix(n_cmp_pad, n_cmp, n_blk):
    ratio = SEL_BLOCK // CMP_STRIDE
    span = CMP_LEN // CMP_STRIDE
    a = np.zeros((n_blk, n_cmp_pad), np.float32)
    for m in range(ratio):
        for n in range(span):
            for jb in range(n_blk):
                c = ratio * jb + m - n
                if 0 <= c < n_cmp:
                    a[jb, c] += 1.0
    return a


def _heads_to_lanes(t, lead):
    nl = len(lead)
    rows, tq = t.shape[-2:]
    t = t.reshape((N_GROUPS, GROUP_HEADS) + tuple(lead) + (rows, tq))
    t = jnp.transpose(t, (0,) + tuple(range(2, 2 + nl)) + (2 + nl, 1, 3 + nl))
    return t.reshape((N_GROUPS,) + tuple(lead) + (rows, GROUP_HEADS * tq))


def _nsa_prompt(q, gate, kc, vct, kvp, kvw, rel_bias, bsz, seq):
    tq, tc = LANES, 4 * LANES
    d = D_MODEL
    nb = seq // SEL_BLOCK
    ncp = seq // CMP_STRIDE
    nqt = seq // tq
    n_cmp = (seq - CMP_LEN) // CMP_STRIDE + 1
    nq = GROUP_HEADS * tq
    assert ncp == LANES and SLIDE_WIN % tq == 0
    qv = q.reshape(bsz, seq, d)
    gv = gate.reshape(bsz, seq, N_GROUPS * LANES)
    kvpv = kvp.reshape(bsz, seq, kvp.shape[1])
    kvwv = kvw.reshape(bsz, seq, kvw.shape[1])

    sub = np.arange(CMP_STRIDE)[:, None]
    x = (np.arange(2 * ncp) - (ncp - 1))[None, :]
    dist_c = -CMP_STRIDE * x + sub - (CMP_LEN - 1)
    bias_c = _toeplitz(_bias_from_dist(rel_bias, dist_c, dist_c >= 0), ncp, ncp)
    bias_c = jnp.where(jnp.asarray(np.arange(ncp) < n_cmp), bias_c, NEG)
    bias_c = jnp.transpose(bias_c, (0, 3, 2, 1)).reshape(N_HEADS, ncp, nqt, tq)
    bias_ct = _heads_to_lanes(jnp.transpose(bias_c, (0, 2, 1, 3)), (nqt,))
    wu = seq - tq + tc
    strip = _toeplitz_bias(rel_bias, tq, wu, lambda x: (seq - tq) - x, lambda x: (seq - tq) - x >= 0)
    strip_t = _heads_to_lanes(jnp.swapaxes(strip, 1, 2), ())
    ww = 2 * SLIDE_WIN + tq
    strip_w = _toeplitz_bias(rel_bias, tq, ww, lambda x: SLIDE_WIN - x,
                             lambda x: (SLIDE_WIN - x >= 0) & (SLIDE_WIN - x < SLIDE_WIN))
    strip_wt = _heads_to_lanes(jnp.swapaxes(strip_w, 1, 2), ())

    a_m = jnp.asarray(_imp_to_block_matrix(ncp, n_cmp, nb), BF16)
    e_bt = jnp.asarray(np.kron(np.eye(nb), np.ones((SEL_BLOCK, 1))), BF16)

    half = lambda g: g // 2
    kv_spec = lambda off: pl.BlockSpec((1, seq, LANES), lambda g, b, qi: (b, 0, off + half(g)))
    full = lambda a: pl.BlockSpec(a.shape, lambda g, b, qi: (0,) * a.ndim)
    out = pl.pallas_call(
        functools.partial(_nsa_kernel, seq=seq, tq=tq, tc=tc),
        grid=(N_GROUPS, bsz, nqt),
        in_specs=[pl.BlockSpec((1, tq, GROUP_HEADS * HEAD_DIM), lambda g, b, qi: (b, qi, g)),
                  pl.BlockSpec((1, tq, LANES), lambda g, b, qi: (b, qi, g)),
                  pl.BlockSpec((1, ncp, LANES), lambda g, b, qi: (b, 0, half(g))),
                  pl.BlockSpec((1, LANES, ncp), lambda g, b, qi: (b, half(g), 0)),
                  kv_spec(4), kv_spec(6), kv_spec(0), kv_spec(2),
                  pl.BlockSpec((1, 1, ncp, nq), lambda g, b, qi: (g, qi, 0, 0)),
                  pl.BlockSpec((1, wu, nq), lambda g, b, qi: (g, 0, 0)),
                  pl.BlockSpec((1, ww, nq), lambda g, b, qi: (g, 0, 0)),
                  full(a_m), full(e_bt)],
        out_specs=pl.BlockSpec((1, tq, GROUP_HEADS * HEAD_DIM), lambda g, b, qi: (b, qi, g)),
        out_shape=jax.ShapeDtypeStruct((bsz, seq, d), F32),
        scratch_shapes=[pltpu.VMEM((seq, LANES), BF16), pltpu.VMEM((LANES, seq), BF16),
                        pltpu.VMEM((seq, LANES), BF16), pltpu.VMEM((LANES, seq), BF16),
                        pltpu.VMEM((seq, tq), F32),
                        pltpu.VMEM((1, nq), F32), pltpu.VMEM((1, nq), F32), pltpu.VMEM((LANES, nq), F32)],
        compiler_params=_cparams("parallel", "parallel", "arbitrary"),
        name="nsa_prompt_attn",
    )(qv, gv, kc, vct, kvpv, kvpv, kvwv, kvwv, bias_ct, strip_t, strip_wt, a_m, e_bt)
    return out.reshape(bsz * seq, d)


CMP_PAGES = 16


def _scmp_kernel(pt_ref, *refs):
    del pt_ref
    pages = refs[:CMP_PAGES]
    w1_ref, pe_ref, o_ref, t_scr, x_scr = refs[CMP_PAGES:]
    cpp = pages[0].shape[3] // CMP_STRIDE
    nh = N_GROUPS * CMP_HIDDEN
    for kind in range(2):
        for i, p_ref in enumerate(pages):
            x = p_ref[0, kind].T
            for hh in range(x.shape[1] // LANES):
                t_scr[i, hh] = x[:, hh * LANES:(hh + 1) * LANES]
            for s in range(CMP_STRIDE):
                for hh in range(x.shape[1] // LANES):
                    x_scr[s, i * cpp:(i + 1) * cpp, hh * LANES:(hh + 1) * LANES] = (
                        t_scr[i, hh, pl.ds(s, cpp, stride=CMP_STRIDE), :])
        for part in range(CMP_LEN // CMP_STRIDE):
            acc = jnp.zeros((x_scr.shape[1], nh), F32)
            for s in range(CMP_STRIDE):
                x = (x_scr[s] + pe_ref[kind, part, s:s + 1, :]).astype(BF16)
                acc = acc + _dot(x, w1_ref[kind, part, s])
            o_ref[kind, 0, :, part * nh:(part + 1) * nh] = acc


def _scmp(cache, page_table, w1bd, pe):
    bsz, n_pages = page_table.shape
    _, _, gw, page = cache.shape
    cpp = page // CMP_STRIDE
    nh = N_GROUPS * CMP_HIDDEN
    n_chunks = n_pages * cpp
    page_spec = lambda i: pl.BlockSpec((1, 2, gw, page), lambda b, j, pt: (pt[b, j * CMP_PAGES + i], 0, 0, 0))
    const = lambda a: pl.BlockSpec(a.shape, lambda b, j, pt: (0,) * a.ndim, pipeline_mode=pl.Buffered(1))
    grid_spec = pltpu.PrefetchScalarGridSpec(
        num_scalar_prefetch=1,
        grid=(bsz, n_pages // CMP_PAGES),
        in_specs=[page_spec(i) for i in range(CMP_PAGES)] + [const(w1bd), const(pe)],
        out_specs=pl.BlockSpec((2, 1, CMP_PAGES * cpp, 2 * nh), lambda b, j, pt: (0, b, j, 0)),
        scratch_shapes=[pltpu.VMEM((CMP_PAGES, gw // LANES, page, LANES), F32),
                        pltpu.VMEM((CMP_STRIDE, CMP_PAGES * cpp, gw), F32)])
    return pl.pallas_call(
        _scmp_kernel,
        grid_spec=grid_spec,
        out_shape=jax.ShapeDtypeStruct((2, bsz, n_chunks, 2 * nh), F32),
        compiler_params=_cparams("parallel", "arbitrary"),
        name="compress_sample",
    )(page_table, *([cache] * CMP_PAGES), w1bd, pe)


def _s1_kernel(ab_ref, w2_ref, q_ref, bc_ref, at_ref, gs_ref, oc_ref, it_ref):
    nck = ab_ref.shape[2]
    nh = N_GROUPS * CMP_HIDDEN
    kv = []
    for kind in range(2):
        h = ab_ref[kind, 0, :, :nh] + pltpu.roll(ab_ref[kind, 0, :, nh:], nck - 1, 0)
        kv.append(_dot(jnp.maximum(h, 0.0).astype(BF16), w2_ref[kind]).astype(BF16))
    diag = _head_diag(N_HEADS, D_MODEL)
    qe = jnp.where(diag, q_ref[0] * ATTN_SCALE, 0.0).astype(BF16)
    bc = bc_ref[...]
    pc = _masked_softmax(_dot_nt(qe, kv[0]) + bc, bc > 0.5 * NEG)
    o = _dot(pc.astype(BF16), kv[1])
    oc_ref[0] = jnp.sum(jnp.where(diag, o, 0.0), axis=0, keepdims=True)
    hi = pc.astype(BF16)
    lo = (pc - hi.astype(F32)).astype(BF16)
    imp = _dot(gs_ref[...], hi) + _dot(gs_ref[...], lo)
    it_ref[0] = _split_dot_nt(at_ref[...], imp)


def _s1(ab, w2ex, q, rel_bias, past):
    _, bsz, nck, _ = ab.shape
    d = D_MODEL
    n_cmp = (past + 1 - CMP_LEN) // CMP_STRIDE + 1
    n_blk = -(-(past + 1) // SEL_BLOCK)
    nbp = -(-n_blk // 8) * 8
    n = np.arange(nck)
    dist_c = past - (n * CMP_STRIDE + CMP_LEN - 1)
    bias_c = _bias_from_dist(rel_bias, dist_c, (dist_c >= 0) & (n < n_cmp))
    a_t = jnp.asarray(_imp_to_block_matrix(nck, n_cmp, nbp)[:nbp] * (np.arange(nbp)[:, None] < n_blk), BF16)
    gsum = jnp.asarray(np.kron(np.eye(8, N_GROUPS), np.ones((1, GROUP_HEADS))), BF16)
    full = lambda a: pl.BlockSpec(a.shape, lambda b: (0,) * a.ndim)
    oc, imp_t = pl.pallas_call(
        _s1_kernel,
        grid=(bsz,),
        in_specs=[pl.BlockSpec((2, 1, nck, ab.shape[3]), lambda b: (0, b, 0, 0)), full(w2ex),
                  pl.BlockSpec((1, 1, d), lambda b: (b, 0, 0)), full(bias_c), full(a_t), full(gsum)],
        out_specs=[pl.BlockSpec((1, 1, d), lambda b: (b, 0, 0)), pl.BlockSpec((1, nbp, 8), lambda b: (b, 0, 0))],
        out_shape=[jax.ShapeDtypeStruct((bsz, 1, d), F32), jax.ShapeDtypeStruct((bsz, nbp, 8), F32)],
        compiler_params=_cparams("parallel"),
        name="sample_compressed_attn",
    )(ab, w2ex, q.reshape(bsz, 1, d), bias_c, a_t, gsum)
    return oc.reshape(bsz, d), imp_t


def _s2_kernel(imp_ref, idx_ref, s_scr, *, n_blk, cur):
    shape = imp_ref.shape
    j = lax.broadcasted_iota(jnp.int32, shape, 0)
    forced = jnp.logical_or(j == 0, jnp.logical_or(j == cur, j == cur - 1))
    score = jnp.where(forced, SEL_FORCE, jnp.where(j < n_blk, imp_ref[...], -2.0))
    s_scr[...] = score

    def body(i, rank):
        si = s_scr[pl.ds(i, 1), :]
        beats = jnp.logical_or(si > score, jnp.logical_and(si == score, j > i))
        return rank + jnp.where(beats, 1.0, 0.0)

    rank = lax.fori_loop(0, n_blk, body, jnp.zeros(shape, F32))
    for k in range(SEL_TOPK):
        idx_ref[k:k + 1, :] = jnp.sum(jnp.where(rank == k, j, 0), axis=0, keepdims=True)


def _s2(imp_t, past):
    n_blk = -(-(past + 1) // SEL_BLOCK)
    return pl.pallas_call(
        functools.partial(_s2_kernel, n_blk=n_blk, cur=past // SEL_BLOCK),
        out_shape=jax.ShapeDtypeStruct((SEL_TOPK, imp_t.shape[1]), jnp.int32),
        scratch_shapes=[pltpu.VMEM(imp_t.shape, F32)],
        name="sample_block_topk",
    )(imp_t)


def _bucket_thresholds(max_dist):
    b = _bucket_np(np.arange(max_dist + 1))
    return [int(np.argmax(b >= k)) if (b >= k).any() else max_dist + 1 for k in range(REL_BUCKETS)]


def _attend_t(q, kt, vt, bias, valid, kn, vn, bias_new):
    s = _dot(q.astype(BF16), kt.astype(BF16)) + bias
    if valid is not None:
        s = jnp.where(valid, s, NEG)
    sn = jnp.sum(q * kn, axis=-1, keepdims=True) + bias_new
    m = jnp.maximum(jnp.max(s, axis=-1, keepdims=True), sn)
    e = jnp.exp(s - m)
    en = jnp.exp(sn - m)
    l = jnp.sum(e, axis=-1, keepdims=True) + en
    return (_dot_nt(e.astype(BF16), vt.astype(BF16)) + en * vn) / l


def _ssel_kernel(idx_ref, page_ref, *refs, past, thresholds):
    del page_ref
    blocks = refs[:SEL_TOPK]
    q_ref, n_ref, tb_ref, o_ref = refs[SEL_TOPK:]
    b = pl.program_id(0)
    g = pl.program_id(1)
    page = blocks[0].shape[3]
    kt = jnp.concatenate([blk[0, 0] for blk in blocks], axis=1)
    vt = jnp.concatenate([blk[0, 1] for blk in blocks], axis=1)
    nk = SEL_TOPK * page
    lane = lax.broadcasted_iota(jnp.int32, (1, nk), 1)
    row = lane % page
    pos = jnp.zeros((1, nk), jnp.int32)
    inside = jnp.zeros((1, nk), jnp.int32)
    bpp = page // SEL_BLOCK
    for k in range(SEL_TOPK):
        blk_idx = idx_ref[(b * N_GROUPS + g) * SEL_TOPK + k]
        mine = lane // page == k
        pos = jnp.where(mine, (blk_idx // bpp) * page + row, pos)
        inside = jnp.where(mine, jnp.where(row // SEL_BLOCK == blk_idx % bpp, 1, 0), inside)
    valid = jnp.logical_and(inside == 1, pos < past)
    dist = past - pos
    bias = jnp.broadcast_to(tb_ref[0, 0][:, :1], (8, nk))
    for k in range(1, REL_BUCKETS):
        bias = jnp.where(dist >= thresholds[k], tb_ref[0, k][:, :1], bias)
    q = q_ref[0, 0] * ATTN_SCALE
    o_ref[0, 0] = _attend_t(q, kt, vt, bias, valid, n_ref[0, 0, 0], n_ref[0, 1, 0], tb_ref[0, 0][:, :1])


def _ssel(sel_idx, cache_t, page_table, q, kvp_new, rel_bias, past):
    bsz = q.shape[0]
    page = cache_t.shape[3]
    bpp = page // SEL_BLOCK
    cur = past // SEL_BLOCK
    cached = jnp.minimum(sel_idx, cur - 1)
    pages = jnp.take_along_axis(page_table, (cached // bpp).reshape(bsz, -1), axis=1).reshape(-1)
    q4 = jnp.pad(q.reshape(bsz, N_GROUPS, GROUP_HEADS, HEAD_DIM), ((0, 0), (0, 0), (0, 8 - GROUP_HEADS), (0, 0)))
    new = kvp_new.reshape(bsz, 4, N_GROUPS, 1, HEAD_DIM)
    tb = jnp.transpose(rel_bias.astype(F32).reshape(REL_BUCKETS, N_GROUPS, GROUP_HEADS), (1, 0, 2))
    tb = jnp.pad(tb, ((0, 0), (0, 0), (0, 8 - GROUP_HEADS)))
    tb = jnp.broadcast_to(tb[..., None], tb.shape + (LANES,))
    blk_spec = lambda k: pl.BlockSpec(
        (1, 2, HEAD_DIM, page), lambda b, g, idx, pg: (pg[(b * N_GROUPS + g) * SEL_TOPK + k], 1, g, 0))
    grid_spec = pltpu.PrefetchScalarGridSpec(
        num_scalar_prefetch=2,
        grid=(bsz, N_GROUPS),
        in_specs=[blk_spec(k) for k in range(SEL_TOPK)] + [
            pl.BlockSpec((1, 1, 8, HEAD_DIM), lambda b, g, idx, pg: (b, g, 0, 0)),
            pl.BlockSpec((1, 2, 1, 1, HEAD_DIM), lambda b, g, idx, pg: (b, 1, g, 0, 0)),
            pl.BlockSpec((1, REL_BUCKETS, 8, LANES), lambda b, g, idx, pg: (g, 0, 0, 0))],
        out_specs=pl.BlockSpec((1, 1, 8, HEAD_DIM), lambda b, g, idx, pg: (b, g, 0, 0)))
    out = pl.pallas_call(
        functools.partial(_ssel_kernel, past=past, thresholds=tuple(_bucket_thresholds(past))),
        grid_spec=grid_spec,
        out_shape=jax.ShapeDtypeStruct((bsz, N_GROUPS, 8, HEAD_DIM), F32),
        compiler_params=_cparams("parallel", "parallel"),
        name="sample_selected_attn",
    )(sel_idx.reshape(-1), pages, *([cache_t] * SEL_TOPK), q4, new, tb)
    return out[:, :, :GROUP_HEADS].reshape(bsz, N_HEADS, HEAD_DIM)


def _swin_kernel(c_ref, q_ref, n_ref, m_ref, bw_ref, b0_ref, o_ref, u_ref):
    kt = c_ref[0, 0]
    vt = c_ref[0, 1]
    q = q_ref[0] * ATTN_SCALE
    o = _attend_t(q, kt, vt, bw_ref[...], None, n_ref[0, 0:1, :], n_ref[0, 1:2, :], b0_ref[...][:, :1])
    grp = lax.broadcasted_iota(jnp.int32, (N_HEADS, HEAD_DIM), 0) // GROUP_HEADS
    out = jnp.zeros((N_HEADS, HEAD_DIM), F32)
    for g in range(N_GROUPS):
        out = jnp.where(grp == g, o[:, g * HEAD_DIM:(g + 1) * HEAD_DIM], out)
    o_ref[0] = out
    u_ref[0, 0] = _shift_in(kt, m_ref[0, 0])
    u_ref[0, 1] = _shift_in(vt, m_ref[0, 1])


def _swin(cwin_t, q, kvw_new, rel_bias):
    bsz = q.shape[0]
    _, _, gw, wlen = cwin_t.shape
    onehot = jnp.asarray(np.kron(np.eye(N_GROUPS), np.ones((GROUP_HEADS, 1))), F32)
    qbd = jnp.einsum("bhd,hg->bhgd", q.reshape(bsz, N_HEADS, HEAD_DIM), onehot).reshape(bsz, N_HEADS, gw)
    dist_w = wlen - np.arange(wlen)
    bw = _bias_from_dist(rel_bias, dist_w, dist_w < SLIDE_WIN)
    b0 = _bias_from_dist(rel_bias, np.zeros((LANES,), np.int64), np.ones((LANES,), bool))
    full = lambda a: pl.BlockSpec(a.shape, lambda b: (0,) * a.ndim)
    c_spec = pl.BlockSpec((1, 2, gw, wlen), lambda b: (b, 0, 0, 0))
    o_spec = pl.BlockSpec((1, N_HEADS, HEAD_DIM), lambda b: (b, 0, 0))
    return pl.pallas_call(
        _swin_kernel,
        grid=(bsz,),
        in_specs=[c_spec, pl.BlockSpec((1, N_HEADS, gw), lambda b: (b, 0, 0)),
                  pl.BlockSpec((1, 2, gw), lambda b: (b, 0, 0)),
                  pl.BlockSpec((1, 2, gw, 1), lambda b: (b, 0, 0, 0)), full(bw), full(b0)],
        out_specs=[o_spec, c_spec],
        out_shape=[jax.ShapeDtypeStruct((bsz, N_HEADS, HEAD_DIM), F32), jax.ShapeDtypeStruct(cwin_t.shape, F32)],
        compiler_params=_cparams("parallel"),
        name="sample_window_attn",
    )(cwin_t, qbd, kvw_new.reshape(bsz, 2, gw), kvw_new.reshape(bsz, 2, gw, 1), bw, b0)


def _gate_sum_kernel(g_ref, oc_ref, os_ref, ow_ref, o_ref):
    g = g_ref[...]
    o_ref[...] = g[:, :, 0:1] * oc_ref[...] + g[:, :, 1:2] * os_ref[...] + g[:, :, 2:3] * ow_ref[...]


def _gate_sum(gate, oc, os_, ow):
    return pl.pallas_call(
        _gate_sum_kernel,
        out_shape=jax.ShapeDtypeStruct(oc.shape, F32),
        name="sample_gate_sum",
    )(gate, oc, os_, ow)


def _split_a_weights(w_in):
    d = D_MODEL
    w = w_in.astype(BF16)
    qs = [w[:, g * 3 * d:g * 3 * d + d] for g in range(len(DIL_PATTERNS))]
    kvs = [w[:, g * 3 * d + d:(g + 1) * 3 * d] for g in range(len(DIL_PATTERNS))]
    return qs + kvs


def _split_b_weights(w_in):
    d = D_MODEL
    gw = N_GROUPS * HEAD_DIM
    w = w_in.astype(BF16)
    wg = w[:, d + 6 * gw:].reshape(d, N_GROUPS, GROUP_HEADS * 3)
    wg = jnp.pad(wg, ((0, 0), (0, 0), (0, LANES - GROUP_HEADS * 3))).reshape(d, N_GROUPS * LANES)
    return [w[:, :d], w[:, d:d + 4 * gw], w[:, d + 4 * gw:d + 6 * gw], wg]


def _unblock_gate(gate):
    m = gate.shape[0]
    return gate.reshape(m, N_GROUPS, LANES)[:, :, :GROUP_HEADS * 3].reshape(m, N_HEADS * 3)


def kernel(x_prompt, x_sample, cache_a_win0, cache_a_win1, cache_a_win2, cache_b_kv, cache_b_win, page_table,
           rel_bias, norm_mix, norm_mlp, norm_final, a_w_in, a_w_out, b_w_in, b_w_out, b_phi_pe, b_phi_w1,
           b_phi_w2, mlp_w_up, mlp_w_down):
    bsz, seq, d = x_prompt.shape
    sb = x_sample.shape[0]
    past = page_table.shape[1] * cache_b_kv.shape[2]
    tm_p, tm_s = 512, sb
    xp = x_prompt.reshape(bsz * seq, d)
    xs = x_sample.reshape(sb, d)
    acts4 = ("none",) * 4
    acts6 = ("none",) * 6

    wa = _split_a_weights(a_w_in[0])
    wa_out = a_w_out[0].astype(BF16)
    pa = _norm_matmul(xp, norm_mix[0], wa, acts6, tm_p, 4)
    sa = _norm_matmul(xs, norm_mix[0], wa, acts6, tm_s, 4)
    os_, ls = [], []
    for gi, (_, dil) in enumerate(DIL_PATTERNS):
        o, lse = _swa_group(pa[gi], pa[3 + gi], rel_bias, bsz, seq, dil)
        os_.append(o)
        ls.append(lse)
    xp = _merge_proj(os_, ls, xp, wa_out, tm_p)
    a_caches = [_rows_last(c[0]) for c in (cache_a_win0, cache_a_win1, cache_a_win2)]
    oa, a_new_t = _samp_a(sa[:3], sa[3:], a_caches, rel_bias)
    xs = _matmul_res(oa, xs, wa_out, tm_s)
    a_new_p = [pa[3 + gi].reshape(bsz, seq, 2 * d)[:, seq - min(w, seq):] for gi, (w, _) in enumerate(DIL_PATTERNS)]
    a_new_s = [_rows_second(c, (2, N_HEADS, HEAD_DIM)) for c in a_new_t]

    wu0, wd0 = mlp_w_up[0].astype(BF16), mlp_w_down[0].astype(BF16)
    xp = _mlp(xp, norm_mlp[0], wu0, wd0, norm_final, False, tm_p, 1024)
    xs = _mlp(xs, norm_mlp[0], wu0, wd0, norm_final, False, tm_s, 1024)

    wb = _split_b_weights(b_w_in[0])
    wb_out = b_w_out[0].astype(BF16)
    acts_b = ("none", "none", "none", "sigmoid")
    qp, kvp_p, kvw_p, gate_p = _norm_matmul(xp, norm_mix[1], wb, acts_b, tm_p, 1)
    qs_, kvp_s, kvw_s, gate_s = _norm_matmul(xs, norm_mix[1], wb, acts_b, tm_s, 1)
    w1bd, w2bd, w2ex, pe = _cmp_weights(b_phi_pe[0], b_phi_w1[0], b_phi_w2[0])
    kc = _cmp_prompt(kvp_p, bsz, seq, 0, w1bd, pe, w2bd, False)
    vct = _cmp_prompt(kvp_p, bsz, seq, 1, w1bd, pe, w2bd, True)
    ob = _nsa_prompt(qp, gate_p, kc, vct, kvp_p, kvw_p, rel_bias, bsz, seq)
    xp = _matmul_res(ob, xp, wb_out, tm_p)

    cache_t = _rows_last(cache_b_kv[0])
    cwin_t = _rows_last(cache_b_win[0])
    ab = _scmp(cache_t, page_table, w1bd, pe)
    oc, imp_t = _s1(ab, w2ex, qs_, rel_bias, past)
    imp_t = jnp.transpose(imp_t[:, :, :N_GROUPS], (1, 0, 2)).reshape(imp_t.shape[1], sb * N_GROUPS)
    sel_idx = jnp.transpose(_s2(imp_t, past)).reshape(sb, N_GROUPS, SEL_TOPK)
    osel = _ssel(sel_idx, cache_t, page_table, qs_, kvp_s, rel_bias, past)
    owin, b_win_t = _swin(cwin_t, qs_, kvw_s, rel_bias)
    obs = _gate_sum(_unblock_gate(gate_s).reshape(sb, N_HEADS, 3), oc.reshape(sb, N_HEADS, HEAD_DIM), osel, owin)
    xs = _matmul_res(obs.reshape(sb, d), xs, wb_out, tm_s)
    b_win_s = _rows_second(b_win_t, (2, N_GROUPS, HEAD_DIM))

    wu1, wd1 = mlp_w_up[1].astype(BF16), mlp_w_down[1].astype(BF16)
    yp = _mlp(xp, norm_mlp[1], wu1, wd1, norm_final, True, tm_p, 1024)
    ys = _mlp(xs, norm_mlp[1], wu1, wd1, norm_final, True, tm_s, 1024)

    gw = N_GROUPS * HEAD_DIM
    keep = min(SLIDE_WIN, seq)
    cache_shape = lambda n: (1, -1, n, 2, N_HEADS, HEAD_DIM)
    outs = [yp.reshape(bsz, seq, d), ys.reshape(sb, 1, d)]
    for p, s in zip(a_new_p, a_new_s):
        outs.append(p.reshape(cache_shape(p.shape[1])))
        outs.append(s[None])
    outs.append(kvp_p.reshape(1, bsz, seq, 4, N_GROUPS, HEAD_DIM))
    outs.append(kvp_s.reshape(1, sb, 1, 4, N_GROUPS, HEAD_DIM))
    outs.append(kvw_p.reshape(bsz, seq, 2 * gw)[:, seq - keep:].reshape(1, bsz, keep, 2, N_GROUPS, HEAD_DIM))
    outs.append(b_win_s[None])
    return tuple(outs)
```

```python
import functools
import math

import numpy as np
import jax
import jax.numpy as jnp
from jax import lax
from jax.experimental import pallas as pl
from jax.experimental.pallas import tpu as pltpu

F32 = jnp.float32
BF16 = jnp.bfloat16

D_MODEL = 1024
HEAD_DIM = 64
N_HEADS = 16
ATTN_SCALE = HEAD_DIM ** -0.5
RMS_EPS = 1e-6
REL_BUCKETS = 32
REL_MAX_DIST = 2048
DIL_PATTERNS = ((128, 1), (512, 4), (2048, 16))
A_KEYS = 128
N_GROUPS = 4
GROUP_HEADS = N_HEADS // N_GROUPS
CMP_LEN = 32
CMP_STRIDE = 16
CMP_HIDDEN = 128
SEL_BLOCK = 64
SEL_TOPK = 16
SEL_FORCE = 1e4
SLIDE_WIN = 512

LANES = 128
VMEM_LIMIT = 56 * 1024 * 1024
NEG = -1e30

NT_DIMS = (((1,), (1,)), ((), ()))


def _cparams(*sem):
    return pltpu.CompilerParams(dimension_semantics=sem, vmem_limit_bytes=VMEM_LIMIT)


def _dot(a, b):
    return jnp.dot(a, b, preferred_element_type=F32)


def _dot_nt(a, b):
    return lax.dot_general(a, b, NT_DIMS, preferred_element_type=F32)


def _split_dot(x, w):
    hi = x.astype(BF16)
    lo = (x - hi.astype(F32)).astype(BF16)
    return _dot(hi, w) + _dot(lo, w)


def _split_dot_nt(w, x):
    hi = x.astype(BF16)
    lo = (x - hi.astype(F32)).astype(BF16)
    return _dot_nt(w, hi) + _dot_nt(w, lo)


def _rms(x, g):
    ms = jnp.mean(x * x, axis=-1, keepdims=True)
    return x * lax.rsqrt(ms + RMS_EPS) * g


def _bucket_np(dist):
    dist = np.maximum(np.asarray(dist, np.int64), 0)
    max_exact = REL_BUCKETS // 2
    ratio = (np.log(np.maximum(dist, 1).astype(np.float32) / np.float32(max_exact))
             / np.float32(math.log(REL_MAX_DIST / max_exact)))
    large = np.minimum(max_exact + (ratio * (REL_BUCKETS - max_exact)).astype(np.int32), REL_BUCKETS - 1)
    return np.where(dist < max_exact, dist, large).astype(np.int32)


def _bias_from_dist(rel_bias, dist, valid):
    idx = jnp.asarray(_bucket_np(dist))
    b = jnp.take(rel_bias.astype(F32).T, idx, axis=1)
    return jnp.where(jnp.asarray(valid)[None], b, NEG)


def _toeplitz(w, n, m):
    length = n + m
    lead = w.shape[:-1]
    t = jnp.tile(w, (1,) * len(lead) + (n,))[..., :n * (length - 1)].reshape(lead + (n, length - 1))
    return t[..., n - 1:n - 1 + m]


def _toeplitz_bias(rel_bias, n, m, dist_of, valid_of):
    x = np.arange(n + m) - (n - 1)
    return _toeplitz(_bias_from_dist(rel_bias, dist_of(x), valid_of(x)), n, m)


def _norm_matmul_kernel(x_ref, g_ref, *refs, n_w, acts):
    w_refs, o_refs, xn_ref = refs[:n_w], refs[n_w:2 * n_w], refs[2 * n_w]

    @pl.when(pl.program_id(1) == 0)
    def _():
        xn_ref[...] = _rms(x_ref[...], g_ref[...]).astype(BF16)

    xn = xn_ref[...]
    for w_ref, o_ref, act in zip(w_refs, o_refs, acts):
        y = _dot(xn, w_ref[...])
        if act == "sigmoid":
            y = jax.nn.sigmoid(y)
        o_ref[...] = y


def _norm_matmul(x, g, ws, acts, tm, nj):
    m, d = x.shape
    kern = functools.partial(_norm_matmul_kernel, n_w=len(ws), acts=tuple(acts))
    in_specs = [pl.BlockSpec((tm, d), lambda i, j: (i, 0)), pl.BlockSpec((1, d), lambda i, j: (0, 0))]
    in_specs += [pl.BlockSpec((d, w.shape[1] // nj), lambda i, j: (0, j)) for w in ws]
    out_specs = [pl.BlockSpec((tm, w.shape[1] // nj), lambda i, j: (i, j)) for w in ws]
    return pl.pallas_call(
        kern,
        grid=(m // tm, nj),
        in_specs=in_specs,
        out_specs=out_specs,
        out_shape=[jax.ShapeDtypeStruct((m, w.shape[1]), F32) for w in ws],
        scratch_shapes=[pltpu.VMEM((tm, d), BF16)],
        compiler_params=_cparams("parallel", "arbitrary"),
        name="norm_matmul",
    )(x, g.reshape(1, d), *ws)


def _matmul_res_kernel(a_ref, x_ref, w_ref, o_ref):
    o_ref[...] = x_ref[...] + _dot(a_ref[...].astype(BF16), w_ref[...])


def _matmul_res(a, x, w, tm):
    m, d = x.shape
    return pl.pallas_call(
        _matmul_res_kernel,
        grid=(m // tm,),
        in_specs=[pl.BlockSpec((tm, a.shape[1]), lambda i: (i, 0)),
                  pl.BlockSpec((tm, d), lambda i: (i, 0)),
                  pl.BlockSpec(w.shape, lambda i: (0, 0))],
        out_specs=pl.BlockSpec((tm, d), lambda i: (i, 0)),
        out_shape=jax.ShapeDtypeStruct((m, d), F32),
        compiler_params=_cparams("parallel"),
        name="matmul_res",
    )(a, x, w)


def _merge_proj_kernel(o0, o1, o2, l0, l1, l2, e_ref, x_ref, w_ref, out_ref):
    ls = [l0[...], l1[...], l2[...]]
    mx = jnp.maximum(jnp.maximum(ls[0], ls[1]), ls[2])
    es = [jnp.exp(l - mx) for l in ls]
    tot = es[0] + es[1] + es[2]
    merged = None
    for e, o in zip(es, (o0, o1, o2)):
        part = _split_dot(e / tot, e_ref[...]) * o[...]
        merged = part if merged is None else merged + part
    out_ref[...] = x_ref[...] + _dot(merged.astype(BF16), w_ref[...])


def _merge_proj(os_, ls, x, w, tm):
    m, d = x.shape
    expand = jnp.asarray(np.kron(np.eye(LANES, N_HEADS), np.ones((1, HEAD_DIM))), BF16)
    row = lambda n: pl.BlockSpec((tm, n), lambda i: (i, 0))
    return pl.pallas_call(
        _merge_proj_kernel,
        grid=(m // tm,),
        in_specs=[row(d)] * 3 + [row(LANES)] * 3 + [
            pl.BlockSpec(expand.shape, lambda i: (0, 0)), row(d), pl.BlockSpec(w.shape, lambda i: (0, 0))],
        out_specs=row(d),
        out_shape=jax.ShapeDtypeStruct((m, d), F32),
        compiler_params=_cparams("parallel"),
        name="merge_proj",
    )(*os_, *ls, expand, x, w)


def _mlp_kernel(x_ref, g_ref, wu_ref, wd_ref, gf_ref, o_ref, xn_ref, acc_ref, *, final_norm):
    f = pl.program_id(1)

    @pl.when(f == 0)
    def _():
        xn_ref[...] = _rms(x_ref[...], g_ref[...]).astype(BF16)
        acc_ref[...] = jnp.zeros_like(acc_ref)

    h = jnp.square(jnp.maximum(_dot(xn_ref[...], wu_ref[...]), 0.0))
    acc_ref[...] += _dot(h.astype(BF16), wd_ref[...])

    @pl.when(f == pl.num_programs(1) - 1)
    def _():
        y = x_ref[...] + acc_ref[...]
        if final_norm:
            y = _rms(y, gf_ref[...])
        o_ref[...] = y


def _mlp(x, g, wu, wd, gf, final_norm, tm, tf):
    m, d = x.shape
    dff = wu.shape[1]
    return pl.pallas_call(
        functools.partial(_mlp_kernel, final_norm=final_norm),
        grid=(m // tm, dff // tf),
        in_specs=[pl.BlockSpec((tm, d), lambda i, f: (i, 0)),
                  pl.BlockSpec((1, d), lambda i, f: (0, 0)),
                  pl.BlockSpec((d, tf), lambda i, f: (0, f)),
                  pl.BlockSpec((tf, d), lambda i, f: (f, 0)),
                  pl.BlockSpec((1, d), lambda i, f: (0, 0))],
        out_specs=pl.BlockSpec((tm, d), lambda i, f: (i, 0)),
        out_shape=jax.ShapeDtypeStruct((m, d), F32),
        scratch_shapes=[pltpu.VMEM((tm, d), BF16), pltpu.VMEM((tm, d), F32)],
        compiler_params=_cparams("parallel", "arbitrary"),
        name="mlp",
    )(x, g.reshape(1, d), wu, wd, gf.reshape(1, d))


def _swa_kernel(q_ref, k_ref, v_ref, b_ref, o_ref, lse_ref, *, dil):
    t = A_KEYS
    hp = pl.program_id(1)
    tiles = q_ref.shape[1] // (dil * t)
    lane = lax.broadcasted_iota(jnp.int32, (t, LANES), 1)
    low = lane < HEAD_DIM

    @pl.when(hp == 0)
    def _():
        lse_ref[...] = jnp.zeros_like(lse_ref)

    def tile(r, qi):
        rows = pl.ds(r + dil * t * qi, t, stride=dil)
        prev = pl.ds(r + dil * t * max(qi - 1, 0), t, stride=dil)
        first = min(qi, 1)
        q2 = (q_ref[0, rows, :] * ATTN_SCALE).astype(BF16)
        k2 = jnp.concatenate([k_ref[0, prev, :], k_ref[0, rows, :]], axis=0).astype(BF16)
        v2 = jnp.concatenate([v_ref[0, prev, :], v_ref[0, rows, :]], axis=0).astype(BF16)
        lse_t = lse_ref[0, rows, :]
        outs = []
        for a in range(2):
            h = 2 * hp + a
            qm = jnp.where(low if a == 0 else jnp.logical_not(low), q2, jnp.zeros_like(q2))
            s = _dot_nt(qm, k2) + b_ref[first, h]
            m = jnp.max(s, axis=-1, keepdims=True)
            e = jnp.exp(s - m)
            l = jnp.sum(e, axis=-1, keepdims=True)
            outs.append(_dot(e.astype(BF16), v2) / l)
            lse_t = jnp.where(lane == h, m + jnp.log(l), lse_t)
        o_ref[0, rows, :] = jnp.where(low, outs[0], outs[1])
        lse_ref[0, rows, :] = lse_t

    for r in range(dil):
        for qi in range(tiles):
            tile(r, qi)


def _swa_bias(rel_bias, dil, t):
    later = _toeplitz_bias(rel_bias, t, 2 * t, lambda x: dil * (t - x), lambda x: (t - x >= 0) & (t - x <= t))
    first = jnp.where(jnp.asarray(np.arange(2 * t) >= t)[None, None, :], later, NEG)
    return jnp.stack([first, later])


def _swa_group(q, kv, rel_bias, bsz, seq, dil):
    d = D_MODEL
    nhp = N_HEADS // 2
    assert seq % (dil * A_KEYS) == 0
    bias = _swa_bias(rel_bias, dil, A_KEYS)
    pair = lambda off: pl.BlockSpec((1, seq, LANES), lambda b, hp: (b, 0, off + hp))
    o, lse = pl.pallas_call(
        functools.partial(_swa_kernel, dil=dil),
        grid=(bsz, nhp),
        in_specs=[pair(0), pair(0), pair(nhp), pl.BlockSpec(bias.shape, lambda b, hp: (0, 0, 0, 0))],
        out_specs=[pair(0), pl.BlockSpec((1, seq, LANES), lambda b, hp: (b, 0, 0))],
        out_shape=[jax.ShapeDtypeStruct((bsz, seq, d), F32), jax.ShapeDtypeStruct((bsz, seq, LANES), F32)],
        compiler_params=_cparams("parallel", "arbitrary"),
        name="swa_attn",
    )(q.reshape(bsz, seq, d), kv.reshape(bsz, seq, 2 * d), kv.reshape(bsz, seq, 2 * d), bias)
    return o.reshape(bsz * seq, d), lse.reshape(bsz * seq, LANES)


def _head_diag(rows, width):
    row = lax.broadcasted_iota(jnp.int32, (rows, width), 0)
    lane = lax.broadcasted_iota(jnp.int32, (rows, width), 1)
    return (lane // HEAD_DIM) == row


def _rows_last(cache):
    nd = cache.ndim
    t = jnp.transpose(cache, (0,) + tuple(range(2, nd)) + (1,))
    bsz, window = cache.shape[0], cache.shape[1]
    lead = int(np.prod(cache.shape[2:nd - 2]))
    return t.reshape(bsz, lead, cache.shape[nd - 2] * cache.shape[nd - 1], window)


def _rows_second(cache_t, feat):
    bsz, _, _, window = cache_t.shape
    nd = len(feat) + 2
    t = cache_t.reshape((bsz,) + tuple(feat) + (window,))
    return jnp.transpose(t, (0, nd - 1) + tuple(range(1, nd - 1)))


def _shift_in(x, col):
    w = x.shape[1]
    lane = lax.broadcasted_iota(jnp.int32, x.shape, 1)
    return jnp.where(lane == w - 1, col, pltpu.roll(x, w - 1, 1))


A_HEAD_CHUNK = 4


def _samp_a_kernel(q0, q1, q2, n0, n1, n2, m0, m1, m2, c0, c1, c2, b0, b1, b2, bs_ref, o_ref, u0, u1, u2):
    hw = A_HEAD_CHUNK * HEAD_DIM
    diag = _head_diag(8, hw)
    outs, lses = [], []
    for q_ref, n_ref, m_ref, c_ref, b_ref, u_ref in (
            (q0, n0, m0, c0, b0, u0), (q1, n1, m1, c1, b1, u1), (q2, n2, m2, c2, b2, u2)):
        qe = jnp.where(diag, q_ref[0] * ATTN_SCALE, 0.0)
        kt = c_ref[0, 0]
        vt = c_ref[0, 1]
        kn = n_ref[0, 0:1, :]
        vn = n_ref[0, 1:2, :]
        s = _dot(qe.astype(BF16), kt.astype(BF16)) + b_ref[0]
        sn = jnp.sum(qe * kn, axis=-1, keepdims=True) + bs_ref[0][:, :1]
        m = jnp.maximum(jnp.max(s, axis=-1, keepdims=True), sn)
        e = jnp.exp(s - m)
        en = jnp.exp(sn - m)
        l = jnp.sum(e, axis=-1, keepdims=True) + en
        outs.append((_dot_nt(e.astype(BF16), vt.astype(BF16)) + en * vn) / l)
        lses.append(m + jnp.log(l))
        u_ref[0, 0] = _shift_in(kt, m_ref[0, 0])
        u_ref[0, 1] = _shift_in(vt, m_ref[0, 1])
    mx = jnp.maximum(jnp.maximum(lses[0], lses[1]), lses[2])
    es = [jnp.exp(l - mx) for l in lses]
    tot = es[0] + es[1] + es[2]
    merged = (es[0] / tot) * outs[0] + (es[1] / tot) * outs[1] + (es[2] / tot) * outs[2]
    o_ref[0] = jnp.sum(jnp.where(diag, merged, 0.0), axis=0, keepdims=True)


def _samp_a(qs, kvs, caches_t, rel_bias):
    bsz, d = qs[0].shape
    hw = A_HEAD_CHUNK * HEAD_DIM
    nhc = N_HEADS // A_HEAD_CHUNK
    pad_rows = lambda a: jnp.pad(a.reshape(nhc, A_HEAD_CHUNK, -1), ((0, 0), (0, 8 - A_HEAD_CHUNK), (0, 0)))
    biases = []
    for window, dil in DIL_PATTERNS:
        dist = window - np.arange(window)
        biases.append(pad_rows(_bias_from_dist(rel_bias, dist, dist % dil == 0)))
    bself = pad_rows(_bias_from_dist(rel_bias, np.zeros((LANES,), np.int64), np.ones((LANES,), bool)))
    q_spec = pl.BlockSpec((1, 1, hw), lambda b, c: (b, 0, c))
    n_spec = pl.BlockSpec((1, 2, hw), lambda b, c: (b, 0, c))
    m_spec = pl.BlockSpec((1, 2, hw, 1), lambda b, c: (b, 0, c, 0))
    c_spec = lambda w: pl.BlockSpec((1, 2, hw, w), lambda b, c: (b, 0, c, 0))
    b_spec = lambda w: pl.BlockSpec((1, 8, w), lambda b, c: (c, 0, 0))
    windows = [w for w, _ in DIL_PATTERNS]
    res = pl.pallas_call(
        _samp_a_kernel,
        grid=(bsz, nhc),
        in_specs=[q_spec] * 3 + [n_spec] * 3 + [m_spec] * 3 + [c_spec(w) for w in windows]
        + [b_spec(w) for w in windows] + [b_spec(LANES)],
        out_specs=[q_spec] + [c_spec(w) for w in windows],
        out_shape=[jax.ShapeDtypeStruct((bsz, 1, d), F32)] + [jax.ShapeDtypeStruct(c.shape, F32) for c in caches_t],
        compiler_params=_cparams("parallel", "parallel"),
        name="sample_dilated_attn",
    )(*[q.reshape(bsz, 1, d) for q in qs], *[kv.reshape(bsz, 2, d) for kv in kvs],
      *[kv.reshape(bsz, 2, d, 1) for kv in kvs], *caches_t, *biases, bself)
    return res[0].reshape(bsz, d), res[1:]


def _cmp_weights(phi_pe, phi_w1, phi_w2):
    r = CMP_LEN // CMP_STRIDE
    eye = jnp.eye(N_GROUPS, dtype=F32)
    w1r = phi_w1.reshape(2, r, CMP_STRIDE, HEAD_DIM, CMP_HIDDEN)
    w1bd = jnp.einsum("kpsdh,gG->kpsgdGh", w1r, eye).reshape(
        2, r, CMP_STRIDE, N_GROUPS * HEAD_DIM, N_GROUPS * CMP_HIDDEN).astype(BF16)
    w2bd = jnp.einsum("khd,gG->kghGd", phi_w2, eye).reshape(
        2, N_GROUPS * CMP_HIDDEN, N_GROUPS * HEAD_DIM).astype(BF16)
    w2ex = jnp.einsum("khd,gG,r->kghGrd", phi_w2, eye, jnp.ones((GROUP_HEADS,), F32)).reshape(
        2, N_GROUPS * CMP_HIDDEN, D_MODEL).astype(BF16)
    pe = jnp.tile(phi_pe.reshape(2, r, CMP_STRIDE, 1, HEAD_DIM), (1, 1, 1, N_GROUPS, 1)).reshape(
        2, r, CMP_STRIDE, N_GROUPS * HEAD_DIM)
    return w1bd, w2bd, w2ex, pe


def _cmp_prompt_kernel(x_ref, w1_ref, pe_ref, w2_ref, o_ref, *, kind, transposed):
    nch = x_ref.shape[1]
    gw = N_GROUPS * HEAD_DIM
    hid = []
    for part in range(CMP_LEN // CMP_STRIDE):
        acc = jnp.zeros((nch, N_GROUPS * CMP_HIDDEN), F32)
        for s in range(CMP_STRIDE):
            c0 = s * 4 * gw + kind * gw
            x = (x_ref[0, :, c0:c0 + gw] + pe_ref[part, s:s + 1, :]).astype(BF16)
            acc = acc + _dot(x, w1_ref[part, s])
        hid.append(acc)
    h = hid[0] + pltpu.roll(hid[1], nch - 1, 0)
    h = jnp.maximum(h, 0.0).astype(BF16)
    o_ref[0] = _dot_nt(w2_ref[...], h) if transposed else _dot(h, w2_ref[...])


def _cmp_prompt(kvp, bsz, seq, kind, w1bd, pe, w2bd, transposed):
    nch = seq // CMP_STRIDE
    xv = kvp.reshape(bsz, nch, CMP_STRIDE * kvp.shape[1])
    full = lambda a: pl.BlockSpec(a.shape, lambda b: (0,) * a.ndim)
    w1, p, w2 = w1bd[kind], pe[kind], w2bd[kind]
    out_dims = (w2.shape[1], nch) if transposed else (nch, w2.shape[1])
    if transposed:
        w2 = w2.T
    return pl.pallas_call(
        functools.partial(_cmp_prompt_kernel, kind=kind, transposed=transposed),
        grid=(bsz,),
        in_specs=[pl.BlockSpec((1, nch, xv.shape[2]), lambda b: (b, 0, 0)), full(w1), full(p), full(w2)],
        out_specs=pl.BlockSpec((1,) + out_dims, lambda b: (b, 0, 0)),
        out_shape=jax.ShapeDtypeStruct((bsz,) + out_dims, F32),
        compiler_params=_cparams("parallel"),
        name="compress_prompt",
    )(xv, w1, p, w2)


def _masked_softmax(s, valid, axis=-1):
    m = jnp.max(s, axis=axis, keepdims=True)
    m = jnp.where(m > 0.5 * NEG, m, 0.0)
    e = jnp.where(valid, jnp.exp(s - m), 0.0)
    return e / jnp.maximum(jnp.sum(e, axis=axis, keepdims=True), 1e-30)


def _topk_mask_t(score, j, n_iter):
    rank = jnp.zeros(score.shape, F32)
    for i in range(n_iter):
        si = score[i:i + 1, :]
        beats = jnp.logical_or(si > score, jnp.logical_and(si == score, j > i))
        rank = rank + jnp.where(beats, 1.0, 0.0)
    return jnp.where(rank < SEL_TOPK, 1.0, 0.0)


def _nsa_kernel(q_ref, gate_ref, kc_ref, vct_ref, ks_ref, vs_ref, kw_ref, vw_ref, bct_ref, ut_ref, uwt_ref,
                a_ref, ebt_ref, o_ref, ksb, vst, kwb, vwt, selb, m_ref, l_ref, acc_ref, *, seq, tq, tc):
    g = pl.program_id(0)
    qi = pl.program_id(2)
    q0 = qi * tq
    par = g % 2
    nb = seq // SEL_BLOCK
    lane = lax.broadcasted_iota(jnp.int32, (tq, LANES), 1)
    keep = (lane // HEAD_DIM) == par

    @pl.when(qi == 0)
    def _():
        ksb[...] = ks_ref[0].astype(BF16)
        kwb[...] = kw_ref[0].astype(BF16)
        for c in range(seq // (2 * LANES)):
            sl = slice(c * 2 * LANES, (c + 1) * 2 * LANES)
            vst[:, sl] = vs_ref[0, sl, :].T.astype(BF16)
            vwt[:, sl] = vw_ref[0, sl, :].T.astype(BF16)

    qs = []
    for r in range(GROUP_HEADS):
        qh = q_ref[0, :, (r // 2) * LANES:(r // 2 + 1) * LANES] * ATTN_SCALE
        qh = jnp.where(par == (r % 2), qh, pltpu.roll(qh, HEAD_DIM, 1))
        qs.append(jnp.where(keep, qh, 0.0).astype(BF16))
    q4 = jnp.concatenate(qs, axis=0)

    bct = bct_ref[0, 0]
    pc = _masked_softmax(_dot_nt(kc_ref[0].astype(BF16), q4) + bct, bct > 0.5 * NEG, axis=0)
    oc = _dot(vct_ref[0].astype(BF16), pc.astype(BF16))
    imp = pc[:, 0:tq] + pc[:, tq:2 * tq] + pc[:, 2 * tq:3 * tq] + pc[:, 3 * tq:4 * tq]

    hi = imp.astype(BF16)
    lo = (imp - hi.astype(F32)).astype(BF16)
    imp_t = _dot(a_ref[...], hi) + _dot(a_ref[...], lo)
    j = lax.broadcasted_iota(jnp.int32, (nb, tq), 0)
    qpos = q0 + lax.broadcasted_iota(jnp.int32, (nb, tq), 1)
    cur = qpos // SEL_BLOCK
    forced = jnp.logical_or(j == 0, jnp.logical_or(j == cur, j == cur - 1))
    score = jnp.where(forced, SEL_FORCE, jnp.where(j * SEL_BLOCK <= qpos, imp_t, -1.0))
    sel_t = _topk_mask_t(score, j, nb).astype(BF16)
    sb = (_dot(ebt_ref[...], sel_t) - 1.0) * (-NEG)
    selb[...] = sb

    m_ref[...] = jnp.full(m_ref.shape, NEG, F32)
    l_ref[...] = jnp.zeros(l_ref.shape, F32)
    acc_ref[...] = jnp.zeros(acc_ref.shape, F32)

    def sel_chunk(t, carry):
        k0 = pl.multiple_of(t * tc, tc)
        u0 = pl.multiple_of(k0 - q0 + (seq - tq), tq)
        sbk = selb[pl.ds(k0, tc), :]
        s = (_dot_nt(ksb[pl.ds(k0, tc), :], q4) + ut_ref[0, pl.ds(u0, tc), :]
             + jnp.concatenate([sbk] * GROUP_HEADS, axis=1))
        m_prev = m_ref[...]
        m_new = jnp.maximum(m_prev, jnp.max(s, axis=0, keepdims=True))
        alpha = jnp.exp(m_prev - m_new)
        e = jnp.exp(s - m_new)
        l_ref[...] = alpha * l_ref[...] + jnp.sum(e, axis=0, keepdims=True)
        acc_ref[...] = alpha * acc_ref[...] + _dot(vst[:, pl.ds(k0, tc)], e.astype(BF16))
        m_ref[...] = m_new
        return carry

    lax.fori_loop(0, (q0 + tq + tc - 1) // tc, sel_chunk, 0)
    osel = acc_ref[...] / l_ref[...]

    ww = SLIDE_WIN + tq
    w0 = pl.multiple_of(jnp.maximum(q0 - SLIDE_WIN, 0), tq)
    uw0 = pl.multiple_of(w0 - (q0 - SLIDE_WIN), tq)
    s = _dot_nt(kwb[pl.ds(w0, ww), :], q4) + uwt_ref[0, pl.ds(uw0, ww), :]
    e = jnp.exp(s - jnp.max(s, axis=0, keepdims=True))
    owin = _dot(vwt[:, pl.ds(w0, ww)], e.astype(BF16)) / jnp.sum(e, axis=0, keepdims=True)

    gt = gate_ref[0].T
    rows = []
    for r in range(GROUP_HEADS):
        cs = slice(r * tq, (r + 1) * tq)
        o = (gt[3 * r:3 * r + 1, :] * oc[:, cs] + gt[3 * r + 1:3 * r + 2, :] * osel[:, cs]
             + gt[3 * r + 2:3 * r + 3, :] * owin[:, cs])
        rows.append(jnp.where(par == 0, o[:HEAD_DIM], o[HEAD_DIM:]))
    o_ref[0] = jnp.concatenate(rows, axis=0).T


def _imp_to_block_matrix(n_cmp_pad, n_cmp, n_blk):
    ratio = SEL_BLOCK // CMP_STRIDE
    span = CMP_LEN // CMP_STRIDE
    a = np.zeros((n_blk, n_cmp_pad), np.float32)
    for m in range(ratio):
        for n in range(span):
            for jb in range(n_blk):
                c = ratio * jb + m - n
                if 0 <= c < n_cmp:
                    a[jb, c] += 1.0
    return a


def _heads_to_lanes(t, lead):
    nl = len(lead)
    rows, tq = t.shape[-2:]
    t = t.reshape((N_GROUPS, GROUP_HEADS) + tuple(lead) + (rows, tq))
    t = jnp.transpose(t, (0,) + tuple(range(2, 2 + nl)) + (2 + nl, 1, 3 + nl))
    return t.reshape((N_GROUPS,) + tuple(lead) + (rows, GROUP_HEADS * tq))


def _nsa_prompt(q, gate, kc, vct, kvp, kvw, rel_bias, bsz, seq):
    tq, tc = LANES, 4 * LANES
    d = D_MODEL
    nb = seq // SEL_BLOCK
    ncp = seq // CMP_STRIDE
    nqt = seq // tq
    n_cmp = (seq - CMP_LEN) // CMP_STRIDE + 1
    nq = GROUP_HEADS * tq
    assert ncp == LANES and SLIDE_WIN % tq == 0
    qv = q.reshape(bsz, seq, d)
    gv = gate.reshape(bsz, seq, N_GROUPS * LANES)
    kvpv = kvp.reshape(bsz, seq, kvp.shape[1])
    kvwv = kvw.reshape(bsz, seq, kvw.shape[1])

    sub = np.arange(CMP_STRIDE)[:, None]
    x = (np.arange(2 * ncp) - (ncp - 1))[None, :]
    dist_c = -CMP_STRIDE * x + sub - (CMP_LEN - 1)
    bias_c = _toeplitz(_bias_from_dist(rel_bias, dist_c, dist_c >= 0), ncp, ncp)
    bias_c = jnp.where(jnp.asarray(np.arange(ncp) < n_cmp), bias_c, NEG)
    bias_c = jnp.transpose(bias_c, (0, 3, 2, 1)).reshape(N_HEADS, ncp, nqt, tq)
    bias_ct = _heads_to_lanes(jnp.transpose(bias_c, (0, 2, 1, 3)), (nqt,))
    wu = seq - tq + tc
    strip = _toeplitz_bias(rel_bias, tq, wu, lambda x: (seq - tq) - x, lambda x: (seq - tq) - x >= 0)
    strip_t = _heads_to_lanes(jnp.swapaxes(strip, 1, 2), ())
    ww = 2 * SLIDE_WIN + tq
    strip_w = _toeplitz_bias(rel_bias, tq, ww, lambda x: SLIDE_WIN - x,
                             lambda x: (SLIDE_WIN - x >= 0) & (SLIDE_WIN - x < SLIDE_WIN))
    strip_wt = _heads_to_lanes(jnp.swapaxes(strip_w, 1, 2), ())

    a_m = jnp.asarray(_imp_to_block_matrix(ncp, n_cmp, nb), BF16)
    e_bt = jnp.asarray(np.kron(np.eye(nb), np.ones((SEL_BLOCK, 1))), BF16)

    half = lambda g: g // 2
    kv_spec = lambda off: pl.BlockSpec((1, seq, LANES), lambda g, b, qi: (b, 0, off + half(g)))
    full = lambda a: pl.BlockSpec(a.shape, lambda g, b, qi: (0,) * a.ndim)
    out = pl.pallas_call(
        functools.partial(_nsa_kernel, seq=seq, tq=tq, tc=tc),
        grid=(N_GROUPS, bsz, nqt),
        in_specs=[pl.BlockSpec((1, tq, GROUP_HEADS * HEAD_DIM), lambda g, b, qi: (b, qi, g)),
                  pl.BlockSpec((1, tq, LANES), lambda g, b, qi: (b, qi, g)),
                  pl.BlockSpec((1, ncp, LANES), lambda g, b, qi: (b, 0, half(g))),
                  pl.BlockSpec((1, LANES, ncp), lambda g, b, qi: (b, half(g), 0)),
                  kv_spec(4), kv_spec(6), kv_spec(0), kv_spec(2),
                  pl.BlockSpec((1, 1, ncp, nq), lambda g, b, qi: (g, qi, 0, 0)),
                  pl.BlockSpec((1, wu, nq), lambda g, b, qi: (g, 0, 0)),
                  pl.BlockSpec((1, ww, nq), lambda g, b, qi: (g, 0, 0)),
                  full(a_m), full(e_bt)],
        out_specs=pl.BlockSpec((1, tq, GROUP_HEADS * HEAD_DIM), lambda g, b, qi: (b, qi, g)),
        out_shape=jax.ShapeDtypeStruct((bsz, seq, d), F32),
        scratch_shapes=[pltpu.VMEM((seq, LANES), BF16), pltpu.VMEM((LANES, seq), BF16),
                        pltpu.VMEM((seq, LANES), BF16), pltpu.VMEM((LANES, seq), BF16),
                        pltpu.VMEM((seq, tq), F32),
                        pltpu.VMEM((1, nq), F32), pltpu.VMEM((1, nq), F32), pltpu.VMEM((LANES, nq), F32)],
        compiler_params=_cparams("parallel", "parallel", "arbitrary"),
        name="nsa_prompt_attn",
    )(qv, gv, kc, vct, kvpv, kvpv, kvwv, kvwv, bias_ct, strip_t, strip_wt, a_m, e_bt)
    return out.reshape(bsz * seq, d)


CMP_PAGES = 16


def _scmp_kernel(pt_ref, *refs):
    del pt_ref
    pages = refs[:CMP_PAGES]
    w1_ref, pe_ref, o_ref, t_scr, x_scr = refs[CMP_PAGES:]
    cpp = pages[0].shape[3] // CMP_STRIDE
    nh = N_GROUPS * CMP_HIDDEN
    for kind in range(2):
        for i, p_ref in enumerate(pages):
            x = p_ref[0, kind].T
            for hh in range(x.shape[1] // LANES):
                t_scr[i, hh] = x[:, hh * LANES:(hh + 1) * LANES]
            for s in range(CMP_STRIDE):
                for hh in range(x.shape[1] // LANES):
                    x_scr[s, i * cpp:(i + 1) * cpp, hh * LANES:(hh + 1) * LANES] = (
                        t_scr[i, hh, pl.ds(s, cpp, stride=CMP_STRIDE), :])
        for part in range(CMP_LEN // CMP_STRIDE):
            acc = jnp.zeros((x_scr.shape[1], nh), F32)
            for s in range(CMP_STRIDE):
                x = (x_scr[s] + pe_ref[kind, part, s:s + 1, :]).astype(BF16)
                acc = acc + _dot(x, w1_ref[kind, part, s])
            o_ref[kind, 0, :, part * nh:(part + 1) * nh] = acc


def _scmp(cache, page_table, w1bd, pe):
    bsz, n_pages = page_table.shape
    _, _, gw, page = cache.shape
    cpp = page // CMP_STRIDE
    nh = N_GROUPS * CMP_HIDDEN
    n_chunks = n_pages * cpp
    page_spec = lambda i: pl.BlockSpec((1, 2, gw, page), lambda b, j, pt: (pt[b, j * CMP_PAGES + i], 0, 0, 0))
    const = lambda a: pl.BlockSpec(a.shape, lambda b, j, pt: (0,) * a.ndim, pipeline_mode=pl.Buffered(1))
    grid_spec = pltpu.PrefetchScalarGridSpec(
        num_scalar_prefetch=1,
        grid=(bsz, n_pages // CMP_PAGES),
        in_specs=[page_spec(i) for i in range(CMP_PAGES)] + [const(w1bd), const(pe)],
        out_specs=pl.BlockSpec((2, 1, CMP_PAGES * cpp, 2 * nh), lambda b, j, pt: (0, b, j, 0)),
        scratch_shapes=[pltpu.VMEM((CMP_PAGES, gw // LANES, page, LANES), F32),
                        pltpu.VMEM((CMP_STRIDE, CMP_PAGES * cpp, gw), F32)])
    return pl.pallas_call(
        _scmp_kernel,
        grid_spec=grid_spec,
        out_shape=jax.ShapeDtypeStruct((2, bsz, n_chunks, 2 * nh), F32),
        compiler_params=_cparams("parallel", "arbitrary"),
        name="compress_sample",
    )(page_table, *([cache] * CMP_PAGES), w1bd, pe)


def _s1_kernel(ab_ref, w2_ref, q_ref, bc_ref, at_ref, gs_ref, oc_ref, it_ref):
    nck = ab_ref.shape[2]
    nh = N_GROUPS * CMP_HIDDEN
    kv = []
    for kind in range(2):
        h = ab_ref[kind, 0, :, :nh] + pltpu.roll(ab_ref[kind, 0, :, nh:], nck - 1, 0)
        kv.append(_dot(jnp.maximum(h, 0.0).astype(BF16), w2_ref[kind]).astype(BF16))
    diag = _head_diag(N_HEADS, D_MODEL)
    qe = jnp.where(diag, q_ref[0] * ATTN_SCALE, 0.0).astype(BF16)
    bc = bc_ref[...]
    pc = _masked_softmax(_dot_nt(qe, kv[0]) + bc, bc > 0.5 * NEG)
    o = _dot(pc.astype(BF16), kv[1])
    oc_ref[0] = jnp.sum(jnp.where(diag, o, 0.0), axis=0, keepdims=True)
    hi = pc.astype(BF16)
    lo = (pc - hi.astype(F32)).astype(BF16)
    imp = _dot(gs_ref[...], hi) + _dot(gs_ref[...], lo)
    it_ref[0] = _split_dot_nt(at_ref[...], imp)


def _s1(ab, w2ex, q, rel_bias, past):
    _, bsz, nck, _ = ab.shape
    d = D_MODEL
    n_cmp = (past + 1 - CMP_LEN) // CMP_STRIDE + 1
    n_blk = -(-(past + 1) // SEL_BLOCK)
    nbp = -(-n_blk // 8) * 8
    n = np.arange(nck)
    dist_c = past - (n * CMP_STRIDE + CMP_LEN - 1)
    bias_c = _bias_from_dist(rel_bias, dist_c, (dist_c >= 0) & (n < n_cmp))
    a_t = jnp.asarray(_imp_to_block_matrix(nck, n_cmp, nbp)[:nbp] * (np.arange(nbp)[:, None] < n_blk), BF16)
    gsum = jnp.asarray(np.kron(np.eye(8, N_GROUPS), np.ones((1, GROUP_HEADS))), BF16)
    full = lambda a: pl.BlockSpec(a.shape, lambda b: (0,) * a.ndim)
    oc, imp_t = pl.pallas_call(
        _s1_kernel,
        grid=(bsz,),
        in_specs=[pl.BlockSpec((2, 1, nck, ab.shape[3]), lambda b: (0, b, 0, 0)), full(w2ex),
                  pl.BlockSpec((1, 1, d), lambda b: (b, 0, 0)), full(bias_c), full(a_t), full(gsum)],
        out_specs=[pl.BlockSpec((1, 1, d), lambda b: (b, 0, 0)), pl.BlockSpec((1, nbp, 8), lambda b: (b, 0, 0))],
        out_shape=[jax.ShapeDtypeStruct((bsz, 1, d), F32), jax.ShapeDtypeStruct((bsz, nbp, 8), F32)],
        compiler_params=_cparams("parallel"),
        name="sample_compressed_attn",
    )(ab, w2ex, q.reshape(bsz, 1, d), bias_c, a_t, gsum)
    return oc.reshape(bsz, d), imp_t


def _s2_kernel(imp_ref, idx_ref, s_scr, *, n_blk, cur):
    shape = imp_ref.shape
    j = lax.broadcasted_iota(jnp.int32, shape, 0)
    forced = jnp.logical_or(j == 0, jnp.logical_or(j == cur, j == cur - 1))
    score = jnp.where(forced, SEL_FORCE, jnp.where(j < n_blk, imp_ref[...], -2.0))
    s_scr[...] = score

    def body(i, rank):
        si = s_scr[pl.ds(i, 1), :]
        beats = jnp.logical_or(si > score, jnp.logical_and(si == score, j > i))
        return rank + jnp.where(beats, 1.0, 0.0)

    rank = lax.fori_loop(0, n_blk, body, jnp.zeros(shape, F32))
    for k in range(SEL_TOPK):
        idx_ref[k:k + 1, :] = jnp.sum(jnp.where(rank == k, j, 0), axis=0, keepdims=True)


def _s2(imp_t, past):
    n_blk = -(-(past + 1) // SEL_BLOCK)
    return pl.pallas_call(
        functools.partial(_s2_kernel, n_blk=n_blk, cur=past // SEL_BLOCK),
        out_shape=jax.ShapeDtypeStruct((SEL_TOPK, imp_t.shape[1]), jnp.int32),
        scratch_shapes=[pltpu.VMEM(imp_t.shape, F32)],
        name="sample_block_topk",
    )(imp_t)


def _bucket_thresholds(max_dist):
    b = _bucket_np(np.arange(max_dist + 1))
    return [int(np.argmax(b >= k)) if (b >= k).any() else max_dist + 1 for k in range(REL_BUCKETS)]


def _attend_t(q, kt, vt, bias, valid, kn, vn, bias_new):
    s = _dot(q.astype(BF16), kt.astype(BF16)) + bias
    if valid is not None:
        s = jnp.where(valid, s, NEG)
    sn = jnp.sum(q * kn, axis=-1, keepdims=True) + bias_new
    m = jnp.maximum(jnp.max(s, axis=-1, keepdims=True), sn)
    e = jnp.exp(s - m)
    en = jnp.exp(sn - m)
    l = jnp.sum(e, axis=-1, keepdims=True) + en
    return (_dot_nt(e.astype(BF16), vt.astype(BF16)) + en * vn) / l


def _ssel_kernel(idx_ref, page_ref, *refs, past, thresholds):
    del page_ref
    blocks = refs[:SEL_TOPK]
    q_ref, n_ref, tb_ref, o_ref = refs[SEL_TOPK:]
    b = pl.program_id(0)
    g = pl.program_id(1)
    page = blocks[0].shape[3]
    kt = jnp.concatenate([blk[0, 0] for blk in blocks], axis=1)
    vt = jnp.concatenate([blk[0, 1] for blk in blocks], axis=1)
    nk = SEL_TOPK * page
    lane = lax.broadcasted_iota(jnp.int32, (1, nk), 1)
    row = lane % page
    pos = jnp.zeros((1, nk), jnp.int32)
    inside = jnp.zeros((1, nk), jnp.int32)
    bpp = page // SEL_BLOCK
    for k in range(SEL_TOPK):
        blk_idx = idx_ref[(b * N_GROUPS + g) * SEL_TOPK + k]
        mine = lane // page == k
        pos = jnp.where(mine, (blk_idx // bpp) * page + row, pos)
        inside = jnp.where(mine, jnp.where(row // SEL_BLOCK == blk_idx % bpp, 1, 0), inside)
    valid = jnp.logical_and(inside == 1, pos < past)
    dist = past - pos
    bias = jnp.broadcast_to(tb_ref[0, 0][:, :1], (8, nk))
    for k in range(1, REL_BUCKETS):
        bias = jnp.where(dist >= thresholds[k], tb_ref[0, k][:, :1], bias)
    q = q_ref[0, 0] * ATTN_SCALE
    o_ref[0, 0] = _attend_t(q, kt, vt, bias, valid, n_ref[0, 0, 0], n_ref[0, 1, 0], tb_ref[0, 0][:, :1])


def _ssel(sel_idx, cache_t, page_table, q, kvp_new, rel_bias, past):
    bsz = q.shape[0]
    page = cache_t.shape[3]
    bpp = page // SEL_BLOCK
    cur = past // SEL_BLOCK
    cached = jnp.minimum(sel_idx, cur - 1)
    pages = jnp.take_along_axis(page_table, (cached // bpp).reshape(bsz, -1), axis=1).reshape(-1)
    q4 = jnp.pad(q.reshape(bsz, N_GROUPS, GROUP_HEADS, HEAD_DIM), ((0, 0), (0, 0), (0, 8 - GROUP_HEADS), (0, 0)))
    new = kvp_new.reshape(bsz, 4, N_GROUPS, 1, HEAD_DIM)
    tb = jnp.transpose(rel_bias.astype(F32).reshape(REL_BUCKETS, N_GROUPS, GROUP_HEADS), (1, 0, 2))
    tb = jnp.pad(tb, ((0, 0), (0, 0), (0, 8 - GROUP_HEADS)))
    tb = jnp.broadcast_to(tb[..., None], tb.shape + (LANES,))
    blk_spec = lambda k: pl.BlockSpec(
        (1, 2, HEAD_DIM, page), lambda b, g, idx, pg: (pg[(b * N_GROUPS + g) * SEL_TOPK + k], 1, g, 0))
    grid_spec = pltpu.PrefetchScalarGridSpec(
        num_scalar_prefetch=2,
        grid=(bsz, N_GROUPS),
        in_specs=[blk_spec(k) for k in range(SEL_TOPK)] + [
            pl.BlockSpec((1, 1, 8, HEAD_DIM), lambda b, g, idx, pg: (b, g, 0, 0)),
            pl.BlockSpec((1, 2, 1, 1, HEAD_DIM), lambda b, g, idx, pg: (b, 1, g, 0, 0)),
            pl.BlockSpec((1, REL_BUCKETS, 8, LANES), lambda b, g, idx, pg: (g, 0, 0, 0))],
        out_specs=pl.BlockSpec((1, 1, 8, HEAD_DIM), lambda b, g, idx, pg: (b, g, 0, 0)))
    out = pl.pallas_call(
        functools.partial(_ssel_kernel, past=past, thresholds=tuple(_bucket_thresholds(past))),
        grid_spec=grid_spec,
        out_shape=jax.ShapeDtypeStruct((bsz, N_GROUPS, 8, HEAD_DIM), F32),
        compiler_params=_cparams("parallel", "parallel"),
        name="sample_selected_attn",
    )(sel_idx.reshape(-1), pages, *([cache_t] * SEL_TOPK), q4, new, tb)
    return out[:, :, :GROUP_HEADS].reshape(bsz, N_HEADS, HEAD_DIM)


def _swin_kernel(c_ref, q_ref, n_ref, m_ref, bw_ref, b0_ref, o_ref, u_ref):
    kt = c_ref[0, 0]
    vt = c_ref[0, 1]
    q = q_ref[0] * ATTN_SCALE
    o = _attend_t(q, kt, vt, bw_ref[...], None, n_ref[0, 0:1, :], n_ref[0, 1:2, :], b0_ref[...][:, :1])
    grp = lax.broadcasted_iota(jnp.int32, (N_HEADS, HEAD_DIM), 0) // GROUP_HEADS
    out = jnp.zeros((N_HEADS, HEAD_DIM), F32)
    for g in range(N_GROUPS):
        out = jnp.where(grp == g, o[:, g * HEAD_DIM:(g + 1) * HEAD_DIM], out)
    o_ref[0] = out
    u_ref[0, 0] = _shift_in(kt, m_ref[0, 0])
    u_ref[0, 1] = _shift_in(vt, m_ref[0, 1])


def _swin(cwin_t, q, kvw_new, rel_bias):
    bsz = q.shape[0]
    _, _, gw, wlen = cwin_t.shape
    onehot = jnp.asarray(np.kron(np.eye(N_GROUPS), np.ones((GROUP_HEADS, 1))), F32)
    qbd = jnp.einsum("bhd,hg->bhgd", q.reshape(bsz, N_HEADS, HEAD_DIM), onehot).reshape(bsz, N_HEADS, gw)
    dist_w = wlen - np.arange(wlen)
    bw = _bias_from_dist(rel_bias, dist_w, dist_w < SLIDE_WIN)
    b0 = _bias_from_dist(rel_bias, np.zeros((LANES,), np.int64), np.ones((LANES,), bool))
    full = lambda a: pl.BlockSpec(a.shape, lambda b: (0,) * a.ndim)
    c_spec = pl.BlockSpec((1, 2, gw, wlen), lambda b: (b, 0, 0, 0))
    o_spec = pl.BlockSpec((1, N_HEADS, HEAD_DIM), lambda b: (b, 0, 0))
    return pl.pallas_call(
        _swin_kernel,
        grid=(bsz,),
        in_specs=[c_spec, pl.BlockSpec((1, N_HEADS, gw), lambda b: (b, 0, 0)),
                  pl.BlockSpec((1, 2, gw), lambda b: (b, 0, 0)),
                  pl.BlockSpec((1, 2, gw, 1), lambda b: (b, 0, 0, 0)), full(bw), full(b0)],
        out_specs=[o_spec, c_spec],
        out_shape=[jax.ShapeDtypeStruct((bsz, N_HEADS, HEAD_DIM), F32), jax.ShapeDtypeStruct(cwin_t.shape, F32)],
        compiler_params=_cparams("parallel"),
        name="sample_window_attn",
    )(cwin_t, qbd, kvw_new.reshape(bsz, 2, gw), kvw_new.reshape(bsz, 2, gw, 1), bw, b0)


def _gate_sum_kernel(g_ref, oc_ref, os_ref, ow_ref, o_ref):
    g = g_ref[...]
    o_ref[...] = g[:, :, 0:1] * oc_ref[...] + g[:, :, 1:2] * os_ref[...] + g[:, :, 2:3] * ow_ref[...]


def _gate_sum(gate, oc, os_, ow):
    return pl.pallas_call(
        _gate_sum_kernel,
        out_shape=jax.ShapeDtypeStruct(oc.shape, F32),
        name="sample_gate_sum",
    )(gate, oc, os_, ow)


def _split_a_weights(w_in):
    d = D_MODEL
    w = w_in.astype(BF16)
    qs = [w[:, g * 3 * d:g * 3 * d + d] for g in range(len(DIL_PATTERNS))]
    kvs = [w[:, g * 3 * d + d:(g + 1) * 3 * d] for g in range(len(DIL_PATTERNS))]
    return qs + kvs


def _split_b_weights(w_in):
    d = D_MODEL
    gw = N_GROUPS * HEAD_DIM
    w = w_in.astype(BF16)
    wg = w[:, d + 6 * gw:].reshape(d, N_GROUPS, GROUP_HEADS * 3)
    wg = jnp.pad(wg, ((0, 0), (0, 0), (0, LANES - GROUP_HEADS * 3))).reshape(d, N_GROUPS * LANES)
    return [w[:, :d], w[:, d:d + 4 * gw], w[:, d + 4 * gw:d + 6 * gw], wg]


def _unblock_gate(gate):
    m = gate.shape[0]
    return gate.reshape(m, N_GROUPS, LANES)[:, :, :GROUP_HEADS * 3].reshape(m, N_HEADS * 3)


def kernel(x_prompt, x_sample, cache_a_win0, cache_a_win1, cache_a_win2, cache_b_kv, cache_b_win, page_table,
           rel_bias, norm_mix, norm_mlp, norm_final, a_w_in, a_w_out, b_w_in, b_w_out, b_phi_pe, b_phi_w1,
           b_phi_w2, mlp_w_up, mlp_w_down):
    bsz, seq, d = x_prompt.shape
    sb = x_sample.shape[0]
    past = page_table.shape[1] * cache_b_kv.shape[2]
    tm_p, tm_s = 512, sb
    tm_w = 1024
    xp = x_prompt.reshape(bsz * seq, d)
    xs = x_sample.reshape(sb, d)
    acts6 = ("none",) * 6

    wa = _split_a_weights(a_w_in[0])
    wa_out = a_w_out[0].astype(BF16)
    pa = _norm_matmul(xp, norm_mix[0], wa, acts6, tm_w, 4)
    sa = _norm_matmul(xs, norm_mix[0], wa, acts6, tm_s, 4)
    os_, ls = [], []
    for gi, (_, dil) in enumerate(DIL_PATTERNS):
        o, lse = _swa_group(pa[gi], pa[3 + gi], rel_bias, bsz, seq, dil)
        os_.append(o)
        ls.append(lse)
    xp = _merge_proj(os_, ls, xp, wa_out, tm_p)
    a_caches = [_rows_last(c[0]) for c in (cache_a_win0, cache_a_win1, cache_a_win2)]
    oa, a_new_t = _samp_a(sa[:3], sa[3:], a_caches, rel_bias)
    xs = _matmul_res(oa, xs, wa_out, tm_s)
    a_new_p = [pa[3 + gi].reshape(bsz, seq, 2 * d)[:, seq - min(w, seq):] for gi, (w, _) in enumerate(DIL_PATTERNS)]
    a_new_s = [_rows_second(c, (2, N_HEADS, HEAD_DIM)) for c in a_new_t]

    wu0, wd0 = mlp_w_up[0].astype(BF16), mlp_w_down[0].astype(BF16)
    xp = _mlp(xp, norm_mlp[0], wu0, wd0, norm_final, False, tm_w, 1024)
    xs = _mlp(xs, norm_mlp[0], wu0, wd0, norm_final, False, tm_s, 1024)

    wb = _split_b_weights(b_w_in[0])
    wb_out = b_w_out[0].astype(BF16)
    acts_b = ("none", "none", "none", "sigmoid")
    qp, kvp_p, kvw_p, gate_p = _norm_matmul(xp, norm_mix[1], wb, acts_b, tm_p, 1)
    qs_, kvp_s, kvw_s, gate_s = _norm_matmul(xs, norm_mix[1], wb, acts_b, tm_s, 1)
    w1bd, w2bd, w2ex, pe = _cmp_weights(b_phi_pe[0], b_phi_w1[0], b_phi_w2[0])
    kc = _cmp_prompt(kvp_p, bsz, seq, 0, w1bd, pe, w2bd, False)
    vct = _cmp_prompt(kvp_p, bsz, seq, 1, w1bd, pe, w2bd, True)
    ob = _nsa_prompt(qp, gate_p, kc, vct, kvp_p, kvw_p, rel_bias, bsz, seq)
    xp = _matmul_res(ob, xp, wb_out, tm_p)

    cache_t = _rows_last(cache_b_kv[0])
    cwin_t = _rows_last(cache_b_win[0])
    ab = _scmp(cache_t, page_table, w1bd, pe)
    oc, imp_t = _s1(ab, w2ex, qs_, rel_bias, past)
    imp_t = jnp.transpose(imp_t[:, :, :N_GROUPS], (1, 0, 2)).reshape(imp_t.shape[1], sb * N_GROUPS)
    sel_idx = jnp.transpose(_s2(imp_t, past)).reshape(sb, N_GROUPS, SEL_TOPK)
    osel = _ssel(sel_idx, cache_t, page_table, qs_, kvp_s, rel_bias, past)
    owin, b_win_t = _swin(cwin_t, qs_, kvw_s, rel_bias)
    obs = _gate_sum(_unblock_gate(gate_s).reshape(sb, N_HEADS, 3), oc.reshape(sb, N_HEADS, HEAD_DIM), osel, owin)
    xs = _matmul_res(obs.reshape(sb, d), xs, wb_out, tm_s)
    b_win_s = _rows_second(b_win_t, (2, N_GROUPS, HEAD_DIM))

    wu1, wd1 = mlp_w_up[1].astype(BF16), mlp_w_down[1].astype(BF16)
    yp = _mlp(xp, norm_mlp[1], wu1, wd1, norm_final, True, tm_w, 1024)
    ys = _mlp(xs, norm_mlp[1], wu1, wd1, norm_final, True, tm_s, 1024)

    gw = N_GROUPS * HEAD_DIM
    keep = min(SLIDE_WIN, seq)
    cache_shape = lambda n: (1, -1, n, 2, N_HEADS, HEAD_DIM)
    outs = [yp.reshape(bsz, seq, d), ys.reshape(sb, 1, d)]
    for p, s in zip(a_new_p, a_new_s):
        outs.append(p.reshape(cache_shape(p.shape[1])))
        outs.append(s[None])
    outs.append(kvp_p.reshape(1, bsz, seq, 4, N_GROUPS, HEAD_DIM))
    outs.append(kvp_s.reshape(1, sb, 1, 4, N_GROUPS, HEAD_DIM))
    outs.append(kvw_p.reshape(bsz, seq, 2 * gw)[:, seq - keep:].reshape(1, bsz, keep, 2, N_GROUPS, HEAD_DIM))
    outs.append(b_win_s[None])
    return tuple(outs)
```

```python
import functools
import math

import numpy as np
import jax
import jax.numpy as jnp
from jax import lax
from jax.experimental import pallas as pl
from jax.experimental.pallas import tpu as pltpu

F32 = jnp.float32
BF16 = jnp.bfloat16

D_MODEL = 1024
HEAD_DIM = 64
N_HEADS = 16
ATTN_SCALE = HEAD_DIM ** -0.5
LOG2E = math.log2(math.e)
RMS_EPS = 1e-6
REL_BUCKETS = 32
REL_MAX_DIST = 2048
DIL_PATTERNS = ((128, 1), (512, 4), (2048, 16))
A_KEYS = 128
N_GROUPS = 4
GROUP_HEADS = N_HEADS // N_GROUPS
CMP_LEN = 32
CMP_STRIDE = 16
CMP_HIDDEN = 128
SEL_BLOCK = 64
SEL_TOPK = 16
SEL_FORCE = 1e4
SLIDE_WIN = 512

LANES = 128
VMEM_LIMIT = 56 * 1024 * 1024
NEG = -1e30

NT_DIMS = (((1,), (1,)), ((), ()))


def _cparams(*sem):
    return pltpu.CompilerParams(dimension_semantics=sem, vmem_limit_bytes=VMEM_LIMIT)


def _dot(a, b):
    return jnp.dot(a, b, preferred_element_type=F32)


def _dot_nt(a, b):
    return lax.dot_general(a, b, NT_DIMS, preferred_element_type=F32)


def _split_dot(x, w):
    hi = x.astype(BF16)
    lo = (x - hi.astype(F32)).astype(BF16)
    return _dot(hi, w) + _dot(lo, w)


def _split_dot_nt(w, x):
    hi = x.astype(BF16)
    lo = (x - hi.astype(F32)).astype(BF16)
    return _dot_nt(w, hi) + _dot_nt(w, lo)


def _rms(x, g):
    ms = jnp.mean(x * x, axis=-1, keepdims=True)
    return x * lax.rsqrt(ms + RMS_EPS) * g


def _bucket_np(dist):
    dist = np.maximum(np.asarray(dist, np.int64), 0)
    max_exact = REL_BUCKETS // 2
    ratio = (np.log(np.maximum(dist, 1).astype(np.float32) / np.float32(max_exact))
             / np.float32(math.log(REL_MAX_DIST / max_exact)))
    large = np.minimum(max_exact + (ratio * (REL_BUCKETS - max_exact)).astype(np.int32), REL_BUCKETS - 1)
    return np.where(dist < max_exact, dist, large).astype(np.int32)


def _bias_from_dist(rel_bias, dist, valid):
    idx = jnp.asarray(_bucket_np(dist))
    b = jnp.take(rel_bias.astype(F32).T, idx, axis=1)
    return jnp.where(jnp.asarray(valid)[None], b, NEG)


def _toeplitz(w, n, m):
    length = n + m
    lead = w.shape[:-1]
    t = jnp.tile(w, (1,) * len(lead) + (n,))[..., :n * (length - 1)].reshape(lead + (n, length - 1))
    return t[..., n - 1:n - 1 + m]


def _toeplitz_bias(rel_bias, n, m, dist_of, valid_of):
    x = np.arange(n + m) - (n - 1)
    return _toeplitz(_bias_from_dist(rel_bias, dist_of(x), valid_of(x)), n, m)


def _norm_matmul_kernel(x_ref, g_ref, *refs, n_w, n_t, acts):
    w_refs, wt_refs = refs[:n_w], refs[n_w:n_w + n_t]
    o_refs, ot_refs = refs[n_w + n_t:2 * n_w + n_t], refs[2 * n_w + n_t:2 * (n_w + n_t)]
    xn_ref = refs[2 * (n_w + n_t)]

    @pl.when(pl.program_id(1) == 0)
    def _():
        xn_ref[...] = _rms(x_ref[...], g_ref[...]).astype(BF16)

    xn = xn_ref[...]
    for w_ref, o_ref, act in zip(w_refs, o_refs, acts):
        y = _dot(xn, w_ref[...])
        if act == "sigmoid":
            y = jax.nn.sigmoid(y)
        o_ref[...] = y
    for wt_ref, ot_ref in zip(wt_refs, ot_refs):
        ot_ref[0] = _dot_nt(wt_ref[...], xn)


def _norm_matmul(x, g, ws, acts, tm, nj, wts=(), seq=None):
    m, d = x.shape
    kern = functools.partial(_norm_matmul_kernel, n_w=len(ws), n_t=len(wts), acts=tuple(acts))
    in_specs = [pl.BlockSpec((tm, d), lambda i, j: (i, 0)), pl.BlockSpec((1, d), lambda i, j: (0, 0))]
    in_specs += [pl.BlockSpec((d, w.shape[1] // nj), lambda i, j: (0, j)) for w in ws]
    in_specs += [pl.BlockSpec((w.shape[0] // nj, d), lambda i, j: (j, 0)) for w in wts]
    out_specs = [pl.BlockSpec((tm, w.shape[1] // nj), lambda i, j: (i, j)) for w in ws]
    out_shape = [jax.ShapeDtypeStruct((m, w.shape[1]), F32) for w in ws]
    if wts:
        tps = seq // tm
        out_specs += [pl.BlockSpec((1, w.shape[0] // nj, tm), lambda i, j: (i // tps, j, i % tps)) for w in wts]
        out_shape += [jax.ShapeDtypeStruct((m // seq, w.shape[0], seq), F32) for w in wts]
    return pl.pallas_call(
        kern,
        grid=(m // tm, nj),
        in_specs=in_specs,
        out_specs=out_specs,
        out_shape=out_shape,
        scratch_shapes=[pltpu.VMEM((tm, d), BF16)],
        compiler_params=_cparams("parallel", "arbitrary"),
        name="norm_matmul",
    )(x, g.reshape(1, d), *ws, *wts)


def _matmul_res_kernel(a_ref, x_ref, w_ref, o_ref):
    o_ref[...] = x_ref[...] + _dot(a_ref[...].astype(BF16), w_ref[...])


def _matmul_res(a, x, w, tm):
    m, d = x.shape
    return pl.pallas_call(
        _matmul_res_kernel,
        grid=(m // tm,),
        in_specs=[pl.BlockSpec((tm, a.shape[1]), lambda i: (i, 0)),
                  pl.BlockSpec((tm, d), lambda i: (i, 0)),
                  pl.BlockSpec(w.shape, lambda i: (0, 0))],
        out_specs=pl.BlockSpec((tm, d), lambda i: (i, 0)),
        out_shape=jax.ShapeDtypeStruct((m, d), F32),
        compiler_params=_cparams("parallel"),
        name="matmul_res",
    )(a, x, w)


def _merge_proj_kernel(o0, o1, o2, l0, l1, l2, e_ref, x_ref, w_ref, out_ref):
    ls = [l0[...], l1[...], l2[...]]
    mx = jnp.maximum(jnp.maximum(ls[0], ls[1]), ls[2])
    es = [jnp.exp(l - mx) for l in ls]
    tot = es[0] + es[1] + es[2]
    merged = None
    for e, o in zip(es, (o0, o1, o2)):
        part = _split_dot(e / tot, e_ref[...]) * o[...]
        merged = part if merged is None else merged + part
    out_ref[...] = x_ref[...] + _dot(merged.astype(BF16), w_ref[...])


def _merge_proj(os_, ls, x, w, tm):
    m, d = x.shape
    expand = jnp.asarray(np.kron(np.eye(LANES, N_HEADS), np.ones((1, HEAD_DIM))), BF16)
    row = lambda n: pl.BlockSpec((tm, n), lambda i: (i, 0))
    return pl.pallas_call(
        _merge_proj_kernel,
        grid=(m // tm,),
        in_specs=[row(d)] * 3 + [row(LANES)] * 3 + [
            pl.BlockSpec(expand.shape, lambda i: (0, 0)), row(d), pl.BlockSpec(w.shape, lambda i: (0, 0))],
        out_specs=row(d),
        out_shape=jax.ShapeDtypeStruct((m, d), F32),
        compiler_params=_cparams("parallel"),
        name="merge_proj",
    )(*os_, *ls, expand, x, w)


def _mlp_kernel(x_ref, g_ref, wu_ref, wd_ref, gf_ref, o_ref, xn_ref, acc_ref, *, final_norm):
    f = pl.program_id(1)

    @pl.when(f == 0)
    def _():
        xn_ref[...] = _rms(x_ref[...], g_ref[...]).astype(BF16)
        acc_ref[...] = jnp.zeros_like(acc_ref)

    h = jnp.square(jnp.maximum(_dot(xn_ref[...], wu_ref[...]), 0.0))
    acc_ref[...] += _dot(h.astype(BF16), wd_ref[...])

    @pl.when(f == pl.num_programs(1) - 1)
    def _():
        y = x_ref[...] + acc_ref[...]
        if final_norm:
            y = _rms(y, gf_ref[...])
        o_ref[...] = y


def _mlp(x, g, wu, wd, gf, final_norm, tm, tf):
    m, d = x.shape
    dff = wu.shape[1]
    return pl.pallas_call(
        functools.partial(_mlp_kernel, final_norm=final_norm),
        grid=(m // tm, dff // tf),
        in_specs=[pl.BlockSpec((tm, d), lambda i, f: (i, 0)),
                  pl.BlockSpec((1, d), lambda i, f: (0, 0)),
                  pl.BlockSpec((d, tf), lambda i, f: (0, f)),
                  pl.BlockSpec((tf, d), lambda i, f: (f, 0)),
                  pl.BlockSpec((1, d), lambda i, f: (0, 0))],
        out_specs=pl.BlockSpec((tm, d), lambda i, f: (i, 0)),
        out_shape=jax.ShapeDtypeStruct((m, d), F32),
        scratch_shapes=[pltpu.VMEM((tm, d), BF16), pltpu.VMEM((tm, d), F32)],
        compiler_params=_cparams("parallel", "arbitrary"),
        name="mlp",
    )(x, g.reshape(1, d), wu, wd, gf.reshape(1, d))


def _swa_kernel(q_ref, k_ref, v_ref, b_ref, o_ref, lse_ref, *, dil):
    t = A_KEYS
    hp = pl.program_id(1)
    tiles = q_ref.shape[1] // (dil * t)
    lane = lax.broadcasted_iota(jnp.int32, (t, LANES), 1)
    low = lane < HEAD_DIM

    @pl.when(hp == 0)
    def _():
        lse_ref[...] = jnp.zeros_like(lse_ref)

    def tile(r, qi):
        rows = pl.ds(r + dil * t * qi, t, stride=dil)
        prev = pl.ds(r + dil * t * max(qi - 1, 0), t, stride=dil)
        first = min(qi, 1)
        q2 = (q_ref[0, rows, :] * ATTN_SCALE).astype(BF16)
        k2 = jnp.concatenate([k_ref[0, prev, :], k_ref[0, rows, :]], axis=0).astype(BF16)
        v2 = jnp.concatenate([v_ref[0, prev, :], v_ref[0, rows, :]], axis=0).astype(BF16)
        lse_t = lse_ref[0, rows, :]
        outs = []
        for a in range(2):
            h = 2 * hp + a
            qm = jnp.where(low if a == 0 else jnp.logical_not(low), q2, jnp.zeros_like(q2))
            s = _dot_nt(qm, k2) + b_ref[first, h]
            m = jnp.max(s, axis=-1, keepdims=True)
            e = jnp.exp(s - m)
            l = jnp.sum(e, axis=-1, keepdims=True)
            outs.append(_dot(e.astype(BF16), v2) / l)
            lse_t = jnp.where(lane == h, m + jnp.log(l), lse_t)
        o_ref[0, rows, :] = jnp.where(low, outs[0], outs[1])
        lse_ref[0, rows, :] = lse_t

    for r in range(dil):
        for qi in range(tiles):
            tile(r, qi)


def _swa_bias(rel_bias, dil, t):
    later = _toeplitz_bias(rel_bias, t, 2 * t, lambda x: dil * (t - x), lambda x: (t - x >= 0) & (t - x <= t))
    first = jnp.where(jnp.asarray(np.arange(2 * t) >= t)[None, None, :], later, NEG)
    return jnp.stack([first, later])


def _swa_group(q, kv, rel_bias, bsz, seq, dil):
    d = D_MODEL
    nhp = N_HEADS // 2
    assert seq % (dil * A_KEYS) == 0
    bias = _swa_bias(rel_bias, dil, A_KEYS)
    pair = lambda off: pl.BlockSpec((1, seq, LANES), lambda b, hp: (b, 0, off + hp))
    o, lse = pl.pallas_call(
        functools.partial(_swa_kernel, dil=dil),
        grid=(bsz, nhp),
        in_specs=[pair(0), pair(0), pair(nhp), pl.BlockSpec(bias.shape, lambda b, hp: (0, 0, 0, 0))],
        out_specs=[pair(0), pl.BlockSpec((1, seq, LANES), lambda b, hp: (b, 0, 0))],
        out_shape=[jax.ShapeDtypeStruct((bsz, seq, d), F32), jax.ShapeDtypeStruct((bsz, seq, LANES), F32)],
        compiler_params=_cparams("parallel", "arbitrary"),
        name="swa_attn",
    )(q.reshape(bsz, seq, d), kv.reshape(bsz, seq, 2 * d), kv.reshape(bsz, seq, 2 * d), bias)
    return o.reshape(bsz * seq, d), lse.reshape(bsz * seq, LANES)


def _head_diag(rows, width):
    row = lax.broadcasted_iota(jnp.int32, (rows, width), 0)
    lane = lax.broadcasted_iota(jnp.int32, (rows, width), 1)
    return (lane // HEAD_DIM) == row


def _rows_last(cache):
    nd = cache.ndim
    t = jnp.transpose(cache, (0,) + tuple(range(2, nd)) + (1,))
    bsz, window = cache.shape[0], cache.shape[1]
    lead = int(np.prod(cache.shape[2:nd - 2]))
    return t.reshape(bsz, lead, cache.shape[nd - 2] * cache.shape[nd - 1], window)


def _rows_second(cache_t, feat):
    bsz, _, _, window = cache_t.shape
    nd = len(feat) + 2
    t = cache_t.reshape((bsz,) + tuple(feat) + (window,))
    return jnp.transpose(t, (0, nd - 1) + tuple(range(1, nd - 1)))


def _shift_in(x, col):
    w = x.shape[1]
    lane = lax.broadcasted_iota(jnp.int32, x.shape, 1)
    return jnp.where(lane == w - 1, col, pltpu.roll(x, w - 1, 1))


A_HEAD_CHUNK = 4


def _samp_a_kernel(q0, q1, q2, n0, n1, n2, m0, m1, m2, c0, c1, c2, b0, b1, b2, bs_ref, o_ref, u0, u1, u2):
    hw = A_HEAD_CHUNK * HEAD_DIM
    diag = _head_diag(8, hw)
    outs, lses = [], []
    for q_ref, n_ref, m_ref, c_ref, b_ref, u_ref in (
            (q0, n0, m0, c0, b0, u0), (q1, n1, m1, c1, b1, u1), (q2, n2, m2, c2, b2, u2)):
        qe = jnp.where(diag, q_ref[0] * ATTN_SCALE, 0.0)
        kt = c_ref[0, 0]
        vt = c_ref[0, 1]
        kn = n_ref[0, 0:1, :]
        vn = n_ref[0, 1:2, :]
        s = _dot(qe.astype(BF16), kt.astype(BF16)) + b_ref[0]
        sn = jnp.sum(qe * kn, axis=-1, keepdims=True) + bs_ref[0][:, :1]
        m = jnp.maximum(jnp.max(s, axis=-1, keepdims=True), sn)
        e = jnp.exp(s - m)
        en = jnp.exp(sn - m)
        l = jnp.sum(e, axis=-1, keepdims=True) + en
        outs.append((_dot_nt(e.astype(BF16), vt.astype(BF16)) + en * vn) / l)
        lses.append(m + jnp.log(l))
        u_ref[0, 0] = _shift_in(kt, m_ref[0, 0])
        u_ref[0, 1] = _shift_in(vt, m_ref[0, 1])
    mx = jnp.maximum(jnp.maximum(lses[0], lses[1]), lses[2])
    es = [jnp.exp(l - mx) for l in lses]
    tot = es[0] + es[1] + es[2]
    merged = (es[0] / tot) * outs[0] + (es[1] / tot) * outs[1] + (es[2] / tot) * outs[2]
    o_ref[0] = jnp.sum(jnp.where(diag, merged, 0.0), axis=0, keepdims=True)


def _samp_a(qs, kvs, caches_t, rel_bias):
    bsz, d = qs[0].shape
    hw = A_HEAD_CHUNK * HEAD_DIM
    nhc = N_HEADS // A_HEAD_CHUNK
    pad_rows = lambda a: jnp.pad(a.reshape(nhc, A_HEAD_CHUNK, -1), ((0, 0), (0, 8 - A_HEAD_CHUNK), (0, 0)))
    biases = []
    for window, dil in DIL_PATTERNS:
        dist = window - np.arange(window)
        biases.append(pad_rows(_bias_from_dist(rel_bias, dist, dist % dil == 0)))
    bself = pad_rows(_bias_from_dist(rel_bias, np.zeros((LANES,), np.int64), np.ones((LANES,), bool)))
    q_spec = pl.BlockSpec((1, 1, hw), lambda b, c: (b, 0, c))
    n_spec = pl.BlockSpec((1, 2, hw), lambda b, c: (b, 0, c))
    m_spec = pl.BlockSpec((1, 2, hw, 1), lambda b, c: (b, 0, c, 0))
    c_spec = lambda w: pl.BlockSpec((1, 2, hw, w), lambda b, c: (b, 0, c, 0))
    b_spec = lambda w: pl.BlockSpec((1, 8, w), lambda b, c: (c, 0, 0))
    windows = [w for w, _ in DIL_PATTERNS]
    res = pl.pallas_call(
        _samp_a_kernel,
        grid=(bsz, nhc),
        in_specs=[q_spec] * 3 + [n_spec] * 3 + [m_spec] * 3 + [c_spec(w) for w in windows]
        + [b_spec(w) for w in windows] + [b_spec(LANES)],
        out_specs=[q_spec] + [c_spec(w) for w in windows],
        out_shape=[jax.ShapeDtypeStruct((bsz, 1, d), F32)] + [jax.ShapeDtypeStruct(c.shape, F32) for c in caches_t],
        compiler_params=_cparams("parallel", "parallel"),
        name="sample_dilated_attn",
    )(*[q.reshape(bsz, 1, d) for q in qs], *[kv.reshape(bsz, 2, d) for kv in kvs],
      *[kv.reshape(bsz, 2, d, 1) for kv in kvs], *caches_t, *biases, bself)
    return res[0].reshape(bsz, d), res[1:]


def _cmp_weights(phi_pe, phi_w1, phi_w2):
    r = CMP_LEN // CMP_STRIDE
    eye = jnp.eye(N_GROUPS, dtype=F32)
    w1r = phi_w1.reshape(2, r, CMP_STRIDE, HEAD_DIM, CMP_HIDDEN)
    w1bd = jnp.einsum("kpsdh,gG->kpsgdGh", w1r, eye).reshape(
        2, r, CMP_STRIDE, N_GROUPS * HEAD_DIM, N_GROUPS * CMP_HIDDEN).astype(BF16)
    w2bd = jnp.einsum("khd,gG->kghGd", phi_w2, eye).reshape(
        2, N_GROUPS * CMP_HIDDEN, N_GROUPS * HEAD_DIM).astype(BF16)
    w2ex = jnp.einsum("khd,gG,r->kghGrd", phi_w2, eye, jnp.ones((GROUP_HEADS,), F32)).reshape(
        2, N_GROUPS * CMP_HIDDEN, D_MODEL).astype(BF16)
    pe = jnp.tile(phi_pe.reshape(2, r, CMP_STRIDE, 1, HEAD_DIM), (1, 1, 1, N_GROUPS, 1)).reshape(
        2, r, CMP_STRIDE, N_GROUPS * HEAD_DIM)
    return w1bd, w2bd, w2ex, pe


def _cmp_prompt_kernel(x_ref, w1_ref, pe_ref, w2_ref, o_ref, *, kind, transposed):
    nch = x_ref.shape[1]
    gw = N_GROUPS * HEAD_DIM
    hid = []
    for part in range(CMP_LEN // CMP_STRIDE):
        acc = jnp.zeros((nch, N_GROUPS * CMP_HIDDEN), F32)
        for s in range(CMP_STRIDE):
            c0 = s * 4 * gw + kind * gw
            x = (x_ref[0, :, c0:c0 + gw] + pe_ref[part, s:s + 1, :]).astype(BF16)
            acc = acc + _dot(x, w1_ref[part, s])
        hid.append(acc)
    h = hid[0] + pltpu.roll(hid[1], nch - 1, 0)
    h = jnp.maximum(h, 0.0).astype(BF16)
    o_ref[0] = _dot_nt(w2_ref[...], h) if transposed else _dot(h, w2_ref[...])


def _cmp_prompt(kvp, bsz, seq, kind, w1bd, pe, w2bd, transposed):
    nch = seq // CMP_STRIDE
    xv = kvp.reshape(bsz, nch, CMP_STRIDE * kvp.shape[1])
    full = lambda a: pl.BlockSpec(a.shape, lambda b: (0,) * a.ndim)
    w1, p, w2 = w1bd[kind], pe[kind], w2bd[kind]
    out_dims = (w2.shape[1], nch) if transposed else (nch, w2.shape[1])
    if transposed:
        w2 = w2.T
    return pl.pallas_call(
        functools.partial(_cmp_prompt_kernel, kind=kind, transposed=transposed),
        grid=(bsz,),
        in_specs=[pl.BlockSpec((1, nch, xv.shape[2]), lambda b: (b, 0, 0)), full(w1), full(p), full(w2)],
        out_specs=pl.BlockSpec((1,) + out_dims, lambda b: (b, 0, 0)),
        out_shape=jax.ShapeDtypeStruct((bsz,) + out_dims, F32),
        compiler_params=_cparams("parallel"),
        name="compress_prompt",
    )(xv, w1, p, w2)


def _masked_softmax(s, valid, axis=-1, exp=jnp.exp):
    m = jnp.max(s, axis=axis, keepdims=True)
    m = jnp.where(m > 0.5 * NEG, m, 0.0)
    e = jnp.where(valid, exp(s - m), 0.0)
    return e / jnp.maximum(jnp.sum(e, axis=axis, keepdims=True), 1e-30)


def _topk_mask_t(score, j, n_iter):
    rank = jnp.zeros(score.shape, F32)
    for i in range(n_iter):
        si = score[i:i + 1, :]
        beats = jnp.logical_or(si > score, jnp.logical_and(si == score, j > i))
        rank = rank + jnp.where(beats, 1.0, 0.0)
    return jnp.where(rank < SEL_TOPK, 1.0, 0.0)


def _nsa_kernel(q_ref, gate_ref, kc_ref, vct_ref, ks_ref, vs_ref, kw_ref, vw_ref, bct_ref, ut_ref, uwt_ref,
                a_ref, ebt_ref, o_ref, ksb, vst, kwb, vwt, selb, m_ref, l_ref, acc_ref, *, seq, tq, tc):
    g = pl.program_id(0)
    qi = pl.program_id(2)
    q0 = qi * tq
    par = g % 2
    nb = seq // SEL_BLOCK
    lane = lax.broadcasted_iota(jnp.int32, (tq, LANES), 1)
    keep = (lane // HEAD_DIM) == par

    @pl.when(qi == 0)
    def _():
        ksb[...] = ks_ref[0].astype(BF16)
        kwb[...] = kw_ref[0].astype(BF16)
        for c in range(seq // (2 * LANES)):
            sl = slice(c * 2 * LANES, (c + 1) * 2 * LANES)
            vst[:, sl] = vs_ref[0, sl, :].T.astype(BF16)
            vwt[:, sl] = vw_ref[0, sl, :].T.astype(BF16)

    qs = []
    for r in range(GROUP_HEADS):
        qh = q_ref[0, :, (r // 2) * LANES:(r // 2 + 1) * LANES] * (ATTN_SCALE * LOG2E)
        qh = jnp.where(par == (r % 2), qh, pltpu.roll(qh, HEAD_DIM, 1))
        qs.append(jnp.where(keep, qh, 0.0).astype(BF16))
    q4 = jnp.concatenate(qs, axis=0)

    bct = bct_ref[0, 0]
    pc = _masked_softmax(_dot_nt(kc_ref[0].astype(BF16), q4) + bct, bct > 0.5 * NEG, axis=0, exp=jnp.exp2)
    oc = _dot(vct_ref[0].astype(BF16), pc.astype(BF16))
    imp = pc[:, 0:tq] + pc[:, tq:2 * tq] + pc[:, 2 * tq:3 * tq] + pc[:, 3 * tq:4 * tq]

    hi = imp.astype(BF16)
    lo = (imp - hi.astype(F32)).astype(BF16)
    imp_t = _dot(a_ref[...], hi) + _dot(a_ref[...], lo)
    j = lax.broadcasted_iota(jnp.int32, (nb, tq), 0)
    qpos = q0 + lax.broadcasted_iota(jnp.int32, (nb, tq), 1)
    cur = qpos // SEL_BLOCK
    forced = jnp.logical_or(j == 0, jnp.logical_or(j == cur, j == cur - 1))
    score = jnp.where(forced, SEL_FORCE, jnp.where(j * SEL_BLOCK <= qpos, imp_t, -1.0))
    sel_t = _topk_mask_t(score, j, nb).astype(BF16)
    sb = (_dot(ebt_ref[...], sel_t) - 1.0) * (-NEG)
    selb[...] = sb

    m_ref[...] = jnp.full(m_ref.shape, NEG, F32)
    l_ref[...] = jnp.zeros(l_ref.shape, F32)
    acc_ref[...] = jnp.zeros(acc_ref.shape, F32)

    def sel_chunk(t, carry):
        k0 = pl.multiple_of(t * tc, tc)
        u0 = pl.multiple_of(k0 - q0 + (seq - tq), tq)
        sbk = selb[pl.ds(k0, tc), :]
        s = (_dot_nt(ksb[pl.ds(k0, tc), :], q4) + ut_ref[0, pl.ds(u0, tc), :]
             + jnp.concatenate([sbk] * GROUP_HEADS, axis=1))
        m_prev = m_ref[...]
        m_new = jnp.maximum(m_prev, jnp.max(s, axis=0, keepdims=True))
        alpha = jnp.exp2(m_prev - m_new)
        e = jnp.exp2(s - m_new)
        l_ref[...] = alpha * l_ref[...] + jnp.sum(e, axis=0, keepdims=True)
        acc_ref[...] = alpha * acc_ref[...] + _dot(vst[:, pl.ds(k0, tc)], e.astype(BF16))
        m_ref[...] = m_new
        return carry

    lax.fori_loop(0, (q0 + tq + tc - 1) // tc, sel_chunk, 0)
    osel = acc_ref[...] / l_ref[...]

    ww = SLIDE_WIN + tq
    w0 = pl.multiple_of(jnp.maximum(q0 - SLIDE_WIN, 0), tq)
    uw0 = pl.multiple_of(w0 - (q0 - SLIDE_WIN), tq)
    s = _dot_nt(kwb[pl.ds(w0, ww), :], q4) + uwt_ref[0, pl.ds(uw0, ww), :]
    e = jnp.exp2(s - jnp.max(s, axis=0, keepdims=True))
    owin = _dot(vwt[:, pl.ds(w0, ww)], e.astype(BF16)) / jnp.sum(e, axis=0, keepdims=True)

    gt = gate_ref[0].T
    rows = []
    for r in range(GROUP_HEADS):
        cs = slice(r * tq, (r + 1) * tq)
        o = (gt[3 * r:3 * r + 1, :] * oc[:, cs] + gt[3 * r + 1:3 * r + 2, :] * osel[:, cs]
             + gt[3 * r + 2:3 * r + 3, :] * owin[:, cs])
        rows.append(jnp.where(par == 0, o[:HEAD_DIM], o[HEAD_DIM:]))
    o_ref[0] = jnp.concatenate(rows, axis=0).T


def _imp_to_block_matrix(n_cmp_pad, n_cmp, n_blk):
    ratio = SEL_BLOCK // CMP_STRIDE
    span = CMP_LEN // CMP_STRIDE
    a = np.zeros((n_blk, n_cmp_pad), np.float32)
    for m in range(ratio):
        for n in range(span):
            for jb in range(n_blk):
                c = ratio * jb + m - n
                if 0 <= c < n_cmp:
                    a[jb, c] += 1.0
    return a


def _heads_to_lanes(t, lead):
    nl = len(lead)
    rows, tq = t.shape[-2:]
    t = t.reshape((N_GROUPS, GROUP_HEADS) + tuple(lead) + (rows, tq))
    t = jnp.transpose(t, (0,) + tuple(range(2, 2 + nl)) + (2 + nl, 1, 3 + nl))
    return t.reshape((N_GROUPS,) + tuple(lead) + (rows, GROUP_HEADS * tq))


def _nsa_prompt(q, gate, kc, vct, kvp, kvw, rel_bias, bsz, seq):
    tq, tc = LANES, 8 * LANES
    d = D_MODEL
    nb = seq // SEL_BLOCK
    ncp = seq // CMP_STRIDE
    nqt = seq // tq
    n_cmp = (seq - CMP_LEN) // CMP_STRIDE + 1
    nq = GROUP_HEADS * tq
    assert ncp == LANES and SLIDE_WIN % tq == 0
    qv = q.reshape(bsz, seq, d)
    gv = gate.reshape(bsz, seq, N_GROUPS * LANES)
    kvpv = kvp.reshape(bsz, seq, kvp.shape[1])
    kvwv = kvw.reshape(bsz, seq, kvw.shape[1])

    sub = np.arange(CMP_STRIDE)[:, None]
    x = (np.arange(2 * ncp) - (ncp - 1))[None, :]
    dist_c = -CMP_STRIDE * x + sub - (CMP_LEN - 1)
    bias_c = _toeplitz(_bias_from_dist(rel_bias, dist_c, dist_c >= 0), ncp, ncp)
    bias_c = jnp.where(jnp.asarray(np.arange(ncp) < n_cmp), bias_c, NEG)
    bias_c = jnp.transpose(bias_c, (0, 3, 2, 1)).reshape(N_HEADS, ncp, nqt, tq)
    bias_ct = _heads_to_lanes(jnp.transpose(bias_c, (0, 2, 1, 3)), (nqt,))
    wu = seq - tq + tc
    strip = _toeplitz_bias(rel_bias, tq, wu, lambda x: (seq - tq) - x, lambda x: (seq - tq) - x >= 0)
    strip_t = _heads_to_lanes(jnp.swapaxes(strip, 1, 2), ())
    ww = 2 * SLIDE_WIN + tq
    strip_w = _toeplitz_bias(rel_bias, tq, ww, lambda x: SLIDE_WIN - x,
                             lambda x: (SLIDE_WIN - x >= 0) & (SLIDE_WIN - x < SLIDE_WIN))
    strip_wt = _heads_to_lanes(jnp.swapaxes(strip_w, 1, 2), ())

    bias_ct, strip_t, strip_wt = bias_ct * LOG2E, strip_t * LOG2E, strip_wt * LOG2E

    a_m = jnp.asarray(_imp_to_block_matrix(ncp, n_cmp, nb), BF16)
    e_bt = jnp.asarray(np.kron(np.eye(nb), np.ones((SEL_BLOCK, 1))), BF16)

    half = lambda g: g // 2
    kv_spec = lambda off: pl.BlockSpec((1, seq, LANES), lambda g, b, qi: (b, 0, off + half(g)))
    full = lambda a: pl.BlockSpec(a.shape, lambda g, b, qi: (0,) * a.ndim)
    out = pl.pallas_call(
        functools.partial(_nsa_kernel, seq=seq, tq=tq, tc=tc),
        grid=(N_GROUPS, bsz, nqt),
        in_specs=[pl.BlockSpec((1, tq, GROUP_HEADS * HEAD_DIM), lambda g, b, qi: (b, qi, g)),
                  pl.BlockSpec((1, tq, LANES), lambda g, b, qi: (b, qi, g)),
                  pl.BlockSpec((1, ncp, LANES), lambda g, b, qi: (b, 0, half(g))),
                  pl.BlockSpec((1, LANES, ncp), lambda g, b, qi: (b, half(g), 0)),
                  kv_spec(4), kv_spec(6), kv_spec(0), kv_spec(2),
                  pl.BlockSpec((1, 1, ncp, nq), lambda g, b, qi: (g, qi, 0, 0)),
                  pl.BlockSpec((1, wu, nq), lambda g, b, qi: (g, 0, 0)),
                  pl.BlockSpec((1, ww, nq), lambda g, b, qi: (g, 0, 0)),
                  full(a_m), full(e_bt)],
        out_specs=pl.BlockSpec((1, tq, GROUP_HEADS * HEAD_DIM), lambda g, b, qi: (b, qi, g)),
        out_shape=jax.ShapeDtypeStruct((bsz, seq, d), F32),
        scratch_shapes=[pltpu.VMEM((seq, LANES), BF16), pltpu.VMEM((LANES, seq), BF16),
                        pltpu.VMEM((seq, LANES), BF16), pltpu.VMEM((LANES, seq), BF16),
                        pltpu.VMEM((seq, tq), F32),
                        pltpu.VMEM((1, nq), F32), pltpu.VMEM((1, nq), F32), pltpu.VMEM((LANES, nq), F32)],
        compiler_params=_cparams("parallel", "parallel", "arbitrary"),
        name="nsa_prompt_attn",
    )(qv, gv, kc, vct, kvpv, kvpv, kvwv, kvwv, bias_ct, strip_t, strip_wt, a_m, e_bt)
    return out.reshape(bsz * seq, d)


CMP_PAGES = 16


def _scmp_kernel(pt_ref, *refs):
    del pt_ref
    pages = refs[:CMP_PAGES]
    w1_ref, pe_ref, o_ref, t_scr, x_scr = refs[CMP_PAGES:]
    cpp = pages[0].shape[3] // CMP_STRIDE
    nh = N_GROUPS * CMP_HIDDEN
    for kind in range(2):
        for i, p_ref in enumerate(pages):
            x = p_ref[0, kind].T
            for hh in range(x.shape[1] // LANES):
                t_scr[i, hh] = x[:, hh * LANES:(hh + 1) * LANES]
            for s in range(CMP_STRIDE):
                for hh in range(x.shape[1] // LANES):
                    x_scr[s, i * cpp:(i + 1) * cpp, hh * LANES:(hh + 1) * LANES] = (
                        t_scr[i, hh, pl.ds(s, cpp, stride=CMP_STRIDE), :])
        for part in range(CMP_LEN // CMP_STRIDE):
            acc = jnp.zeros((x_scr.shape[1], nh), F32)
            for s in range(CMP_STRIDE):
                x = (x_scr[s] + pe_ref[kind, part, s:s + 1, :]).astype(BF16)
                acc = acc + _dot(x, w1_ref[kind, part, s])
            o_ref[kind, 0, :, part * nh:(part + 1) * nh] = acc


def _scmp(cache, page_table, w1bd, pe):
    bsz, n_pages = page_table.shape
    _, _, gw, page = cache.shape
    cpp = page // CMP_STRIDE
    nh = N_GROUPS * CMP_HIDDEN
    n_chunks = n_pages * cpp
    page_spec = lambda i: pl.BlockSpec((1, 2, gw, page), lambda b, j, pt: (pt[b, j * CMP_PAGES + i], 0, 0, 0))
    const = lambda a: pl.BlockSpec(a.shape, lambda b, j, pt: (0,) * a.ndim, pipeline_mode=pl.Buffered(1))
    grid_spec = pltpu.PrefetchScalarGridSpec(
        num_scalar_prefetch=1,
        grid=(bsz, n_pages // CMP_PAGES),
        in_specs=[page_spec(i) for i in range(CMP_PAGES)] + [const(w1bd), const(pe)],
        out_specs=pl.BlockSpec((2, 1, CMP_PAGES * cpp, 2 * nh), lambda b, j, pt: (0, b, j, 0)),
        scratch_shapes=[pltpu.VMEM((CMP_PAGES, gw // LANES, page, LANES), F32),
                        pltpu.VMEM((CMP_STRIDE, CMP_PAGES * cpp, gw), F32)])
    return pl.pallas_call(
        _scmp_kernel,
        grid_spec=grid_spec,
        out_shape=jax.ShapeDtypeStruct((2, bsz, n_chunks, 2 * nh), F32),
        compiler_params=_cparams("parallel", "arbitrary"),
        name="compress_sample",
    )(page_table, *([cache] * CMP_PAGES), w1bd, pe)


def _s1_kernel(ab_ref, w2_ref, q_ref, bc_ref, at_ref, gs_ref, oc_ref, it_ref):
    nck = ab_ref.shape[2]
    nh = N_GROUPS * CMP_HIDDEN
    kv = []
    for kind in range(2):
        h = ab_ref[kind, 0, :, :nh] + pltpu.roll(ab_ref[kind, 0, :, nh:], nck - 1, 0)
        kv.append(_dot(jnp.maximum(h, 0.0).astype(BF16), w2_ref[kind]).astype(BF16))
    diag = _head_diag(N_HEADS, D_MODEL)
    qe = jnp.where(diag, q_ref[0] * ATTN_SCALE, 0.0).astype(BF16)
    bc = bc_ref[...]
    pc = _masked_softmax(_dot_nt(qe, kv[0]) + bc, bc > 0.5 * NEG)
    o = _dot(pc.astype(BF16), kv[1])
    oc_ref[0] = jnp.sum(jnp.where(diag, o, 0.0), axis=0, keepdims=True)
    hi = pc.astype(BF16)
    lo = (pc - hi.astype(F32)).astype(BF16)
    imp = _dot(gs_ref[...], hi) + _dot(gs_ref[...], lo)
    it_ref[0] = _split_dot_nt(at_ref[...], imp)


def _s1(ab, w2ex, q, rel_bias, past):
    _, bsz, nck, _ = ab.shape
    d = D_MODEL
    n_cmp = (past + 1 - CMP_LEN) // CMP_STRIDE + 1
    n_blk = -(-(past + 1) // SEL_BLOCK)
    nbp = -(-n_blk // 8) * 8
    n = np.arange(nck)
    dist_c = past - (n * CMP_STRIDE + CMP_LEN - 1)
    bias_c = _bias_from_dist(rel_bias, dist_c, (dist_c >= 0) & (n < n_cmp))
    a_t = jnp.asarray(_imp_to_block_matrix(nck, n_cmp, nbp)[:nbp] * (np.arange(nbp)[:, None] < n_blk), BF16)
    gsum = jnp.asarray(np.kron(np.eye(8, N_GROUPS), np.ones((1, GROUP_HEADS))), BF16)
    full = lambda a: pl.BlockSpec(a.shape, lambda b: (0,) * a.ndim)
    oc, imp_t = pl.pallas_call(
        _s1_kernel,
        grid=(bsz,),
        in_specs=[pl.BlockSpec((2, 1, nck, ab.shape[3]), lambda b: (0, b, 0, 0)), full(w2ex),
                  pl.BlockSpec((1, 1, d), lambda b: (b, 0, 0)), full(bias_c), full(a_t), full(gsum)],
        out_specs=[pl.BlockSpec((1, 1, d), lambda b: (b, 0, 0)), pl.BlockSpec((1, nbp, 8), lambda b: (b, 0, 0))],
        out_shape=[jax.ShapeDtypeStruct((bsz, 1, d), F32), jax.ShapeDtypeStruct((bsz, nbp, 8), F32)],
        compiler_params=_cparams("parallel"),
        name="sample_compressed_attn",
    )(ab, w2ex, q.reshape(bsz, 1, d), bias_c, a_t, gsum)
    return oc.reshape(bsz, d), imp_t


def _s2_kernel(imp_ref, idx_ref, s_scr, *, n_blk, cur):
    shape = imp_ref.shape
    j = lax.broadcasted_iota(jnp.int32, shape, 0)
    forced = jnp.logical_or(j == 0, jnp.logical_or(j == cur, j == cur - 1))
    score = jnp.where(forced, SEL_FORCE, jnp.where(j < n_blk, imp_ref[...], -2.0))
    s_scr[...] = score

    def body(i, rank):
        si = s_scr[pl.ds(i, 1), :]
        beats = jnp.logical_or(si > score, jnp.logical_and(si == score, j > i))
        return rank + jnp.where(beats, 1.0, 0.0)

    rank = lax.fori_loop(0, n_blk, body, jnp.zeros(shape, F32))
    for k in range(SEL_TOPK):
        idx_ref[k:k + 1, :] = jnp.sum(jnp.where(rank == k, j, 0), axis=0, keepdims=True)


def _s2(imp_t, past):
    n_blk = -(-(past + 1) // SEL_BLOCK)
    return pl.pallas_call(
        functools.partial(_s2_kernel, n_blk=n_blk, cur=past // SEL_BLOCK),
        out_shape=jax.ShapeDtypeStruct((SEL_TOPK, imp_t.shape[1]), jnp.int32),
        scratch_shapes=[pltpu.VMEM(imp_t.shape, F32)],
        name="sample_block_topk",
    )(imp_t)


def _bucket_thresholds(max_dist):
    b = _bucket_np(np.arange(max_dist + 1))
    return [int(np.argmax(b >= k)) if (b >= k).any() else max_dist + 1 for k in range(REL_BUCKETS)]


def _attend_t(q, kt, vt, bias, valid, kn, vn, bias_new):
    s = _dot(q.astype(BF16), kt.astype(BF16)) + bias
    if valid is not None:
        s = jnp.where(valid, s, NEG)
    sn = jnp.sum(q * kn, axis=-1, keepdims=True) + bias_new
    m = jnp.maximum(jnp.max(s, axis=-1, keepdims=True), sn)
    e = jnp.exp(s - m)
    en = jnp.exp(sn - m)
    l = jnp.sum(e, axis=-1, keepdims=True) + en
    return (_dot_nt(e.astype(BF16), vt.astype(BF16)) + en * vn) / l


def _ssel_kernel(idx_ref, page_ref, *refs, past, thresholds):
    del page_ref
    blocks = refs[:SEL_TOPK]
    q_ref, n_ref, tb_ref, o_ref = refs[SEL_TOPK:]
    b = pl.program_id(0)
    g = pl.program_id(1)
    page = blocks[0].shape[3]
    kt = jnp.concatenate([blk[0, 0] for blk in blocks], axis=1)
    vt = jnp.concatenate([blk[0, 1] for blk in blocks], axis=1)
    nk = SEL_TOPK * page
    lane = lax.broadcasted_iota(jnp.int32, (1, nk), 1)
    row = lane % page
    pos = jnp.zeros((1, nk), jnp.int32)
    inside = jnp.zeros((1, nk), jnp.int32)
    bpp = page // SEL_BLOCK
    for k in range(SEL_TOPK):
        blk_idx = idx_ref[(b * N_GROUPS + g) * SEL_TOPK + k]
        mine = lane // page == k
        pos = jnp.where(mine, (blk_idx // bpp) * page + row, pos)
        inside = jnp.where(mine, jnp.where(row // SEL_BLOCK == blk_idx % bpp, 1, 0), inside)
    valid = jnp.logical_and(inside == 1, pos < past)
    dist = past - pos
    bias = jnp.broadcast_to(tb_ref[0, 0][:, :1], (8, nk))
    for k in range(1, REL_BUCKETS):
        bias = jnp.where(dist >= thresholds[k], tb_ref[0, k][:, :1], bias)
    q = q_ref[0, 0] * ATTN_SCALE
    o_ref[0, 0] = _attend_t(q, kt, vt, bias, valid, n_ref[0, 0, 0], n_ref[0, 1, 0], tb_ref[0, 0][:, :1])


def _ssel(sel_idx, cache_t, page_table, q, kvp_new, rel_bias, past):
    bsz = q.shape[0]
    page = cache_t.shape[3]
    bpp = page // SEL_BLOCK
    cur = past // SEL_BLOCK
    cached = jnp.minimum(sel_idx, cur - 1)
    pages = jnp.take_along_axis(page_table, (cached // bpp).reshape(bsz, -1), axis=1).reshape(-1)
    q4 = jnp.pad(q.reshape(bsz, N_GROUPS, GROUP_HEADS, HEAD_DIM), ((0, 0), (0, 0), (0, 8 - GROUP_HEADS), (0, 0)))
    new = kvp_new.reshape(bsz, 4, N_GROUPS, 1, HEAD_DIM)
    tb = jnp.transpose(rel_bias.astype(F32).reshape(REL_BUCKETS, N_GROUPS, GROUP_HEADS), (1, 0, 2))
    tb = jnp.pad(tb, ((0, 0), (0, 0), (0, 8 - GROUP_HEADS)))
    tb = jnp.broadcast_to(tb[..., None], tb.shape + (LANES,))
    blk_spec = lambda k: pl.BlockSpec(
        (1, 2, HEAD_DIM, page), lambda b, g, idx, pg: (pg[(b * N_GROUPS + g) * SEL_TOPK + k], 1, g, 0))
    grid_spec = pltpu.PrefetchScalarGridSpec(
        num_scalar_prefetch=2,
        grid=(bsz, N_GROUPS),
        in_specs=[blk_spec(k) for k in range(SEL_TOPK)] + [
            pl.BlockSpec((1, 1, 8, HEAD_DIM), lambda b, g, idx, pg: (b, g, 0, 0)),
            pl.BlockSpec((1, 2, 1, 1, HEAD_DIM), lambda b, g, idx, pg: (b, 1, g, 0, 0)),
            pl.BlockSpec((1, REL_BUCKETS, 8, LANES), lambda b, g, idx, pg: (g, 0, 0, 0))],
        out_specs=pl.BlockSpec((1, 1, 8, HEAD_DIM), lambda b, g, idx, pg: (b, g, 0, 0)))
    out = pl.pallas_call(
        functools.partial(_ssel_kernel, past=past, thresholds=tuple(_bucket_thresholds(past))),
        grid_spec=grid_spec,
        out_shape=jax.ShapeDtypeStruct((bsz, N_GROUPS, 8, HEAD_DIM), F32),
        compiler_params=_cparams("parallel", "parallel"),
        name="sample_selected_attn",
    )(sel_idx.reshape(-1), pages, *([cache_t] * SEL_TOPK), q4, new, tb)
    return out[:, :, :GROUP_HEADS].reshape(bsz, N_HEADS, HEAD_DIM)


def _swin_kernel(c_ref, q_ref, n_ref, m_ref, bw_ref, b0_ref, o_ref, u_ref):
    kt = c_ref[0, 0]
    vt = c_ref[0, 1]
    q = q_ref[0] * ATTN_SCALE
    o = _attend_t(q, kt, vt, bw_ref[...], None, n_ref[0, 0:1, :], n_ref[0, 1:2, :], b0_ref[...][:, :1])
    grp = lax.broadcasted_iota(jnp.int32, (N_HEADS, HEAD_DIM), 0) // GROUP_HEADS
    out = jnp.zeros((N_HEADS, HEAD_DIM), F32)
    for g in range(N_GROUPS):
        out = jnp.where(grp == g, o[:, g * HEAD_DIM:(g + 1) * HEAD_DIM], out)
    o_ref[0] = out
    u_ref[0, 0] = _shift_in(kt, m_ref[0, 0])
    u_ref[0, 1] = _shift_in(vt, m_ref[0, 1])


def _swin(cwin_t, q, kvw_new, rel_bias):
    bsz = q.shape[0]
    _, _, gw, wlen = cwin_t.shape
    onehot = jnp.asarray(np.kron(np.eye(N_GROUPS), np.ones((GROUP_HEADS, 1))), F32)
    qbd = jnp.einsum("bhd,hg->bhgd", q.reshape(bsz, N_HEADS, HEAD_DIM), onehot).reshape(bsz, N_HEADS, gw)
    dist_w = wlen - np.arange(wlen)
    bw = _bias_from_dist(rel_bias, dist_w, dist_w < SLIDE_WIN)
    b0 = _bias_from_dist(rel_bias, np.zeros((LANES,), np.int64), np.ones((LANES,), bool))
    full = lambda a: pl.BlockSpec(a.shape, lambda b: (0,) * a.ndim)
    c_spec = pl.BlockSpec((1, 2, gw, wlen), lambda b: (b, 0, 0, 0))
    o_spec = pl.BlockSpec((1, N_HEADS, HEAD_DIM), lambda b: (b, 0, 0))
    return pl.pallas_call(
        _swin_kernel,
        grid=(bsz,),
        in_specs=[c_spec, pl.BlockSpec((1, N_HEADS, gw), lambda b: (b, 0, 0)),
                  pl.BlockSpec((1, 2, gw), lambda b: (b, 0, 0)),
                  pl.BlockSpec((1, 2, gw, 1), lambda b: (b, 0, 0, 0)), full(bw), full(b0)],
        out_specs=[o_spec, c_spec],
        out_shape=[jax.ShapeDtypeStruct((bsz, N_HEADS, HEAD_DIM), F32), jax.ShapeDtypeStruct(cwin_t.shape, F32)],
        compiler_params=_cparams("parallel"),
        name="sample_window_attn",
    )(cwin_t, qbd, kvw_new.reshape(bsz, 2, gw), kvw_new.reshape(bsz, 2, gw, 1), bw, b0)


def _gate_sum_kernel(g_ref, oc_ref, os_ref, ow_ref, o_ref):
    g = g_ref[...]
    o_ref[...] = g[:, :, 0:1] * oc_ref[...] + g[:, :, 1:2] * os_ref[...] + g[:, :, 2:3] * ow_ref[...]


def _gate_sum(gate, oc, os_, ow):
    return pl.pallas_call(
        _gate_sum_kernel,
        out_shape=jax.ShapeDtypeStruct(oc.shape, F32),
        name="sample_gate_sum",
    )(gate, oc, os_, ow)


def _split_a_weights(w_in):
    d = D_MODEL
    w = w_in.astype(BF16)
    qs = [w[:, g * 3 * d:g * 3 * d + d] for g in range(len(DIL_PATTERNS))]
    kvs = [w[:, g * 3 * d + d:(g + 1) * 3 * d] for g in range(len(DIL_PATTERNS))]
    return qs + kvs


def _split_b_weights(w_in):
    d = D_MODEL
    gw = N_GROUPS * HEAD_DIM
    w = w_in.astype(BF16)
    wg = w[:, d + 6 * gw:].reshape(d, N_GROUPS, GROUP_HEADS * 3)
    wg = jnp.pad(wg, ((0, 0), (0, 0), (0, LANES - GROUP_HEADS * 3))).reshape(d, N_GROUPS * LANES)
    return [w[:, :d], w[:, d:d + 4 * gw], w[:, d + 4 * gw:d + 6 * gw], wg]


def _unblock_gate(gate):
    m = gate.shape[0]
    return gate.reshape(m, N_GROUPS, LANES)[:, :, :GROUP_HEADS * 3].reshape(m, N_HEADS * 3)


def kernel(x_prompt, x_sample, cache_a_win0, cache_a_win1, cache_a_win2, cache_b_kv, cache_b_win, page_table,
           rel_bias, norm_mix, norm_mlp, norm_final, a_w_in, a_w_out, b_w_in, b_w_out, b_phi_pe, b_phi_w1,
           b_phi_w2, mlp_w_up, mlp_w_down):
    bsz, seq, d = x_prompt.shape
    sb = x_sample.shape[0]
    past = page_table.shape[1] * cache_b_kv.shape[2]
    tm_p, tm_s = 512, sb
    tm_w = 1024
    xp = x_prompt.reshape(bsz * seq, d)
    xs = x_sample.reshape(sb, d)
    acts6 = ("none",) * 6

    wa = _split_a_weights(a_w_in[0])
    wa_out = a_w_out[0].astype(BF16)
    full_window = [gi for gi, (w, _) in enumerate(DIL_PATTERNS) if w >= seq]
    pa = _norm_matmul(xp, norm_mix[0], wa, acts6, tm_w, 4, [wa[3 + gi].T for gi in full_window], seq)
    sa = _norm_matmul(xs, norm_mix[0], wa, acts6, tm_s, 4)
    os_, ls = [], []
    for gi, (_, dil) in enumerate(DIL_PATTERNS):
        o, lse = _swa_group(pa[gi], pa[3 + gi], rel_bias, bsz, seq, dil)
        os_.append(o)
        ls.append(lse)
    xp = _merge_proj(os_, ls, xp, wa_out, tm_p)
    a_caches = [_rows_last(c[0]) for c in (cache_a_win0, cache_a_win1, cache_a_win2)]
    oa, a_new_t = _samp_a(sa[:3], sa[3:], a_caches, rel_bias)
    xs = _matmul_res(oa, xs, wa_out, tm_s)
    cache_shape = lambda n: (1, -1, n, 2, N_HEADS, HEAD_DIM)
    a_new_p = [pa[3 + gi].reshape(bsz, seq, 2 * d)[:, seq - min(w, seq):].reshape(cache_shape(min(w, seq)))
               for gi, (w, _) in enumerate(DIL_PATTERNS)]
    for k, gi in enumerate(full_window):
        a_new_p[gi] = _rows_second(pa[6 + k].reshape(bsz, 2, d, seq), (2, N_HEADS, HEAD_DIM))[None]
    a_new_s = [_rows_second(c, (2, N_HEADS, HEAD_DIM)) for c in a_new_t]

    wu0, wd0 = mlp_w_up[0].astype(BF16), mlp_w_down[0].astype(BF16)
    xp = _mlp(xp, norm_mlp[0], wu0, wd0, norm_final, False, tm_w, 1024)
    xs = _mlp(xs, norm_mlp[0], wu0, wd0, norm_final, False, tm_s, 1024)

    wb = _split_b_weights(b_w_in[0])
    wb_out = b_w_out[0].astype(BF16)
    acts_b = ("none", "none", "none", "sigmoid")
    qp, kvp_p, kvw_p, gate_p, kvp_t = _norm_matmul(xp, norm_mix[1], wb, acts_b, tm_p, 1, [wb[1].T], seq)
    qs_, kvp_s, kvw_s, gate_s = _norm_matmul(xs, norm_mix[1], wb, acts_b, tm_s, 1)
    w1bd, w2bd, w2ex, pe = _cmp_weights(b_phi_pe[0], b_phi_w1[0], b_phi_w2[0])
    kc = _cmp_prompt(kvp_p, bsz, seq, 0, w1bd, pe, w2bd, False)
    vct = _cmp_prompt(kvp_p, bsz, seq, 1, w1bd, pe, w2bd, True)
    ob = _nsa_prompt(qp, gate_p, kc, vct, kvp_p, kvw_p, rel_bias, bsz, seq)
    xp = _matmul_res(ob, xp, wb_out, tm_p)

    cache_t = _rows_last(cache_b_kv[0])
    cwin_t = _rows_last(cache_b_win[0])
    ab = _scmp(cache_t, page_table, w1bd, pe)
    oc, imp_t = _s1(ab, w2ex, qs_, rel_bias, past)
    imp_t = jnp.transpose(imp_t[:, :, :N_GROUPS], (1, 0, 2)).reshape(imp_t.shape[1], sb * N_GROUPS)
    sel_idx = jnp.transpose(_s2(imp_t, past)).reshape(sb, N_GROUPS, SEL_TOPK)
    osel = _ssel(sel_idx, cache_t, page_table, qs_, kvp_s, rel_bias, past)
    owin, b_win_t = _swin(cwin_t, qs_, kvw_s, rel_bias)
    obs = _gate_sum(_unblock_gate(gate_s).reshape(sb, N_HEADS, 3), oc.reshape(sb, N_HEADS, HEAD_DIM), osel, owin)
    xs = _matmul_res(obs.reshape(sb, d), xs, wb_out, tm_s)
    b_win_s = _rows_second(b_win_t, (2, N_GROUPS, HEAD_DIM))

    wu1, wd1 = mlp_w_up[1].astype(BF16), mlp_w_down[1].astype(BF16)
    yp = _mlp(xp, norm_mlp[1], wu1, wd1, norm_final, True, tm_w, 1024)
    ys = _mlp(xs, norm_mlp[1], wu1, wd1, norm_final, True, tm_s, 1024)

    gw = N_GROUPS * HEAD_DIM
    keep = min(SLIDE_WIN, seq)
    outs = [yp.reshape(bsz, seq, d), ys.reshape(sb, 1, d)]
    for p, s in zip(a_new_p, a_new_s):
        outs.append(p)
        outs.append(s[None])
    outs.append(_rows_second(kvp_t.reshape(bsz, 4, N_GROUPS * HEAD_DIM, seq), (4, N_GROUPS, HEAD_DIM))[None])
    outs.append(kvp_s.reshape(1, sb, 1, 4, N_GROUPS, HEAD_DIM))
    outs.append(kvw_p.reshape(bsz, seq, 2 * gw)[:, seq - keep:].reshape(1, bsz, keep, 2, N_GROUPS, HEAD_DIM))
    outs.append(b_win_s[None])
    return tuple(outs)
```

```python
import functools
import math

import numpy as np
import jax
import jax.numpy as jnp
from jax import lax
from jax.experimental import pallas as pl
from jax.experimental.pallas import tpu as pltpu

F32 = jnp.float32
BF16 = jnp.bfloat16

D_MODEL = 1024
HEAD_DIM = 64
N_HEADS = 16
ATTN_SCALE = HEAD_DIM ** -0.5
LOG2E = math.log2(math.e)
RMS_EPS = 1e-6
REL_BUCKETS = 32
REL_MAX_DIST = 2048
DIL_PATTERNS = ((128, 1), (512, 4), (2048, 16))
A_KEYS = 128
N_GROUPS = 4
GROUP_HEADS = N_HEADS // N_GROUPS
CMP_LEN = 32
CMP_STRIDE = 16
CMP_HIDDEN = 128
SEL_BLOCK = 64
SEL_TOPK = 16
SEL_FORCE = 1e4
SLIDE_WIN = 512

LANES = 128
VMEM_LIMIT = 56 * 1024 * 1024
NEG = -1e30

NT_DIMS = (((1,), (1,)), ((), ()))


def _cparams(*sem):
    return pltpu.CompilerParams(dimension_semantics=sem, vmem_limit_bytes=VMEM_LIMIT)


def _dot(a, b):
    return jnp.dot(a, b, preferred_element_type=F32)


def _dot_nt(a, b):
    return lax.dot_general(a, b, NT_DIMS, preferred_element_type=F32)


def _split_dot(x, w):
    hi = x.astype(BF16)
    lo = (x - hi.astype(F32)).astype(BF16)
    return _dot(hi, w) + _dot(lo, w)


def _split_dot_nt(w, x):
    hi = x.astype(BF16)
    lo = (x - hi.astype(F32)).astype(BF16)
    return _dot_nt(w, hi) + _dot_nt(w, lo)


def _rms(x, g):
    ms = jnp.mean(x * x, axis=-1, keepdims=True)
    return x * lax.rsqrt(ms + RMS_EPS) * g


def _bucket_np(dist):
    dist = np.maximum(np.asarray(dist, np.int64), 0)
    max_exact = REL_BUCKETS // 2
    ratio = (np.log(np.maximum(dist, 1).astype(np.float32) / np.float32(max_exact))
             / np.float32(math.log(REL_MAX_DIST / max_exact)))
    large = np.minimum(max_exact + (ratio * (REL_BUCKETS - max_exact)).astype(np.int32), REL_BUCKETS - 1)
    return np.where(dist < max_exact, dist, large).astype(np.int32)


def _bias_from_dist(rel_bias, dist, valid):
    idx = jnp.asarray(_bucket_np(dist))
    b = jnp.take(rel_bias.astype(F32).T, idx, axis=1)
    return jnp.where(jnp.asarray(valid)[None], b, NEG)


def _toeplitz(w, n, m):
    length = n + m
    lead = w.shape[:-1]
    t = jnp.tile(w, (1,) * len(lead) + (n,))[..., :n * (length - 1)].reshape(lead + (n, length - 1))
    return t[..., n - 1:n - 1 + m]


def _toeplitz_bias(rel_bias, n, m, dist_of, valid_of):
    x = np.arange(n + m) - (n - 1)
    return _toeplitz(_bias_from_dist(rel_bias, dist_of(x), valid_of(x)), n, m)


def _norm_matmul_kernel(x_ref, g_ref, *refs, n_w, n_t, acts):
    w_refs, wt_refs = refs[:n_w], refs[n_w:n_w + n_t]
    o_refs, ot_refs = refs[n_w + n_t:2 * n_w + n_t], refs[2 * n_w + n_t:2 * (n_w + n_t)]
    xn_ref = refs[2 * (n_w + n_t)]

    @pl.when(pl.program_id(1) == 0)
    def _():
        xn_ref[...] = _rms(x_ref[...], g_ref[...]).astype(BF16)

    xn = xn_ref[...]
    for w_ref, o_ref, act in zip(w_refs, o_refs, acts):
        y = _dot(xn, w_ref[...])
        if act == "sigmoid":
            y = jax.nn.sigmoid(y)
        o_ref[...] = y
    for wt_ref, ot_ref in zip(wt_refs, ot_refs):
        ot_ref[0] = _dot_nt(wt_ref[...], xn)


def _norm_matmul(x, g, ws, acts, tm, nj, wts=(), seq=None):
    m, d = x.shape
    kern = functools.partial(_norm_matmul_kernel, n_w=len(ws), n_t=len(wts), acts=tuple(acts))
    in_specs = [pl.BlockSpec((tm, d), lambda i, j: (i, 0)), pl.BlockSpec((1, d), lambda i, j: (0, 0))]
    in_specs += [pl.BlockSpec((d, w.shape[1] // nj), lambda i, j: (0, j)) for w in ws]
    in_specs += [pl.BlockSpec((w.shape[0] // nj, d), lambda i, j: (j, 0)) for w in wts]
    out_specs = [pl.BlockSpec((tm, w.shape[1] // nj), lambda i, j: (i, j)) for w in ws]
    out_shape = [jax.ShapeDtypeStruct((m, w.shape[1]), F32) for w in ws]
    if wts:
        tps = seq // tm
        out_specs += [pl.BlockSpec((1, w.shape[0] // nj, tm), lambda i, j: (i // tps, j, i % tps)) for w in wts]
        out_shape += [jax.ShapeDtypeStruct((m // seq, w.shape[0], seq), F32) for w in wts]
    return pl.pallas_call(
        kern,
        grid=(m // tm, nj),
        in_specs=in_specs,
        out_specs=out_specs,
        out_shape=out_shape,
        scratch_shapes=[pltpu.VMEM((tm, d), BF16)],
        compiler_params=_cparams("parallel", "arbitrary"),
        name="norm_matmul",
    )(x, g.reshape(1, d), *ws, *wts)


def _matmul_res_kernel(a_ref, x_ref, w_ref, o_ref):
    o_ref[...] = x_ref[...] + _dot(a_ref[...].astype(BF16), w_ref[...])


def _matmul_res(a, x, w, tm):
    m, d = x.shape
    return pl.pallas_call(
        _matmul_res_kernel,
        grid=(m // tm,),
        in_specs=[pl.BlockSpec((tm, a.shape[1]), lambda i: (i, 0)),
                  pl.BlockSpec((tm, d), lambda i: (i, 0)),
                  pl.BlockSpec(w.shape, lambda i: (0, 0))],
        out_specs=pl.BlockSpec((tm, d), lambda i: (i, 0)),
        out_shape=jax.ShapeDtypeStruct((m, d), F32),
        compiler_params=_cparams("parallel"),
        name="matmul_res",
    )(a, x, w)


def _merge_proj_kernel(o0, o1, o2, l0, l1, l2, e_ref, x_ref, w_ref, out_ref):
    ls = [l0[...], l1[...], l2[...]]
    mx = jnp.maximum(jnp.maximum(ls[0], ls[1]), ls[2])
    es = [jnp.exp(l - mx) for l in ls]
    tot = es[0] + es[1] + es[2]
    merged = None
    for e, o in zip(es, (o0, o1, o2)):
        part = _split_dot(e / tot, e_ref[...]) * o[...]
        merged = part if merged is None else merged + part
    out_ref[...] = x_ref[...] + _dot(merged.astype(BF16), w_ref[...])


def _merge_proj(os_, ls, x, w, tm):
    m, d = x.shape
    expand = jnp.asarray(np.kron(np.eye(LANES, N_HEADS), np.ones((1, HEAD_DIM))), BF16)
    row = lambda n: pl.BlockSpec((tm, n), lambda i: (i, 0))
    return pl.pallas_call(
        _merge_proj_kernel,
        grid=(m // tm,),
        in_specs=[row(d)] * 3 + [row(LANES)] * 3 + [
            pl.BlockSpec(expand.shape, lambda i: (0, 0)), row(d), pl.BlockSpec(w.shape, lambda i: (0, 0))],
        out_specs=row(d),
        out_shape=jax.ShapeDtypeStruct((m, d), F32),
        compiler_params=_cparams("parallel"),
        name="merge_proj",
    )(*os_, *ls, expand, x, w)


def _mlp_kernel(x_ref, g_ref, wu_ref, wd_ref, gf_ref, o_ref, xn_ref, acc_ref, *, final_norm):
    f = pl.program_id(1)

    @pl.when(f == 0)
    def _():
        xn_ref[...] = _rms(x_ref[...], g_ref[...]).astype(BF16)
        acc_ref[...] = jnp.zeros_like(acc_ref)

    h = jnp.square(jnp.maximum(_dot(xn_ref[...], wu_ref[...]), 0.0))
    acc_ref[...] += _dot(h.astype(BF16), wd_ref[...])

    @pl.when(f == pl.num_programs(1) - 1)
    def _():
        y = x_ref[...] + acc_ref[...]
        if final_norm:
            y = _rms(y, gf_ref[...])
        o_ref[...] = y


def _mlp(x, g, wu, wd, gf, final_norm, tm, tf):
    m, d = x.shape
    dff = wu.shape[1]
    return pl.pallas_call(
        functools.partial(_mlp_kernel, final_norm=final_norm),
        grid=(m // tm, dff // tf),
        in_specs=[pl.BlockSpec((tm, d), lambda i, f: (i, 0)),
                  pl.BlockSpec((1, d), lambda i, f: (0, 0)),
                  pl.BlockSpec((d, tf), lambda i, f: (0, f)),
                  pl.BlockSpec((tf, d), lambda i, f: (f, 0)),
                  pl.BlockSpec((1, d), lambda i, f: (0, 0))],
        out_specs=pl.BlockSpec((tm, d), lambda i, f: (i, 0)),
        out_shape=jax.ShapeDtypeStruct((m, d), F32),
        scratch_shapes=[pltpu.VMEM((tm, d), BF16), pltpu.VMEM((tm, d), F32)],
        compiler_params=_cparams("parallel", "arbitrary"),
        name="mlp",
    )(x, g.reshape(1, d), wu, wd, gf.reshape(1, d))


def _swa_kernel(q_ref, k_ref, v_ref, b_ref, o_ref, lse_ref, *, dil):
    t = A_KEYS
    hp = pl.program_id(1)
    tiles = q_ref.shape[1] // (dil * t)
    lane = lax.broadcasted_iota(jnp.int32, (t, LANES), 1)
    low = lane < HEAD_DIM

    @pl.when(hp == 0)
    def _():
        lse_ref[...] = jnp.zeros_like(lse_ref)

    def tile(r, qi):
        rows = pl.ds(r + dil * t * qi, t, stride=dil)
        prev = pl.ds(r + dil * t * max(qi - 1, 0), t, stride=dil)
        first = min(qi, 1)
        q2 = (q_ref[0, rows, :] * ATTN_SCALE).astype(BF16)
        k2 = jnp.concatenate([k_ref[0, prev, :], k_ref[0, rows, :]], axis=0).astype(BF16)
        v2 = jnp.concatenate([v_ref[0, prev, :], v_ref[0, rows, :]], axis=0).astype(BF16)
        lse_t = lse_ref[0, rows, :]
        outs = []
        for a in range(2):
            h = 2 * hp + a
            qm = jnp.where(low if a == 0 else jnp.logical_not(low), q2, jnp.zeros_like(q2))
            s = _dot_nt(qm, k2) + b_ref[first, h]
            m = jnp.max(s, axis=-1, keepdims=True)
            e = jnp.exp(s - m)
            l = jnp.sum(e, axis=-1, keepdims=True)
            outs.append(_dot(e.astype(BF16), v2) / l)
            lse_t = jnp.where(lane == h, m + jnp.log(l), lse_t)
        o_ref[0, rows, :] = jnp.where(low, outs[0], outs[1])
        lse_ref[0, rows, :] = lse_t

    for r in range(dil):
        for qi in range(tiles):
            tile(r, qi)


def _swa_bias(rel_bias, dil, t):
    later = _toeplitz_bias(rel_bias, t, 2 * t, lambda x: dil * (t - x), lambda x: (t - x >= 0) & (t - x <= t))
    first = jnp.where(jnp.asarray(np.arange(2 * t) >= t)[None, None, :], later, NEG)
    return jnp.stack([first, later])


def _swa_group(q, kv, rel_bias, bsz, seq, dil):
    d = D_MODEL
    nhp = N_HEADS // 2
    assert seq % (dil * A_KEYS) == 0
    bias = _swa_bias(rel_bias, dil, A_KEYS)
    pair = lambda off: pl.BlockSpec((1, seq, LANES), lambda b, hp: (b, 0, off + hp))
    o, lse = pl.pallas_call(
        functools.partial(_swa_kernel, dil=dil),
        grid=(bsz, nhp),
        in_specs=[pair(0), pair(0), pair(nhp), pl.BlockSpec(bias.shape, lambda b, hp: (0, 0, 0, 0))],
        out_specs=[pair(0), pl.BlockSpec((1, seq, LANES), lambda b, hp: (b, 0, 0))],
        out_shape=[jax.ShapeDtypeStruct((bsz, seq, d), F32), jax.ShapeDtypeStruct((bsz, seq, LANES), F32)],
        compiler_params=_cparams("parallel", "arbitrary"),
        name="swa_attn",
    )(q.reshape(bsz, seq, d), kv.reshape(bsz, seq, 2 * d), kv.reshape(bsz, seq, 2 * d), bias)
    return o.reshape(bsz * seq, d), lse.reshape(bsz * seq, LANES)


def _head_diag(rows, width):
    row = lax.broadcasted_iota(jnp.int32, (rows, width), 0)
    lane = lax.broadcasted_iota(jnp.int32, (rows, width), 1)
    return (lane // HEAD_DIM) == row


def _rows_last(cache):
    nd = cache.ndim
    t = jnp.transpose(cache, (0,) + tuple(range(2, nd)) + (1,))
    bsz, window = cache.shape[0], cache.shape[1]
    lead = int(np.prod(cache.shape[2:nd - 2]))
    return t.reshape(bsz, lead, cache.shape[nd - 2] * cache.shape[nd - 1], window)


def _rows_second(cache_t, feat):
    bsz, _, _, window = cache_t.shape
    nd = len(feat) + 2
    t = cache_t.reshape((bsz,) + tuple(feat) + (window,))
    return jnp.transpose(t, (0, nd - 1) + tuple(range(1, nd - 1)))


def _shift_in(x, col):
    w = x.shape[1]
    lane = lax.broadcasted_iota(jnp.int32, x.shape, 1)
    return jnp.where(lane == w - 1, col, pltpu.roll(x, w - 1, 1))


A_HEAD_CHUNK = 4


def _samp_a_kernel(q0, q1, q2, n0, n1, n2, m0, m1, m2, c0, c1, c2, b0, b1, b2, bs_ref, o_ref, u0, u1, u2):
    hw = A_HEAD_CHUNK * HEAD_DIM
    diag = _head_diag(8, hw)
    outs, lses = [], []
    for q_ref, n_ref, m_ref, c_ref, b_ref, u_ref in (
            (q0, n0, m0, c0, b0, u0), (q1, n1, m1, c1, b1, u1), (q2, n2, m2, c2, b2, u2)):
        qe = jnp.where(diag, q_ref[0] * ATTN_SCALE, 0.0)
        kt = c_ref[0, 0]
        vt = c_ref[0, 1]
        kn = n_ref[0, 0:1, :]
        vn = n_ref[0, 1:2, :]
        s = _dot(qe.astype(BF16), kt.astype(BF16)) + b_ref[0]
        sn = jnp.sum(qe * kn, axis=-1, keepdims=True) + bs_ref[0][:, :1]
        m = jnp.maximum(jnp.max(s, axis=-1, keepdims=True), sn)
        e = jnp.exp(s - m)
        en = jnp.exp(sn - m)
        l = jnp.sum(e, axis=-1, keepdims=True) + en
        outs.append((_dot_nt(e.astype(BF16), vt.astype(BF16)) + en * vn) / l)
        lses.append(m + jnp.log(l))
        u_ref[0, 0] = _shift_in(kt, m_ref[0, 0])
        u_ref[0, 1] = _shift_in(vt, m_ref[0, 1])
    mx = jnp.maximum(jnp.maximum(lses[0], lses[1]), lses[2])
    es = [jnp.exp(l - mx) for l in lses]
    tot = es[0] + es[1] + es[2]
    merged = (es[0] / tot) * outs[0] + (es[1] / tot) * outs[1] + (es[2] / tot) * outs[2]
    o_ref[0] = jnp.sum(jnp.where(diag, merged, 0.0), axis=0, keepdims=True)


def _samp_a(qs, kvs, caches_t, rel_bias):
    bsz, d = qs[0].shape
    hw = A_HEAD_CHUNK * HEAD_DIM
    nhc = N_HEADS // A_HEAD_CHUNK
    pad_rows = lambda a: jnp.pad(a.reshape(nhc, A_HEAD_CHUNK, -1), ((0, 0), (0, 8 - A_HEAD_CHUNK), (0, 0)))
    biases = []
    for window, dil in DIL_PATTERNS:
        dist = window - np.arange(window)
        biases.append(pad_rows(_bias_from_dist(rel_bias, dist, dist % dil == 0)))
    bself = pad_rows(_bias_from_dist(rel_bias, np.zeros((LANES,), np.int64), np.ones((LANES,), bool)))
    q_spec = pl.BlockSpec((1, 1, hw), lambda b, c: (b, 0, c))
    n_spec = pl.BlockSpec((1, 2, hw), lambda b, c: (b, 0, c))
    m_spec = pl.BlockSpec((1, 2, hw, 1), lambda b, c: (b, 0, c, 0))
    c_spec = lambda w: pl.BlockSpec((1, 2, hw, w), lambda b, c: (b, 0, c, 0))
    b_spec = lambda w: pl.BlockSpec((1, 8, w), lambda b, c: (c, 0, 0))
    windows = [w for w, _ in DIL_PATTERNS]
    res = pl.pallas_call(
        _samp_a_kernel,
        grid=(bsz, nhc),
        in_specs=[q_spec] * 3 + [n_spec] * 3 + [m_spec] * 3 + [c_spec(w) for w in windows]
        + [b_spec(w) for w in windows] + [b_spec(LANES)],
        out_specs=[q_spec] + [c_spec(w) for w in windows],
        out_shape=[jax.ShapeDtypeStruct((bsz, 1, d), F32)] + [jax.ShapeDtypeStruct(c.shape, F32) for c in caches_t],
        compiler_params=_cparams("parallel", "parallel"),
        name="sample_dilated_attn",
    )(*[q.reshape(bsz, 1, d) for q in qs], *[kv.reshape(bsz, 2, d) for kv in kvs],
      *[kv.reshape(bsz, 2, d, 1) for kv in kvs], *caches_t, *biases, bself)
    return res[0].reshape(bsz, d), res[1:]


def _cmp_weights(phi_pe, phi_w1, phi_w2):
    r = CMP_LEN // CMP_STRIDE
    eye = jnp.eye(N_GROUPS, dtype=F32)
    w1r = phi_w1.reshape(2, r, CMP_STRIDE, HEAD_DIM, CMP_HIDDEN)
    w1bd = jnp.einsum("kpsdh,gG->kpsgdGh", w1r, eye).reshape(
        2, r, CMP_STRIDE, N_GROUPS * HEAD_DIM, N_GROUPS * CMP_HIDDEN).astype(BF16)
    w2bd = jnp.einsum("khd,gG->kghGd", phi_w2, eye).reshape(
        2, N_GROUPS * CMP_HIDDEN, N_GROUPS * HEAD_DIM).astype(BF16)
    w2ex = jnp.einsum("khd,gG,r->kghGrd", phi_w2, eye, jnp.ones((GROUP_HEADS,), F32)).reshape(
        2, N_GROUPS * CMP_HIDDEN, D_MODEL).astype(BF16)
    pe = jnp.tile(phi_pe.reshape(2, r, CMP_STRIDE, 1, HEAD_DIM), (1, 1, 1, N_GROUPS, 1)).reshape(
        2, r, CMP_STRIDE, N_GROUPS * HEAD_DIM)
    return w1bd, w2bd, w2ex, pe


def _cmp_prompt_kernel(x_ref, w1_ref, pe_ref, w2_ref, o_ref, *, kind, transposed):
    nch = x_ref.shape[1]
    gw = N_GROUPS * HEAD_DIM
    hid = []
    for part in range(CMP_LEN // CMP_STRIDE):
        acc = jnp.zeros((nch, N_GROUPS * CMP_HIDDEN), F32)
        for s in range(CMP_STRIDE):
            c0 = s * 4 * gw + kind * gw
            x = (x_ref[0, :, c0:c0 + gw] + pe_ref[part, s:s + 1, :]).astype(BF16)
            acc = acc + _dot(x, w1_ref[part, s])
        hid.append(acc)
    h = hid[0] + pltpu.roll(hid[1], nch - 1, 0)
    h = jnp.maximum(h, 0.0).astype(BF16)
    o_ref[0] = _dot_nt(w2_ref[...], h) if transposed else _dot(h, w2_ref[...])


def _cmp_prompt(kvp, bsz, seq, kind, w1bd, pe, w2bd, transposed):
    nch = seq // CMP_STRIDE
    xv = kvp.reshape(bsz, nch, CMP_STRIDE * kvp.shape[1])
    full = lambda a: pl.BlockSpec(a.shape, lambda b: (0,) * a.ndim)
    w1, p, w2 = w1bd[kind], pe[kind], w2bd[kind]
    out_dims = (w2.shape[1], nch) if transposed else (nch, w2.shape[1])
    if transposed:
        w2 = w2.T
    return pl.pallas_call(
        functools.partial(_cmp_prompt_kernel, kind=kind, transposed=transposed),
        grid=(bsz,),
        in_specs=[pl.BlockSpec((1, nch, xv.shape[2]), lambda b: (b, 0, 0)), full(w1), full(p), full(w2)],
        out_specs=pl.BlockSpec((1,) + out_dims, lambda b: (b, 0, 0)),
        out_shape=jax.ShapeDtypeStruct((bsz,) + out_dims, F32),
        compiler_params=_cparams("parallel"),
        name="compress_prompt",
    )(xv, w1, p, w2)


def _masked_softmax(s, valid, axis=-1, exp=jnp.exp):
    m = jnp.max(s, axis=axis, keepdims=True)
    m = jnp.where(m > 0.5 * NEG, m, 0.0)
    e = jnp.where(valid, exp(s - m), 0.0)
    return e / jnp.maximum(jnp.sum(e, axis=axis, keepdims=True), 1e-30)


def _topk_mask_t(score, j, n_iter):
    rank = jnp.zeros(score.shape, F32)
    for i in range(n_iter):
        si = score[i:i + 1, :]
        beats = jnp.logical_or(si > score, jnp.logical_and(si == score, j > i))
        rank = rank + jnp.where(beats, 1.0, 0.0)
    return jnp.where(rank < SEL_TOPK, 1.0, 0.0)


def _nsa_kernel(q_ref, gate_ref, kc_ref, vct_ref, ks_ref, vs_ref, kw_ref, vw_ref, bct_ref, ut_ref, uwt_ref,
                a_ref, ebt_ref, o_ref, ksb, vst, kwb, vwt, selb, m_ref, l_ref, acc_ref, *, seq, tq, tc):
    g = pl.program_id(0)
    qi = pl.program_id(2)
    q0 = qi * tq
    par = g % 2
    nb = seq // SEL_BLOCK
    lane = lax.broadcasted_iota(jnp.int32, (tq, LANES), 1)
    keep = (lane // HEAD_DIM) == par

    @pl.when(qi == 0)
    def _():
        ksb[...] = ks_ref[0].astype(BF16)
        kwb[...] = kw_ref[0].astype(BF16)
        for c in range(seq // (2 * LANES)):
            sl = slice(c * 2 * LANES, (c + 1) * 2 * LANES)
            vst[:, sl] = vs_ref[0, sl, :].T.astype(BF16)
            vwt[:, sl] = vw_ref[0, sl, :].T.astype(BF16)

    qs = []
    for r in range(GROUP_HEADS):
        qh = q_ref[0, :, (r // 2) * LANES:(r // 2 + 1) * LANES] * (ATTN_SCALE * LOG2E)
        qh = jnp.where(par == (r % 2), qh, pltpu.roll(qh, HEAD_DIM, 1))
        qs.append(jnp.where(keep, qh, 0.0).astype(BF16))
    q4 = jnp.concatenate(qs, axis=0)

    bct = bct_ref[0, 0]
    pc = _masked_softmax(_dot_nt(kc_ref[0].astype(BF16), q4) + bct, bct > 0.5 * NEG, axis=0, exp=jnp.exp2)
    oc = _dot(vct_ref[0].astype(BF16), pc.astype(BF16))
    imp = pc[:, 0:tq] + pc[:, tq:2 * tq] + pc[:, 2 * tq:3 * tq] + pc[:, 3 * tq:4 * tq]

    hi = imp.astype(BF16)
    lo = (imp - hi.astype(F32)).astype(BF16)
    imp_t = _dot(a_ref[...], hi) + _dot(a_ref[...], lo)
    j = lax.broadcasted_iota(jnp.int32, (nb, tq), 0)
    qpos = q0 + lax.broadcasted_iota(jnp.int32, (nb, tq), 1)
    cur = qpos // SEL_BLOCK
    forced = jnp.logical_or(j == 0, jnp.logical_or(j == cur, j == cur - 1))
    score = jnp.where(forced, SEL_FORCE, jnp.where(j * SEL_BLOCK <= qpos, imp_t, -1.0))
    sel_t = _topk_mask_t(score, j, nb).astype(BF16)
    sb = (_dot(ebt_ref[...], sel_t) - 1.0) * (-NEG)
    selb[...] = sb

    m_ref[...] = jnp.full(m_ref.shape, NEG, F32)
    l_ref[...] = jnp.zeros(l_ref.shape, F32)
    acc_ref[...] = jnp.zeros(acc_ref.shape, F32)

    def sel_chunk(k0, size):
        k0 = pl.multiple_of(k0, size)
        u0 = pl.multiple_of(k0 - q0 + (seq - tq), tq)
        sbk = selb[pl.ds(k0, size), :]
        s = (_dot_nt(ksb[pl.ds(k0, size), :], q4) + ut_ref[0, pl.ds(u0, size), :]
             + jnp.concatenate([sbk] * GROUP_HEADS, axis=1))
        m_prev = m_ref[...]
        m_new = jnp.maximum(m_prev, jnp.max(s, axis=0, keepdims=True))
        alpha = jnp.exp2(m_prev - m_new)
        e = jnp.exp2(s - m_new)
        l_ref[...] = alpha * l_ref[...] + jnp.sum(e, axis=0, keepdims=True)
        acc_ref[...] = alpha * acc_ref[...] + _dot(vst[:, pl.ds(k0, size)], e.astype(BF16))
        m_ref[...] = m_new

    def full_chunk(t, carry):
        sel_chunk(t * tc, tc)
        return carry

    halves = (q0 + tq + tc // 2 - 1) // (tc // 2)
    lax.fori_loop(0, halves // 2, full_chunk, 0)

    @pl.when(halves % 2 == 1)
    def _():
        sel_chunk((halves // 2) * tc, tc // 2)

    osel = acc_ref[...] / l_ref[...]

    ww = SLIDE_WIN + tq
    w0 = pl.multiple_of(jnp.maximum(q0 - SLIDE_WIN, 0), tq)
    uw0 = pl.multiple_of(w0 - (q0 - SLIDE_WIN), tq)
    s = _dot_nt(kwb[pl.ds(w0, ww), :], q4) + uwt_ref[0, pl.ds(uw0, ww), :]
    e = jnp.exp2(s - jnp.max(s, axis=0, keepdims=True))
    owin = _dot(vwt[:, pl.ds(w0, ww)], e.astype(BF16)) / jnp.sum(e, axis=0, keepdims=True)

    gt = gate_ref[0].T
    rows = []
    for r in range(GROUP_HEADS):
        cs = slice(r * tq, (r + 1) * tq)
        o = (gt[3 * r:3 * r + 1, :] * oc[:, cs] + gt[3 * r + 1:3 * r + 2, :] * osel[:, cs]
             + gt[3 * r + 2:3 * r + 3, :] * owin[:, cs])
        rows.append(jnp.where(par == 0, o[:HEAD_DIM], o[HEAD_DIM:]))
    o_ref[0] = jnp.concatenate(rows, axis=0).T


def _imp_to_block_matrix(n_cmp_pad, n_cmp, n_blk):
    ratio = SEL_BLOCK // CMP_STRIDE
    span = CMP_LEN // CMP_STRIDE
    a = np.zeros((n_blk, n_cmp_pad), np.float32)
    for m in range(ratio):
        for n in range(span):
            for jb in range(n_blk):
                c = ratio * jb + m - n
                if 0 <= c < n_cmp:
                    a[jb, c] += 1.0
    return a


def _heads_to_lanes(t, lead):
    nl = len(lead)
    rows, tq = t.shape[-2:]
    t = t.reshape((N_GROUPS, GROUP_HEADS) + tuple(lead) + (rows, tq))
    t = jnp.transpose(t, (0,) + tuple(range(2, 2 + nl)) + (2 + nl, 1, 3 + nl))
    return t.reshape((N_GROUPS,) + tuple(lead) + (rows, GROUP_HEADS * tq))


def _nsa_prompt(q, gate, kc, vct, kvp, kvw, rel_bias, bsz, seq):
    tq, tc = LANES, 8 * LANES
    d = D_MODEL
    nb = seq // SEL_BLOCK
    ncp = seq // CMP_STRIDE
    nqt = seq // tq
    n_cmp = (seq - CMP_LEN) // CMP_STRIDE + 1
    nq = GROUP_HEADS * tq
    assert ncp == LANES and SLIDE_WIN % tq == 0
    qv = q.reshape(bsz, seq, d)
    gv = gate.reshape(bsz, seq, N_GROUPS * LANES)
    kvpv = kvp.reshape(bsz, seq, kvp.shape[1])
    kvwv = kvw.reshape(bsz, seq, kvw.shape[1])

    sub = np.arange(CMP_STRIDE)[:, None]
    x = (np.arange(2 * ncp) - (ncp - 1))[None, :]
    dist_c = -CMP_STRIDE * x + sub - (CMP_LEN - 1)
    bias_c = _toeplitz(_bias_from_dist(rel_bias, dist_c, dist_c >= 0), ncp, ncp)
    bias_c = jnp.where(jnp.asarray(np.arange(ncp) < n_cmp), bias_c, NEG)
    bias_c = jnp.transpose(bias_c, (0, 3, 2, 1)).reshape(N_HEADS, ncp, nqt, tq)
    bias_ct = _heads_to_lanes(jnp.transpose(bias_c, (0, 2, 1, 3)), (nqt,))
    wu = seq - tq + tc
    strip = _toeplitz_bias(rel_bias, tq, wu, lambda x: (seq - tq) - x, lambda x: (seq - tq) - x >= 0)
    strip_t = _heads_to_lanes(jnp.swapaxes(strip, 1, 2), ())
    ww = 2 * SLIDE_WIN + tq
    strip_w = _toeplitz_bias(rel_bias, tq, ww, lambda x: SLIDE_WIN - x,
                             lambda x: (SLIDE_WIN - x >= 0) & (SLIDE_WIN - x < SLIDE_WIN))
    strip_wt = _heads_to_lanes(jnp.swapaxes(strip_w, 1, 2), ())

    bias_ct, strip_t, strip_wt = bias_ct * LOG2E, strip_t * LOG2E, strip_wt * LOG2E

    a_m = jnp.asarray(_imp_to_block_matrix(ncp, n_cmp, nb), BF16)
    e_bt = jnp.asarray(np.kron(np.eye(nb), np.ones((SEL_BLOCK, 1))), BF16)

    half = lambda g: g // 2
    kv_spec = lambda off: pl.BlockSpec((1, seq, LANES), lambda g, b, qi: (b, 0, off + half(g)))
    full = lambda a: pl.BlockSpec(a.shape, lambda g, b, qi: (0,) * a.ndim)
    out = pl.pallas_call(
        functools.partial(_nsa_kernel, seq=seq, tq=tq, tc=tc),
        grid=(N_GROUPS, bsz, nqt),
        in_specs=[pl.BlockSpec((1, tq, GROUP_HEADS * HEAD_DIM), lambda g, b, qi: (b, qi, g)),
                  pl.BlockSpec((1, tq, LANES), lambda g, b, qi: (b, qi, g)),
                  pl.BlockSpec((1, ncp, LANES), lambda g, b, qi: (b, 0, half(g))),
                  pl.BlockSpec((1, LANES, ncp), lambda g, b, qi: (b, half(g), 0)),
                  kv_spec(4), kv_spec(6), kv_spec(0), kv_spec(2),
                  pl.BlockSpec((1, 1, ncp, nq), lambda g, b, qi: (g, qi, 0, 0)),
                  pl.BlockSpec((1, wu, nq), lambda g, b, qi: (g, 0, 0)),
                  pl.BlockSpec((1, ww, nq), lambda g, b, qi: (g, 0, 0)),
                  full(a_m), full(e_bt)],
        out_specs=pl.BlockSpec((1, tq, GROUP_HEADS * HEAD_DIM), lambda g, b, qi: (b, qi, g)),
        out_shape=jax.ShapeDtypeStruct((bsz, seq, d), F32),
        scratch_shapes=[pltpu.VMEM((seq, LANES), BF16), pltpu.VMEM((LANES, seq), BF16),
                        pltpu.VMEM((seq, LANES), BF16), pltpu.VMEM((LANES, seq), BF16),
                        pltpu.VMEM((seq, tq), F32),
                        pltpu.VMEM((1, nq), F32), pltpu.VMEM((1, nq), F32), pltpu.VMEM((LANES, nq), F32)],
        compiler_params=_cparams("parallel", "parallel", "arbitrary"),
        name="nsa_prompt_attn",
    )(qv, gv, kc, vct, kvpv, kvpv, kvwv, kvwv, bias_ct, strip_t, strip_wt, a_m, e_bt)
    return out.reshape(bsz * seq, d)


CMP_PAGES = 16


def _scmp_kernel(pt_ref, *refs):
    del pt_ref
    pages = refs[:CMP_PAGES]
    w1_ref, pe_ref, o_ref, t_scr, x_scr = refs[CMP_PAGES:]
    cpp = pages[0].shape[3] // CMP_STRIDE
    nh = N_GROUPS * CMP_HIDDEN
    for kind in range(2):
        for i, p_ref in enumerate(pages):
            x = p_ref[0, kind].T
            for hh in range(x.shape[1] // LANES):
                t_scr[i, hh] = x[:, hh * LANES:(hh + 1) * LANES]
            for s in range(CMP_STRIDE):
                for hh in range(x.shape[1] // LANES):
                    x_scr[s, i * cpp:(i + 1) * cpp, hh * LANES:(hh + 1) * LANES] = (
                        t_scr[i, hh, pl.ds(s, cpp, stride=CMP_STRIDE), :])
        for part in range(CMP_LEN // CMP_STRIDE):
            acc = jnp.zeros((x_scr.shape[1], nh), F32)
            for s in range(CMP_STRIDE):
                x = (x_scr[s] + pe_ref[kind, part, s:s + 1, :]).astype(BF16)
                acc = acc + _dot(x, w1_ref[kind, part, s])
            o_ref[kind, 0, :, part * nh:(part + 1) * nh] = acc


def _scmp(cache, page_table, w1bd, pe):
    bsz, n_pages = page_table.shape
    _, _, gw, page = cache.shape
    cpp = page // CMP_STRIDE
    nh = N_GROUPS * CMP_HIDDEN
    n_chunks = n_pages * cpp
    page_spec = lambda i: pl.BlockSpec((1, 2, gw, page), lambda b, j, pt: (pt[b, j * CMP_PAGES + i], 0, 0, 0))
    const = lambda a: pl.BlockSpec(a.shape, lambda b, j, pt: (0,) * a.ndim, pipeline_mode=pl.Buffered(1))
    grid_spec = pltpu.PrefetchScalarGridSpec(
        num_scalar_prefetch=1,
        grid=(bsz, n_pages // CMP_PAGES),
        in_specs=[page_spec(i) for i in range(CMP_PAGES)] + [const(w1bd), const(pe)],
        out_specs=pl.BlockSpec((2, 1, CMP_PAGES * cpp, 2 * nh), lambda b, j, pt: (0, b, j, 0)),
        scratch_shapes=[pltpu.VMEM((CMP_PAGES, gw // LANES, page, LANES), F32),
                        pltpu.VMEM((CMP_STRIDE, CMP_PAGES * cpp, gw), F32)])
    return pl.pallas_call(
        _scmp_kernel,
        grid_spec=grid_spec,
        out_shape=jax.ShapeDtypeStruct((2, bsz, n_chunks, 2 * nh), F32),
        compiler_params=_cparams("parallel", "arbitrary"),
        name="compress_sample",
    )(page_table, *([cache] * CMP_PAGES), w1bd, pe)


def _s1_kernel(ab_ref, w2_ref, q_ref, bc_ref, at_ref, gs_ref, oc_ref, it_ref):
    nck = ab_ref.shape[2]
    nh = N_GROUPS * CMP_HIDDEN
    kv = []
    for kind in range(2):
        h = ab_ref[kind, 0, :, :nh] + pltpu.roll(ab_ref[kind, 0, :, nh:], nck - 1, 0)
        kv.append(_dot(jnp.maximum(h, 0.0).astype(BF16), w2_ref[kind]).astype(BF16))
    diag = _head_diag(N_HEADS, D_MODEL)
    qe = jnp.where(diag, q_ref[0] * ATTN_SCALE, 0.0).astype(BF16)
    bc = bc_ref[...]
    pc = _masked_softmax(_dot_nt(qe, kv[0]) + bc, bc > 0.5 * NEG)
    o = _dot(pc.astype(BF16), kv[1])
    oc_ref[0] = jnp.sum(jnp.where(diag, o, 0.0), axis=0, keepdims=True)
    hi = pc.astype(BF16)
    lo = (pc - hi.astype(F32)).astype(BF16)
    imp = _dot(gs_ref[...], hi) + _dot(gs_ref[...], lo)
    it_ref[0] = _split_dot_nt(at_ref[...], imp)


def _s1(ab, w2ex, q, rel_bias, past):
    _, bsz, nck, _ = ab.shape
    d = D_MODEL
    n_cmp = (past + 1 - CMP_LEN) // CMP_STRIDE + 1
    n_blk = -(-(past + 1) // SEL_BLOCK)
    nbp = -(-n_blk // 8) * 8
    n = np.arange(nck)
    dist_c = past - (n * CMP_STRIDE + CMP_LEN - 1)
    bias_c = _bias_from_dist(rel_bias, dist_c, (dist_c >= 0) & (n < n_cmp))
    a_t = jnp.asarray(_imp_to_block_matrix(nck, n_cmp, nbp)[:nbp] * (np.arange(nbp)[:, None] < n_blk), BF16)
    gsum = jnp.asarray(np.kron(np.eye(8, N_GROUPS), np.ones((1, GROUP_HEADS))), BF16)
    full = lambda a: pl.BlockSpec(a.shape, lambda b: (0,) * a.ndim)
    oc, imp_t = pl.pallas_call(
        _s1_kernel,
        grid=(bsz,),
        in_specs=[pl.BlockSpec((2, 1, nck, ab.shape[3]), lambda b: (0, b, 0, 0)), full(w2ex),
                  pl.BlockSpec((1, 1, d), lambda b: (b, 0, 0)), full(bias_c), full(a_t), full(gsum)],
        out_specs=[pl.BlockSpec((1, 1, d), lambda b: (b, 0, 0)), pl.BlockSpec((1, nbp, 8), lambda b: (b, 0, 0))],
        out_shape=[jax.ShapeDtypeStruct((bsz, 1, d), F32), jax.ShapeDtypeStruct((bsz, nbp, 8), F32)],
        compiler_params=_cparams("parallel"),
        name="sample_compressed_attn",
    )(ab, w2ex, q.reshape(bsz, 1, d), bias_c, a_t, gsum)
    return oc.reshape(bsz, d), imp_t


def _s2_kernel(imp_ref, idx_ref, s_scr, *, n_blk, cur):
    shape = imp_ref.shape
    j = lax.broadcasted_iota(jnp.int32, shape, 0)
    forced = jnp.logical_or(j == 0, jnp.logical_or(j == cur, j == cur - 1))
    score = jnp.where(forced, SEL_FORCE, jnp.where(j < n_blk, imp_ref[...], -2.0))
    s_scr[...] = score

    def body(i, rank):
        si = s_scr[pl.ds(i, 1), :]
        beats = jnp.logical_or(si > score, jnp.logical_and(si == score, j > i))
        return rank + jnp.where(beats, 1.0, 0.0)

    rank = lax.fori_loop(0, n_blk, body, jnp.zeros(shape, F32))
    for k in range(SEL_TOPK):
        idx_ref[k:k + 1, :] = jnp.sum(jnp.where(rank == k, j, 0), axis=0, keepdims=True)


def _s2(imp_t, past):
    n_blk = -(-(past + 1) // SEL_BLOCK)
    return pl.pallas_call(
        functools.partial(_s2_kernel, n_blk=n_blk, cur=past // SEL_BLOCK),
        out_shape=jax.ShapeDtypeStruct((SEL_TOPK, imp_t.shape[1]), jnp.int32),
        scratch_shapes=[pltpu.VMEM(imp_t.shape, F32)],
        name="sample_block_topk",
    )(imp_t)


def _bucket_thresholds(max_dist):
    b = _bucket_np(np.arange(max_dist + 1))
    return [int(np.argmax(b >= k)) if (b >= k).any() else max_dist + 1 for k in range(REL_BUCKETS)]


def _attend_t(q, kt, vt, bias, valid, kn, vn, bias_new):
    s = _dot(q.astype(BF16), kt.astype(BF16)) + bias
    if valid is not None:
        s = jnp.where(valid, s, NEG)
    sn = jnp.sum(q * kn, axis=-1, keepdims=True) + bias_new
    m = jnp.maximum(jnp.max(s, axis=-1, keepdims=True), sn)
    e = jnp.exp(s - m)
    en = jnp.exp(sn - m)
    l = jnp.sum(e, axis=-1, keepdims=True) + en
    return (_dot_nt(e.astype(BF16), vt.astype(BF16)) + en * vn) / l


def _ssel_kernel(idx_ref, page_ref, *refs, past, thresholds):
    del page_ref
    blocks = refs[:SEL_TOPK]
    q_ref, n_ref, tb_ref, o_ref = refs[SEL_TOPK:]
    b = pl.program_id(0)
    g = pl.program_id(1)
    page = blocks[0].shape[3]
    kt = jnp.concatenate([blk[0, 0] for blk in blocks], axis=1)
    vt = jnp.concatenate([blk[0, 1] for blk in blocks], axis=1)
    nk = SEL_TOPK * page
    lane = lax.broadcasted_iota(jnp.int32, (1, nk), 1)
    row = lane % page
    pos = jnp.zeros((1, nk), jnp.int32)
    inside = jnp.zeros((1, nk), jnp.int32)
    bpp = page // SEL_BLOCK
    for k in range(SEL_TOPK):
        blk_idx = idx_ref[(b * N_GROUPS + g) * SEL_TOPK + k]
        mine = lane // page == k
        pos = jnp.where(mine, (blk_idx // bpp) * page + row, pos)
        inside = jnp.where(mine, jnp.where(row // SEL_BLOCK == blk_idx % bpp, 1, 0), inside)
    valid = jnp.logical_and(inside == 1, pos < past)
    dist = past - pos
    bias = jnp.broadcast_to(tb_ref[0, 0][:, :1], (8, nk))
    for k in range(1, REL_BUCKETS):
        bias = jnp.where(dist >= thresholds[k], tb_ref[0, k][:, :1], bias)
    q = q_ref[0, 0] * ATTN_SCALE
    o_ref[0, 0] = _attend_t(q, kt, vt, bias, valid, n_ref[0, 0, 0], n_ref[0, 1, 0], tb_ref[0, 0][:, :1])


def _ssel(sel_idx, cache_t, page_table, q, kvp_new, rel_bias, past):
    bsz = q.shape[0]
    page = cache_t.shape[3]
    bpp = page // SEL_BLOCK
    cur = past // SEL_BLOCK
    cached = jnp.minimum(sel_idx, cur - 1)
    pages = jnp.take_along_axis(page_table, (cached // bpp).reshape(bsz, -1), axis=1).reshape(-1)
    q4 = jnp.pad(q.reshape(bsz, N_GROUPS, GROUP_HEADS, HEAD_DIM), ((0, 0), (0, 0), (0, 8 - GROUP_HEADS), (0, 0)))
    new = kvp_new.reshape(bsz, 4, N_GROUPS, 1, HEAD_DIM)
    tb = jnp.transpose(rel_bias.astype(F32).reshape(REL_BUCKETS, N_GROUPS, GROUP_HEADS), (1, 0, 2))
    tb = jnp.pad(tb, ((0, 0), (0, 0), (0, 8 - GROUP_HEADS)))
    tb = jnp.broadcast_to(tb[..., None], tb.shape + (LANES,))
    blk_spec = lambda k: pl.BlockSpec(
        (1, 2, HEAD_DIM, page), lambda b, g, idx, pg: (pg[(b * N_GROUPS + g) * SEL_TOPK + k], 1, g, 0))
    grid_spec = pltpu.PrefetchScalarGridSpec(
        num_scalar_prefetch=2,
        grid=(bsz, N_GROUPS),
        in_specs=[blk_spec(k) for k in range(SEL_TOPK)] + [
            pl.BlockSpec((1, 1, 8, HEAD_DIM), lambda b, g, idx, pg: (b, g, 0, 0)),
            pl.BlockSpec((1, 2, 1, 1, HEAD_DIM), lambda b, g, idx, pg: (b, 1, g, 0, 0)),
            pl.BlockSpec((1, REL_BUCKETS, 8, LANES), lambda b, g, idx, pg: (g, 0, 0, 0))],
        out_specs=pl.BlockSpec((1, 1, 8, HEAD_DIM), lambda b, g, idx, pg: (b, g, 0, 0)))
    out = pl.pallas_call(
        functools.partial(_ssel_kernel, past=past, thresholds=tuple(_bucket_thresholds(past))),
        grid_spec=grid_spec,
        out_shape=jax.ShapeDtypeStruct((bsz, N_GROUPS, 8, HEAD_DIM), F32),
        compiler_params=_cparams("parallel", "parallel"),
        name="sample_selected_attn",
    )(sel_idx.reshape(-1), pages, *([cache_t] * SEL_TOPK), q4, new, tb)
    return out[:, :, :GROUP_HEADS].reshape(bsz, N_HEADS, HEAD_DIM)


def _swin_kernel(c_ref, q_ref, n_ref, m_ref, bw_ref, b0_ref, o_ref, u_ref):
    kt = c_ref[0, 0]
    vt = c_ref[0, 1]
    q = q_ref[0] * ATTN_SCALE
    o = _attend_t(q, kt, vt, bw_ref[...], None, n_ref[0, 0:1, :], n_ref[0, 1:2, :], b0_ref[...][:, :1])
    grp = lax.broadcasted_iota(jnp.int32, (N_HEADS, HEAD_DIM), 0) // GROUP_HEADS
    out = jnp.zeros((N_HEADS, HEAD_DIM), F32)
    for g in range(N_GROUPS):
        out = jnp.where(grp == g, o[:, g * HEAD_DIM:(g + 1) * HEAD_DIM], out)
    o_ref[0] = out
    u_ref[0, 0] = _shift_in(kt, m_ref[0, 0])
    u_ref[0, 1] = _shift_in(vt, m_ref[0, 1])


def _swin(cwin_t, q, kvw_new, rel_bias):
    bsz = q.shape[0]
    _, _, gw, wlen = cwin_t.shape
    onehot = jnp.asarray(np.kron(np.eye(N_GROUPS), np.ones((GROUP_HEADS, 1))), F32)
    qbd = jnp.einsum("bhd,hg->bhgd", q.reshape(bsz, N_HEADS, HEAD_DIM), onehot).reshape(bsz, N_HEADS, gw)
    dist_w = wlen - np.arange(wlen)
    bw = _bias_from_dist(rel_bias, dist_w, dist_w < SLIDE_WIN)
    b0 = _bias_from_dist(rel_bias, np.zeros((LANES,), np.int64), np.ones((LANES,), bool))
    full = lambda a: pl.BlockSpec(a.shape, lambda b: (0,) * a.ndim)
    c_spec = pl.BlockSpec((1, 2, gw, wlen), lambda b: (b, 0, 0, 0))
    o_spec = pl.BlockSpec((1, N_HEADS, HEAD_DIM), lambda b: (b, 0, 0))
    return pl.pallas_call(
        _swin_kernel,
        grid=(bsz,),
        in_specs=[c_spec, pl.BlockSpec((1, N_HEADS, gw), lambda b: (b, 0, 0)),
                  pl.BlockSpec((1, 2, gw), lambda b: (b, 0, 0)),
                  pl.BlockSpec((1, 2, gw, 1), lambda b: (b, 0, 0, 0)), full(bw), full(b0)],
        out_specs=[o_spec, c_spec],
        out_shape=[jax.ShapeDtypeStruct((bsz, N_HEADS, HEAD_DIM), F32), jax.ShapeDtypeStruct(cwin_t.shape, F32)],
        compiler_params=_cparams("parallel"),
        name="sample_window_attn",
    )(cwin_t, qbd, kvw_new.reshape(bsz, 2, gw), kvw_new.reshape(bsz, 2, gw, 1), bw, b0)


def _gate_sum_kernel(g_ref, oc_ref, os_ref, ow_ref, o_ref):
    g = g_ref[...]
    o_ref[...] = g[:, :, 0:1] * oc_ref[...] + g[:, :, 1:2] * os_ref[...] + g[:, :, 2:3] * ow_ref[...]


def _gate_sum(gate, oc, os_, ow):
    return pl.pallas_call(
        _gate_sum_kernel,
        out_shape=jax.ShapeDtypeStruct(oc.shape, F32),
        name="sample_gate_sum",
    )(gate, oc, os_, ow)


def _split_a_weights(w_in):
    d = D_MODEL
    w = w_in.astype(BF16)
    qs = [w[:, g * 3 * d:g * 3 * d + d] for g in range(len(DIL_PATTERNS))]
    kvs = [w[:, g * 3 * d + d:(g + 1) * 3 * d] for g in range(len(DIL_PATTERNS))]
    return qs + kvs


def _split_b_weights(w_in):
    d = D_MODEL
    gw = N_GROUPS * HEAD_DIM
    w = w_in.astype(BF16)
    wg = w[:, d + 6 * gw:].reshape(d, N_GROUPS, GROUP_HEADS * 3)
    wg = jnp.pad(wg, ((0, 0), (0, 0), (0, LANES - GROUP_HEADS * 3))).reshape(d, N_GROUPS * LANES)
    return [w[:, :d], w[:, d:d + 4 * gw], w[:, d + 4 * gw:d + 6 * gw], wg]


def _unblock_gate(gate):
    m = gate.shape[0]
    return gate.reshape(m, N_GROUPS, LANES)[:, :, :GROUP_HEADS * 3].reshape(m, N_HEADS * 3)


def kernel(x_prompt, x_sample, cache_a_win0, cache_a_win1, cache_a_win2, cache_b_kv, cache_b_win, page_table,
           rel_bias, norm_mix, norm_mlp, norm_final, a_w_in, a_w_out, b_w_in, b_w_out, b_phi_pe, b_phi_w1,
           b_phi_w2, mlp_w_up, mlp_w_down):
    bsz, seq, d = x_prompt.shape
    sb = x_sample.shape[0]
    past = page_table.shape[1] * cache_b_kv.shape[2]
    tm_p, tm_s = 512, sb
    tm_w = 1024
    xp = x_prompt.reshape(bsz * seq, d)
    xs = x_sample.reshape(sb, d)
    acts6 = ("none",) * 6

    wa = _split_a_weights(a_w_in[0])
    wa_out = a_w_out[0].astype(BF16)
    full_window = [gi for gi, (w, _) in enumerate(DIL_PATTERNS) if w >= seq]
    pa = _norm_matmul(xp, norm_mix[0], wa, acts6, tm_w, 4, [wa[3 + gi].T for gi in full_window], seq)
    sa = _norm_matmul(xs, norm_mix[0], wa, acts6, tm_s, 4)
    os_, ls = [], []
    for gi, (_, dil) in enumerate(DIL_PATTERNS):
        o, lse = _swa_group(pa[gi], pa[3 + gi], rel_bias, bsz, seq, dil)
        os_.append(o)
        ls.append(lse)
    xp = _merge_proj(os_, ls, xp, wa_out, tm_p)
    a_caches = [_rows_last(c[0]) for c in (cache_a_win0, cache_a_win1, cache_a_win2)]
    oa, a_new_t = _samp_a(sa[:3], sa[3:], a_caches, rel_bias)
    xs = _matmul_res(oa, xs, wa_out, tm_s)
    cache_shape = lambda n: (1, -1, n, 2, N_HEADS, HEAD_DIM)
    a_new_p = [pa[3 + gi].reshape(bsz, seq, 2 * d)[:, seq - min(w, seq):].reshape(cache_shape(min(w, seq)))
               for gi, (w, _) in enumerate(DIL_PATTERNS)]
    for k, gi in enumerate(full_window):
        a_new_p[gi] = _rows_second(pa[6 + k].reshape(bsz, 2, d, seq), (2, N_HEADS, HEAD_DIM))[None]
    a_new_s = [_rows_second(c, (2, N_HEADS, HEAD_DIM)) for c in a_new_t]

    wu0, wd0 = mlp_w_up[0].astype(BF16), mlp_w_down[0].astype(BF16)
    xp = _mlp(xp, norm_mlp[0], wu0, wd0, norm_final, False, tm_w, 1024)
    xs = _mlp(xs, norm_mlp[0], wu0, wd0, norm_final, False, tm_s, 1024)

    wb = _split_b_weights(b_w_in[0])
    wb_out = b_w_out[0].astype(BF16)
    acts_b = ("none", "none", "none", "sigmoid")
    qp, kvp_p, kvw_p, gate_p, kvp_t = _norm_matmul(xp, norm_mix[1], wb, acts_b, tm_p, 1, [wb[1].T], seq)
    qs_, kvp_s, kvw_s, gate_s = _norm_matmul(xs, norm_mix[1], wb, acts_b, tm_s, 1)
    w1bd, w2bd, w2ex, pe = _cmp_weights(b_phi_pe[0], b_phi_w1[0], b_phi_w2[0])
    kc = _cmp_prompt(kvp_p, bsz, seq, 0, w1bd, pe, w2bd, False)
    vct = _cmp_prompt(kvp_p, bsz, seq, 1, w1bd, pe, w2bd, True)
    ob = _nsa_prompt(qp, gate_p, kc, vct, kvp_p, kvw_p, rel_bias, bsz, seq)
    xp = _matmul_res(ob, xp, wb_out, tm_p)

    cache_t = _rows_last(cache_b_kv[0])
    cwin_t = _rows_last(cache_b_win[0])
    ab = _scmp(cache_t, page_table, w1bd, pe)
    oc, imp_t = _s1(ab, w2ex, qs_, rel_bias, past)
    imp_t = jnp.transpose(imp_t[:, :, :N_GROUPS], (1, 0, 2)).reshape(imp_t.shape[1], sb * N_GROUPS)
    sel_idx = jnp.transpose(_s2(imp_t, past)).reshape(sb, N_GROUPS, SEL_TOPK)
    osel = _ssel(sel_idx, cache_t, page_table, qs_, kvp_s, rel_bias, past)
    owin, b_win_t = _swin(cwin_t, qs_, kvw_s, rel_bias)
    obs = _gate_sum(_unblock_gate(gate_s).reshape(sb, N_HEADS, 3), oc.reshape(sb, N_HEADS, HEAD_DIM), osel, owin)
    xs = _matmul_res(obs.reshape(sb, d), xs, wb_out, tm_s)
    b_win_s = _rows_second(b_win_t, (2, N_GROUPS, HEAD_DIM))

    wu1, wd1 = mlp_w_up[1].astype(BF16), mlp_w_down[1].astype(BF16)
    yp = _mlp(xp, norm_mlp[1], wu1, wd1, norm_final, True, tm_w, 1024)
    ys = _mlp(xs, norm_mlp[1], wu1, wd1, norm_final, True, tm_s, 1024)

    gw = N_GROUPS * HEAD_DIM
    keep = min(SLIDE_WIN, seq)
    outs = [yp.reshape(bsz, seq, d), ys.reshape(sb, 1, d)]
    for p, s in zip(a_new_p, a_new_s):
        outs.append(p)
        outs.append(s[None])
    outs.append(_rows_second(kvp_t.reshape(bsz, 4, N_GROUPS * HEAD_DIM, seq), (4, N_GROUPS, HEAD_DIM))[None])
    outs.append(kvp_s.reshape(1, sb, 1, 4, N_GROUPS, HEAD_DIM))
    outs.append(kvw_p.reshape(bsz, seq, 2 * gw)[:, seq - keep:].reshape(1, bsz, keep, 2, N_GROUPS, HEAD_DIM))
    outs.append(b_win_s[None])
    return tuple(outs)
```

```python
import functools
import math

import numpy as np
import jax
import jax.numpy as jnp
from jax import lax
from jax.experimental import pallas as pl
from jax.experimental.pallas import tpu as pltpu

F32 = jnp.float32
BF16 = jnp.bfloat16

D_MODEL = 1024
HEAD_DIM = 64
N_HEADS = 16
ATTN_SCALE = HEAD_DIM ** -0.5
LOG2E = math.log2(math.e)
RMS_EPS = 1e-6
REL_BUCKETS = 32
REL_MAX_DIST = 2048
DIL_PATTERNS = ((128, 1), (512, 4), (2048, 16))
A_KEYS = 128
N_GROUPS = 4
GROUP_HEADS = N_HEADS // N_GROUPS
CMP_LEN = 32
CMP_STRIDE = 16
CMP_HIDDEN = 128
SEL_BLOCK = 64
SEL_TOPK = 16
SEL_FORCE = 1e4
SLIDE_WIN = 512

LANES = 128
VMEM_LIMIT = 56 * 1024 * 1024
NEG = -1e30

NT_DIMS = (((1,), (1,)), ((), ()))


def _cparams(*sem):
    return pltpu.CompilerParams(dimension_semantics=sem, vmem_limit_bytes=VMEM_LIMIT)


def _dot(a, b):
    return jnp.dot(a, b, preferred_element_type=F32)


def _dot_nt(a, b):
    return lax.dot_general(a, b, NT_DIMS, preferred_element_type=F32)


def _split_dot(x, w):
    hi = x.astype(BF16)
    lo = (x - hi.astype(F32)).astype(BF16)
    return _dot(hi, w) + _dot(lo, w)


def _split_dot_nt(w, x):
    hi = x.astype(BF16)
    lo = (x - hi.astype(F32)).astype(BF16)
    return _dot_nt(w, hi) + _dot_nt(w, lo)


def _rms(x, g):
    ms = jnp.mean(x * x, axis=-1, keepdims=True)
    return x * lax.rsqrt(ms + RMS_EPS) * g


def _bucket_np(dist):
    dist = np.maximum(np.asarray(dist, np.int64), 0)
    max_exact = REL_BUCKETS // 2
    ratio = (np.log(np.maximum(dist, 1).astype(np.float32) / np.float32(max_exact))
             / np.float32(math.log(REL_MAX_DIST / max_exact)))
    large = np.minimum(max_exact + (ratio * (REL_BUCKETS - max_exact)).astype(np.int32), REL_BUCKETS - 1)
    return np.where(dist < max_exact, dist, large).astype(np.int32)


def _bias_from_dist(rel_bias, dist, valid):
    idx = jnp.asarray(_bucket_np(dist))
    b = jnp.take(rel_bias.astype(F32).T, idx, axis=1)
    return jnp.where(jnp.asarray(valid)[None], b, NEG)


def _toeplitz(w, n, m):
    length = n + m
    lead = w.shape[:-1]
    t = jnp.tile(w, (1,) * len(lead) + (n,))[..., :n * (length - 1)].reshape(lead + (n, length - 1))
    return t[..., n - 1:n - 1 + m]


def _toeplitz_bias(rel_bias, n, m, dist_of, valid_of):
    x = np.arange(n + m) - (n - 1)
    return _toeplitz(_bias_from_dist(rel_bias, dist_of(x), valid_of(x)), n, m)


def _norm_matmul_kernel(x_ref, g_ref, *refs, n_w, n_t, acts):
    w_refs, wt_refs = refs[:n_w], refs[n_w:n_w + n_t]
    o_refs, ot_refs = refs[n_w + n_t:2 * n_w + n_t], refs[2 * n_w + n_t:2 * (n_w + n_t)]
    xn_ref = refs[2 * (n_w + n_t)]

    @pl.when(pl.program_id(1) == 0)
    def _():
        xn_ref[...] = _rms(x_ref[...], g_ref[...]).astype(BF16)

    xn = xn_ref[...]
    for w_ref, o_ref, act in zip(w_refs, o_refs, acts):
        y = _dot(xn, w_ref[...])
        if act == "sigmoid":
            y = jax.nn.sigmoid(y)
        o_ref[...] = y.astype(o_ref.dtype)
    for wt_ref, ot_ref in zip(wt_refs, ot_refs):
        ot_ref[0] = _dot_nt(wt_ref[...], xn)


def _norm_matmul(x, g, ws, acts, tm, nj, wts=(), seq=None):
    m, d = x.shape
    kern = functools.partial(_norm_matmul_kernel, n_w=len(ws), n_t=len(wts), acts=tuple(acts))
    in_specs = [pl.BlockSpec((tm, d), lambda i, j: (i, 0)), pl.BlockSpec((1, d), lambda i, j: (0, 0))]
    in_specs += [pl.BlockSpec((d, w.shape[1] // nj), lambda i, j: (0, j)) for w in ws]
    in_specs += [pl.BlockSpec((w.shape[0] // nj, d), lambda i, j: (j, 0)) for w in wts]
    out_specs = [pl.BlockSpec((tm, w.shape[1] // nj), lambda i, j: (i, j)) for w in ws]
    out_shape = [jax.ShapeDtypeStruct((m, w.shape[1]), BF16 if a == "bf16" else F32) for w, a in zip(ws, acts)]
    if wts:
        tps = seq // tm
        out_specs += [pl.BlockSpec((1, w.shape[0] // nj, tm), lambda i, j: (i // tps, j, i % tps)) for w in wts]
        out_shape += [jax.ShapeDtypeStruct((m // seq, w.shape[0], seq), F32) for w in wts]
    return pl.pallas_call(
        kern,
        grid=(m // tm, nj),
        in_specs=in_specs,
        out_specs=out_specs,
        out_shape=out_shape,
        scratch_shapes=[pltpu.VMEM((tm, d), BF16)],
        compiler_params=_cparams("parallel", "arbitrary"),
        name="norm_matmul",
    )(x, g.reshape(1, d), *ws, *wts)


def _matmul_res_kernel(a_ref, x_ref, w_ref, o_ref):
    o_ref[...] = x_ref[...] + _dot(a_ref[...].astype(BF16), w_ref[...])


def _matmul_res(a, x, w, tm):
    m, d = x.shape
    return pl.pallas_call(
        _matmul_res_kernel,
        grid=(m // tm,),
        in_specs=[pl.BlockSpec((tm, a.shape[1]), lambda i: (i, 0)),
                  pl.BlockSpec((tm, d), lambda i: (i, 0)),
                  pl.BlockSpec(w.shape, lambda i: (0, 0))],
        out_specs=pl.BlockSpec((tm, d), lambda i: (i, 0)),
        out_shape=jax.ShapeDtypeStruct((m, d), F32),
        compiler_params=_cparams("parallel"),
        name="matmul_res",
    )(a, x, w)


def _merge_proj_kernel(o0, o1, o2, l0, l1, l2, e_ref, x_ref, w_ref, out_ref):
    ls = [l0[...], l1[...], l2[...]]
    mx = jnp.maximum(jnp.maximum(ls[0], ls[1]), ls[2])
    es = [jnp.exp(l - mx) for l in ls]
    tot = es[0] + es[1] + es[2]
    merged = None
    for e, o in zip(es, (o0, o1, o2)):
        part = _split_dot(e / tot, e_ref[...]) * o[...]
        merged = part if merged is None else merged + part
    out_ref[...] = x_ref[...] + _dot(merged.astype(BF16), w_ref[...])


def _merge_proj(os_, ls, x, w, tm):
    m, d = x.shape
    expand = jnp.asarray(np.kron(np.eye(LANES, N_HEADS), np.ones((1, HEAD_DIM))), BF16)
    row = lambda n: pl.BlockSpec((tm, n), lambda i: (i, 0))
    return pl.pallas_call(
        _merge_proj_kernel,
        grid=(m // tm,),
        in_specs=[row(d)] * 3 + [row(LANES)] * 3 + [
            pl.BlockSpec(expand.shape, lambda i: (0, 0)), row(d), pl.BlockSpec(w.shape, lambda i: (0, 0))],
        out_specs=row(d),
        out_shape=jax.ShapeDtypeStruct((m, d), F32),
        compiler_params=_cparams("parallel"),
        name="merge_proj",
    )(*os_, *ls, expand, x, w)


def _mlp_kernel(x_ref, g_ref, wu_ref, wd_ref, gf_ref, o_ref, xn_ref, acc_ref, *, final_norm):
    f = pl.program_id(1)

    @pl.when(f == 0)
    def _():
        xn_ref[...] = _rms(x_ref[...], g_ref[...]).astype(BF16)
        acc_ref[...] = jnp.zeros_like(acc_ref)

    h = jnp.square(jnp.maximum(_dot(xn_ref[...], wu_ref[...]), 0.0))
    acc_ref[...] += _dot(h.astype(BF16), wd_ref[...])

    @pl.when(f == pl.num_programs(1) - 1)
    def _():
        y = x_ref[...] + acc_ref[...]
        if final_norm:
            y = _rms(y, gf_ref[...])
        o_ref[...] = y


def _mlp(x, g, wu, wd, gf, final_norm, tm, tf):
    m, d = x.shape
    dff = wu.shape[1]
    return pl.pallas_call(
        functools.partial(_mlp_kernel, final_norm=final_norm),
        grid=(m // tm, dff // tf),
        in_specs=[pl.BlockSpec((tm, d), lambda i, f: (i, 0)),
                  pl.BlockSpec((1, d), lambda i, f: (0, 0)),
                  pl.BlockSpec((d, tf), lambda i, f: (0, f)),
                  pl.BlockSpec((tf, d), lambda i, f: (f, 0)),
                  pl.BlockSpec((1, d), lambda i, f: (0, 0))],
        out_specs=pl.BlockSpec((tm, d), lambda i, f: (i, 0)),
        out_shape=jax.ShapeDtypeStruct((m, d), F32),
        scratch_shapes=[pltpu.VMEM((tm, d), BF16), pltpu.VMEM((tm, d), F32)],
        compiler_params=_cparams("parallel", "arbitrary"),
        name="mlp",
    )(x, g.reshape(1, d), wu, wd, gf.reshape(1, d))


def _swa_kernel(q_ref, k_ref, v_ref, b_ref, o_ref, lse_ref, *, dil):
    t = A_KEYS
    hp = pl.program_id(1)
    tiles = q_ref.shape[1] // (dil * t)
    lane = lax.broadcasted_iota(jnp.int32, (t, LANES), 1)
    low = lane < HEAD_DIM

    @pl.when(hp == 0)
    def _():
        lse_ref[...] = jnp.zeros_like(lse_ref)

    def tile(r, qi):
        rows = pl.ds(r + dil * t * qi, t, stride=dil)
        prev = pl.ds(r + dil * t * max(qi - 1, 0), t, stride=dil)
        first = min(qi, 1)
        q2 = (q_ref[0, rows, :] * ATTN_SCALE).astype(BF16)
        k2 = jnp.concatenate([k_ref[0, prev, :], k_ref[0, rows, :]], axis=0).astype(BF16)
        v2 = jnp.concatenate([v_ref[0, prev, :], v_ref[0, rows, :]], axis=0).astype(BF16)
        lse_t = lse_ref[0, rows, :]
        outs = []
        for a in range(2):
            h = 2 * hp + a
            qm = jnp.where(low if a == 0 else jnp.logical_not(low), q2, jnp.zeros_like(q2))
            s = _dot_nt(qm, k2) + b_ref[first, h]
            m = jnp.max(s, axis=-1, keepdims=True)
            e = jnp.exp(s - m)
            l = jnp.sum(e, axis=-1, keepdims=True)
            outs.append(_dot(e.astype(BF16), v2) / l)
            lse_t = jnp.where(lane == h, m + jnp.log(l), lse_t)
        o_ref[0, rows, :] = jnp.where(low, outs[0], outs[1])
        lse_ref[0, rows, :] = lse_t

    for r in range(dil):
        for qi in range(tiles):
            tile(r, qi)


def _swa_bias(rel_bias, dil, t):
    later = _toeplitz_bias(rel_bias, t, 2 * t, lambda x: dil * (t - x), lambda x: (t - x >= 0) & (t - x <= t))
    first = jnp.where(jnp.asarray(np.arange(2 * t) >= t)[None, None, :], later, NEG)
    return jnp.stack([first, later])


def _swa_group(q, kv, rel_bias, bsz, seq, dil):
    d = D_MODEL
    nhp = N_HEADS // 2
    assert seq % (dil * A_KEYS) == 0
    bias = _swa_bias(rel_bias, dil, A_KEYS)
    pair = lambda off: pl.BlockSpec((1, seq, LANES), lambda b, hp: (b, 0, off + hp))
    o, lse = pl.pallas_call(
        functools.partial(_swa_kernel, dil=dil),
        grid=(bsz, nhp),
        in_specs=[pair(0), pair(0), pair(nhp), pl.BlockSpec(bias.shape, lambda b, hp: (0, 0, 0, 0))],
        out_specs=[pair(0), pl.BlockSpec((1, seq, LANES), lambda b, hp: (b, 0, 0))],
        out_shape=[jax.ShapeDtypeStruct((bsz, seq, d), F32), jax.ShapeDtypeStruct((bsz, seq, LANES), F32)],
        compiler_params=_cparams("parallel", "arbitrary"),
        name="swa_attn",
    )(q.reshape(bsz, seq, d), kv.reshape(bsz, seq, 2 * d), kv.reshape(bsz, seq, 2 * d), bias)
    return o.reshape(bsz * seq, d), lse.reshape(bsz * seq, LANES)


def _head_diag(rows, width):
    row = lax.broadcasted_iota(jnp.int32, (rows, width), 0)
    lane = lax.broadcasted_iota(jnp.int32, (rows, width), 1)
    return (lane // HEAD_DIM) == row


def _rows_last(cache):
    nd = cache.ndim
    t = jnp.transpose(cache, (0,) + tuple(range(2, nd)) + (1,))
    bsz, window = cache.shape[0], cache.shape[1]
    lead = int(np.prod(cache.shape[2:nd - 2]))
    return t.reshape(bsz, lead, cache.shape[nd - 2] * cache.shape[nd - 1], window)


def _rows_second(cache_t, feat):
    bsz, _, _, window = cache_t.shape
    nd = len(feat) + 2
    t = cache_t.reshape((bsz,) + tuple(feat) + (window,))
    return jnp.transpose(t, (0, nd - 1) + tuple(range(1, nd - 1)))


def _shift_in(x, col):
    w = x.shape[1]
    lane = lax.broadcasted_iota(jnp.int32, x.shape, 1)
    return jnp.where(lane == w - 1, col, pltpu.roll(x, w - 1, 1))


A_HEAD_CHUNK = 4


def _samp_a_kernel(q0, q1, q2, n0, n1, n2, m0, m1, m2, c0, c1, c2, b0, b1, b2, bs_ref, o_ref, u0, u1, u2):
    hw = A_HEAD_CHUNK * HEAD_DIM
    diag = _head_diag(8, hw)
    outs, lses = [], []
    for q_ref, n_ref, m_ref, c_ref, b_ref, u_ref in (
            (q0, n0, m0, c0, b0, u0), (q1, n1, m1, c1, b1, u1), (q2, n2, m2, c2, b2, u2)):
        qe = jnp.where(diag, q_ref[0] * ATTN_SCALE, 0.0)
        kt = c_ref[0, 0]
        vt = c_ref[0, 1]
        kn = n_ref[0, 0:1, :]
        vn = n_ref[0, 1:2, :]
        s = _dot(qe.astype(BF16), kt.astype(BF16)) + b_ref[0]
        sn = jnp.sum(qe * kn, axis=-1, keepdims=True) + bs_ref[0][:, :1]
        m = jnp.maximum(jnp.max(s, axis=-1, keepdims=True), sn)
        e = jnp.exp(s - m)
        en = jnp.exp(sn - m)
        l = jnp.sum(e, axis=-1, keepdims=True) + en
        outs.append((_dot_nt(e.astype(BF16), vt.astype(BF16)) + en * vn) / l)
        lses.append(m + jnp.log(l))
        u_ref[0, 0] = _shift_in(kt, m_ref[0, 0])
        u_ref[0, 1] = _shift_in(vt, m_ref[0, 1])
    mx = jnp.maximum(jnp.maximum(lses[0], lses[1]), lses[2])
    es = [jnp.exp(l - mx) for l in lses]
    tot = es[0] + es[1] + es[2]
    merged = (es[0] / tot) * outs[0] + (es[1] / tot) * outs[1] + (es[2] / tot) * outs[2]
    o_ref[0] = jnp.sum(jnp.where(diag, merged, 0.0), axis=0, keepdims=True)


def _samp_a(qs, kvs, caches_t, rel_bias):
    bsz, d = qs[0].shape
    hw = A_HEAD_CHUNK * HEAD_DIM
    nhc = N_HEADS // A_HEAD_CHUNK
    pad_rows = lambda a: jnp.pad(a.reshape(nhc, A_HEAD_CHUNK, -1), ((0, 0), (0, 8 - A_HEAD_CHUNK), (0, 0)))
    biases = []
    for window, dil in DIL_PATTERNS:
        dist = window - np.arange(window)
        biases.append(pad_rows(_bias_from_dist(rel_bias, dist, dist % dil == 0)))
    bself = pad_rows(_bias_from_dist(rel_bias, np.zeros((LANES,), np.int64), np.ones((LANES,), bool)))
    q_spec = pl.BlockSpec((1, 1, hw), lambda b, c: (b, 0, c))
    n_spec = pl.BlockSpec((1, 2, hw), lambda b, c: (b, 0, c))
    m_spec = pl.BlockSpec((1, 2, hw, 1), lambda b, c: (b, 0, c, 0))
    c_spec = lambda w: pl.BlockSpec((1, 2, hw, w), lambda b, c: (b, 0, c, 0))
    b_spec = lambda w: pl.BlockSpec((1, 8, w), lambda b, c: (c, 0, 0))
    windows = [w for w, _ in DIL_PATTERNS]
    res = pl.pallas_call(
        _samp_a_kernel,
        grid=(bsz, nhc),
        in_specs=[q_spec] * 3 + [n_spec] * 3 + [m_spec] * 3 + [c_spec(w) for w in windows]
        + [b_spec(w) for w in windows] + [b_spec(LANES)],
        out_specs=[q_spec] + [c_spec(w) for w in windows],
        out_shape=[jax.ShapeDtypeStruct((bsz, 1, d), F32)] + [jax.ShapeDtypeStruct(c.shape, F32) for c in caches_t],
        compiler_params=_cparams("parallel", "parallel"),
        name="sample_dilated_attn",
    )(*[q.reshape(bsz, 1, d) for q in qs], *[kv.reshape(bsz, 2, d) for kv in kvs],
      *[kv.reshape(bsz, 2, d, 1) for kv in kvs], *caches_t, *biases, bself)
    return res[0].reshape(bsz, d), res[1:]


def _cmp_weights(phi_pe, phi_w1, phi_w2):
    r = CMP_LEN // CMP_STRIDE
    eye = jnp.eye(N_GROUPS, dtype=F32)
    w1r = phi_w1.reshape(2, r, CMP_STRIDE, HEAD_DIM, CMP_HIDDEN)
    w1bd = jnp.einsum("kpsdh,gG->kpsgdGh", w1r, eye).reshape(
        2, r, CMP_STRIDE, N_GROUPS * HEAD_DIM, N_GROUPS * CMP_HIDDEN).astype(BF16)
    w2bd = jnp.einsum("khd,gG->kghGd", phi_w2, eye).reshape(
        2, N_GROUPS * CMP_HIDDEN, N_GROUPS * HEAD_DIM).astype(BF16)
    w2ex = jnp.einsum("khd,gG,r->kghGrd", phi_w2, eye, jnp.ones((GROUP_HEADS,), F32)).reshape(
        2, N_GROUPS * CMP_HIDDEN, D_MODEL).astype(BF16)
    pe = jnp.tile(phi_pe.reshape(2, r, CMP_STRIDE, 1, HEAD_DIM), (1, 1, 1, N_GROUPS, 1)).reshape(
        2, r, CMP_STRIDE, N_GROUPS * HEAD_DIM)
    return w1bd, w2bd, w2ex, pe


def _cmp_prompt_kernel(x_ref, w1_ref, pe_ref, w2_ref, o_ref, *, kind, transposed):
    nch = x_ref.shape[1]
    gw = N_GROUPS * HEAD_DIM
    hid = []
    for part in range(CMP_LEN // CMP_STRIDE):
        acc = jnp.zeros((nch, N_GROUPS * CMP_HIDDEN), F32)
        for s in range(CMP_STRIDE):
            c0 = s * 4 * gw + kind * gw
            x = (x_ref[0, :, c0:c0 + gw] + pe_ref[part, s:s + 1, :]).astype(BF16)
            acc = acc + _dot(x, w1_ref[part, s])
        hid.append(acc)
    h = hid[0] + pltpu.roll(hid[1], nch - 1, 0)
    h = jnp.maximum(h, 0.0).astype(BF16)
    o_ref[0] = _dot_nt(w2_ref[...], h) if transposed else _dot(h, w2_ref[...])


def _cmp_prompt(kvp, bsz, seq, kind, w1bd, pe, w2bd, transposed):
    nch = seq // CMP_STRIDE
    xv = kvp.reshape(bsz, nch, CMP_STRIDE * kvp.shape[1])
    full = lambda a: pl.BlockSpec(a.shape, lambda b: (0,) * a.ndim)
    w1, p, w2 = w1bd[kind], pe[kind], w2bd[kind]
    out_dims = (w2.shape[1], nch) if transposed else (nch, w2.shape[1])
    if transposed:
        w2 = w2.T
    return pl.pallas_call(
        functools.partial(_cmp_prompt_kernel, kind=kind, transposed=transposed),
        grid=(bsz,),
        in_specs=[pl.BlockSpec((1, nch, xv.shape[2]), lambda b: (b, 0, 0)), full(w1), full(p), full(w2)],
        out_specs=pl.BlockSpec((1,) + out_dims, lambda b: (b, 0, 0)),
        out_shape=jax.ShapeDtypeStruct((bsz,) + out_dims, F32),
        compiler_params=_cparams("parallel"),
        name="compress_prompt",
    )(xv, w1, p, w2)


def _masked_softmax(s, valid, axis=-1, exp=jnp.exp):
    m = jnp.max(s, axis=axis, keepdims=True)
    m = jnp.where(m > 0.5 * NEG, m, 0.0)
    e = jnp.where(valid, exp(s - m), 0.0)
    return e / jnp.maximum(jnp.sum(e, axis=axis, keepdims=True), 1e-30)


def _topk_mask_t(score, j, n_iter):
    rank = jnp.zeros(score.shape, F32)
    for i in range(n_iter):
        si = score[i:i + 1, :]
        beats = jnp.logical_or(si > score, jnp.logical_and(si == score, j > i))
        rank = rank + jnp.where(beats, 1.0, 0.0)
    return jnp.where(rank < SEL_TOPK, 1.0, 0.0)


def _nsa_kernel(q_ref, gate_ref, kc_ref, vct_ref, ks_ref, vs_ref, kw_ref, vw_ref, bct_ref, ut_ref, uwt_ref,
                a_ref, ebt_ref, o_ref, ksb, vst, kwb, vwt, selb, m_ref, l_ref, acc_ref, *, seq, tq, tc):
    g = pl.program_id(0)
    qi = pl.program_id(2)
    q0 = qi * tq
    par = g % 2
    nb = seq // SEL_BLOCK
    lane = lax.broadcasted_iota(jnp.int32, (tq, LANES), 1)
    keep = (lane // HEAD_DIM) == par

    @pl.when(qi == 0)
    def _():
        ksb[...] = ks_ref[0].astype(BF16)
        kwb[...] = kw_ref[0].astype(BF16)
        for c in range(seq // (2 * LANES)):
            sl = slice(c * 2 * LANES, (c + 1) * 2 * LANES)
            vst[:, sl] = vs_ref[0, sl, :].T.astype(BF16)
            vwt[:, sl] = vw_ref[0, sl, :].T.astype(BF16)

    qs = []
    for r in range(GROUP_HEADS):
        qh = q_ref[0, :, (r // 2) * LANES:(r // 2 + 1) * LANES].astype(F32) * (ATTN_SCALE * LOG2E)
        qh = jnp.where(par == (r % 2), qh, pltpu.roll(qh, HEAD_DIM, 1))
        qs.append(jnp.where(keep, qh, 0.0).astype(BF16))
    q4 = jnp.concatenate(qs, axis=0)

    bct = bct_ref[0, 0]
    pc = _masked_softmax(_dot_nt(kc_ref[0].astype(BF16), q4) + bct, bct > 0.5 * NEG, axis=0, exp=jnp.exp2)
    oc = _dot(vct_ref[0].astype(BF16), pc.astype(BF16))
    imp = pc[:, 0:tq] + pc[:, tq:2 * tq] + pc[:, 2 * tq:3 * tq] + pc[:, 3 * tq:4 * tq]

    hi = imp.astype(BF16)
    lo = (imp - hi.astype(F32)).astype(BF16)
    imp_t = _dot(a_ref[...], hi) + _dot(a_ref[...], lo)
    j = lax.broadcasted_iota(jnp.int32, (nb, tq), 0)
    qpos = q0 + lax.broadcasted_iota(jnp.int32, (nb, tq), 1)
    cur = qpos // SEL_BLOCK
    forced = jnp.logical_or(j == 0, jnp.logical_or(j == cur, j == cur - 1))
    score = jnp.where(forced, SEL_FORCE, jnp.where(j * SEL_BLOCK <= qpos, imp_t, -1.0))
    sel_t = _topk_mask_t(score, j, nb).astype(BF16)
    sb = (_dot(ebt_ref[...], sel_t) - 1.0) * (-NEG)
    selb[...] = sb

    m_ref[...] = jnp.full(m_ref.shape, NEG, F32)
    l_ref[...] = jnp.zeros(l_ref.shape, F32)
    acc_ref[...] = jnp.zeros(acc_ref.shape, F32)

    def sel_chunk(k0, size):
        k0 = pl.multiple_of(k0, size)
        u0 = pl.multiple_of(k0 - q0 + (seq - tq), tq)
        sbk = selb[pl.ds(k0, size), :]
        s = (_dot_nt(ksb[pl.ds(k0, size), :], q4) + ut_ref[0, pl.ds(u0, size), :]
             + jnp.concatenate([sbk] * GROUP_HEADS, axis=1))
        m_prev = m_ref[...]
        m_new = jnp.maximum(m_prev, jnp.max(s, axis=0, keepdims=True))
        alpha = jnp.exp2(m_prev - m_new)
        e = jnp.exp2(s - m_new)
        l_ref[...] = alpha * l_ref[...] + jnp.sum(e, axis=0, keepdims=True)
        acc_ref[...] = alpha * acc_ref[...] + _dot(vst[:, pl.ds(k0, size)], e.astype(BF16))
        m_ref[...] = m_new

    def full_chunk(t, carry):
        sel_chunk(t * tc, tc)
        return carry

    halves = (q0 + tq + tc // 2 - 1) // (tc // 2)
    lax.fori_loop(0, halves // 2, full_chunk, 0)

    @pl.when(halves % 2 == 1)
    def _():
        sel_chunk((halves // 2) * tc, tc // 2)

    osel = acc_ref[...] / l_ref[...]

    ww = SLIDE_WIN + tq
    w0 = pl.multiple_of(jnp.maximum(q0 - SLIDE_WIN, 0), tq)
    uw0 = pl.multiple_of(w0 - (q0 - SLIDE_WIN), tq)
    s = _dot_nt(kwb[pl.ds(w0, ww), :], q4) + uwt_ref[0, pl.ds(uw0, ww), :]
    e = jnp.exp2(s - jnp.max(s, axis=0, keepdims=True))
    owin = _dot(vwt[:, pl.ds(w0, ww)], e.astype(BF16)) / jnp.sum(e, axis=0, keepdims=True)

    gt = gate_ref[0].T
    rows = []
    for r in range(GROUP_HEADS):
        cs = slice(r * tq, (r + 1) * tq)
        o = (gt[3 * r:3 * r + 1, :] * oc[:, cs] + gt[3 * r + 1:3 * r + 2, :] * osel[:, cs]
             + gt[3 * r + 2:3 * r + 3, :] * owin[:, cs])
        rows.append(jnp.where(par == 0, o[:HEAD_DIM], o[HEAD_DIM:]))
    o_ref[0] = jnp.concatenate(rows, axis=0).T.astype(o_ref.dtype)


def _imp_to_block_matrix(n_cmp_pad, n_cmp, n_blk):
    ratio = SEL_BLOCK // CMP_STRIDE
    span = CMP_LEN // CMP_STRIDE
    a = np.zeros((n_blk, n_cmp_pad), np.float32)
    for m in range(ratio):
        for n in range(span):
            for jb in range(n_blk):
                c = ratio * jb + m - n
                if 0 <= c < n_cmp:
                    a[jb, c] += 1.0
    return a


def _heads_to_lanes(t, lead):
    nl = len(lead)
    rows, tq = t.shape[-2:]
    t = t.reshape((N_GROUPS, GROUP_HEADS) + tuple(lead) + (rows, tq))
    t = jnp.transpose(t, (0,) + tuple(range(2, 2 + nl)) + (2 + nl, 1, 3 + nl))
    return t.reshape((N_GROUPS,) + tuple(lead) + (rows, GROUP_HEADS * tq))


def _nsa_prompt(q, gate, kc, vct, kvp, kvw, rel_bias, bsz, seq):
    tq, tc = LANES, 8 * LANES
    d = D_MODEL
    nb = seq // SEL_BLOCK
    ncp = seq // CMP_STRIDE
    nqt = seq // tq
    n_cmp = (seq - CMP_LEN) // CMP_STRIDE + 1
    nq = GROUP_HEADS * tq
    assert ncp == LANES and SLIDE_WIN % tq == 0
    qv = q.reshape(bsz, seq, d)
    gv = gate.reshape(bsz, seq, N_GROUPS * LANES)
    kvpv = kvp.reshape(bsz, seq, kvp.shape[1])
    kvwv = kvw.reshape(bsz, seq, kvw.shape[1])

    sub = np.arange(CMP_STRIDE)[:, None]
    x = (np.arange(2 * ncp) - (ncp - 1))[None, :]
    dist_c = -CMP_STRIDE * x + sub - (CMP_LEN - 1)
    bias_c = _toeplitz(_bias_from_dist(rel_bias, dist_c, dist_c >= 0), ncp, ncp)
    bias_c = jnp.where(jnp.asarray(np.arange(ncp) < n_cmp), bias_c, NEG)
    bias_c = jnp.transpose(bias_c, (0, 3, 2, 1)).reshape(N_HEADS, ncp, nqt, tq)
    bias_ct = _heads_to_lanes(jnp.transpose(bias_c, (0, 2, 1, 3)), (nqt,))
    wu = seq - tq + tc
    strip = _toeplitz_bias(rel_bias, tq, wu, lambda x: (seq - tq) - x, lambda x: (seq - tq) - x >= 0)
    strip_t = _heads_to_lanes(jnp.swapaxes(strip, 1, 2), ())
    ww = 2 * SLIDE_WIN + tq
    strip_w = _toeplitz_bias(rel_bias, tq, ww, lambda x: SLIDE_WIN - x,
                             lambda x: (SLIDE_WIN - x >= 0) & (SLIDE_WIN - x < SLIDE_WIN))
    strip_wt = _heads_to_lanes(jnp.swapaxes(strip_w, 1, 2), ())

    bias_ct, strip_t, strip_wt = bias_ct * LOG2E, strip_t * LOG2E, strip_wt * LOG2E

    a_m = jnp.asarray(_imp_to_block_matrix(ncp, n_cmp, nb), BF16)
    e_bt = jnp.asarray(np.kron(np.eye(nb), np.ones((SEL_BLOCK, 1))), BF16)

    half = lambda g: g // 2
    kv_spec = lambda off: pl.BlockSpec((1, seq, LANES), lambda g, b, qi: (b, 0, off + half(g)))
    full = lambda a: pl.BlockSpec(a.shape, lambda g, b, qi: (0,) * a.ndim)
    out = pl.pallas_call(
        functools.partial(_nsa_kernel, seq=seq, tq=tq, tc=tc),
        grid=(N_GROUPS, bsz, nqt),
        in_specs=[pl.BlockSpec((1, tq, GROUP_HEADS * HEAD_DIM), lambda g, b, qi: (b, qi, g)),
                  pl.BlockSpec((1, tq, LANES), lambda g, b, qi: (b, qi, g)),
                  pl.BlockSpec((1, ncp, LANES), lambda g, b, qi: (b, 0, half(g))),
                  pl.BlockSpec((1, LANES, ncp), lambda g, b, qi: (b, half(g), 0)),
                  kv_spec(4), kv_spec(6), kv_spec(0), kv_spec(2),
                  pl.BlockSpec((1, 1, ncp, nq), lambda g, b, qi: (g, qi, 0, 0)),
                  pl.BlockSpec((1, wu, nq), lambda g, b, qi: (g, 0, 0)),
                  pl.BlockSpec((1, ww, nq), lambda g, b, qi: (g, 0, 0)),
                  full(a_m), full(e_bt)],
        out_specs=pl.BlockSpec((1, tq, GROUP_HEADS * HEAD_DIM), lambda g, b, qi: (b, qi, g)),
        out_shape=jax.ShapeDtypeStruct((bsz, seq, d), BF16),
        scratch_shapes=[pltpu.VMEM((seq, LANES), BF16), pltpu.VMEM((LANES, seq), BF16),
                        pltpu.VMEM((seq, LANES), BF16), pltpu.VMEM((LANES, seq), BF16),
                        pltpu.VMEM((seq, tq), F32),
                        pltpu.VMEM((1, nq), F32), pltpu.VMEM((1, nq), F32), pltpu.VMEM((LANES, nq), F32)],
        compiler_params=_cparams("parallel", "parallel", "arbitrary"),
        name="nsa_prompt_attn",
    )(qv, gv, kc, vct, kvpv, kvpv, kvwv, kvwv, bias_ct, strip_t, strip_wt, a_m, e_bt)
    return out.reshape(bsz * seq, d)


CMP_PAGES = 16


def _scmp_kernel(pt_ref, *refs):
    del pt_ref
    pages = refs[:CMP_PAGES]
    w1_ref, pe_ref, o_ref, t_scr, x_scr = refs[CMP_PAGES:]
    cpp = pages[0].shape[3] // CMP_STRIDE
    nh = N_GROUPS * CMP_HIDDEN
    for kind in range(2):
        for i, p_ref in enumerate(pages):
            x = p_ref[0, kind].T
            for hh in range(x.shape[1] // LANES):
                t_scr[i, hh] = x[:, hh * LANES:(hh + 1) * LANES]
            for s in range(CMP_STRIDE):
                for hh in range(x.shape[1] // LANES):
                    x_scr[s, i * cpp:(i + 1) * cpp, hh * LANES:(hh + 1) * LANES] = (
                        t_scr[i, hh, pl.ds(s, cpp, stride=CMP_STRIDE), :])
        for part in range(CMP_LEN // CMP_STRIDE):
            acc = jnp.zeros((x_scr.shape[1], nh), F32)
            for s in range(CMP_STRIDE):
                x = (x_scr[s] + pe_ref[kind, part, s:s + 1, :]).astype(BF16)
                acc = acc + _dot(x, w1_ref[kind, part, s])
            o_ref[kind, 0, :, part * nh:(part + 1) * nh] = acc


def _scmp(cache, page_table, w1bd, pe):
    bsz, n_pages = page_table.shape
    _, _, gw, page = cache.shape
    cpp = page // CMP_STRIDE
    nh = N_GROUPS * CMP_HIDDEN
    n_chunks = n_pages * cpp
    page_spec = lambda i: pl.BlockSpec((1, 2, gw, page), lambda b, j, pt: (pt[b, j * CMP_PAGES + i], 0, 0, 0))
    const = lambda a: pl.BlockSpec(a.shape, lambda b, j, pt: (0,) * a.ndim, pipeline_mode=pl.Buffered(1))
    grid_spec = pltpu.PrefetchScalarGridSpec(
        num_scalar_prefetch=1,
        grid=(bsz, n_pages // CMP_PAGES),
        in_specs=[page_spec(i) for i in range(CMP_PAGES)] + [const(w1bd), const(pe)],
        out_specs=pl.BlockSpec((2, 1, CMP_PAGES * cpp, 2 * nh), lambda b, j, pt: (0, b, j, 0)),
        scratch_shapes=[pltpu.VMEM((CMP_PAGES, gw // LANES, page, LANES), F32),
                        pltpu.VMEM((CMP_STRIDE, CMP_PAGES * cpp, gw), F32)])
    return pl.pallas_call(
        _scmp_kernel,
        grid_spec=grid_spec,
        out_shape=jax.ShapeDtypeStruct((2, bsz, n_chunks, 2 * nh), F32),
        compiler_params=_cparams("parallel", "arbitrary"),
        name="compress_sample",
    )(page_table, *([cache] * CMP_PAGES), w1bd, pe)


def _s1_kernel(ab_ref, w2_ref, q_ref, bc_ref, at_ref, gs_ref, oc_ref, it_ref):
    nck = ab_ref.shape[2]
    nh = N_GROUPS * CMP_HIDDEN
    kv = []
    for kind in range(2):
        h = ab_ref[kind, 0, :, :nh] + pltpu.roll(ab_ref[kind, 0, :, nh:], nck - 1, 0)
        kv.append(_dot(jnp.maximum(h, 0.0).astype(BF16), w2_ref[kind]).astype(BF16))
    diag = _head_diag(N_HEADS, D_MODEL)
    qe = jnp.where(diag, q_ref[0] * ATTN_SCALE, 0.0).astype(BF16)
    bc = bc_ref[...]
    pc = _masked_softmax(_dot_nt(qe, kv[0]) + bc, bc > 0.5 * NEG)
    o = _dot(pc.astype(BF16), kv[1])
    oc_ref[0] = jnp.sum(jnp.where(diag, o, 0.0), axis=0, keepdims=True)
    hi = pc.astype(BF16)
    lo = (pc - hi.astype(F32)).astype(BF16)
    imp = _dot(gs_ref[...], hi) + _dot(gs_ref[...], lo)
    it_ref[0] = _split_dot_nt(at_ref[...], imp)


def _s1(ab, w2ex, q, rel_bias, past):
    _, bsz, nck, _ = ab.shape
    d = D_MODEL
    n_cmp = (past + 1 - CMP_LEN) // CMP_STRIDE + 1
    n_blk = -(-(past + 1) // SEL_BLOCK)
    nbp = -(-n_blk // 8) * 8
    n = np.arange(nck)
    dist_c = past - (n * CMP_STRIDE + CMP_LEN - 1)
    bias_c = _bias_from_dist(rel_bias, dist_c, (dist_c >= 0) & (n < n_cmp))
    a_t = jnp.asarray(_imp_to_block_matrix(nck, n_cmp, nbp)[:nbp] * (np.arange(nbp)[:, None] < n_blk), BF16)
    gsum = jnp.asarray(np.kron(np.eye(8, N_GROUPS), np.ones((1, GROUP_HEADS))), BF16)
    full = lambda a: pl.BlockSpec(a.shape, lambda b: (0,) * a.ndim)
    oc, imp_t = pl.pallas_call(
        _s1_kernel,
        grid=(bsz,),
        in_specs=[pl.BlockSpec((2, 1, nck, ab.shape[3]), lambda b: (0, b, 0, 0)), full(w2ex),
                  pl.BlockSpec((1, 1, d), lambda b: (b, 0, 0)), full(bias_c), full(a_t), full(gsum)],
        out_specs=[pl.BlockSpec((1, 1, d), lambda b: (b, 0, 0)), pl.BlockSpec((1, nbp, 8), lambda b: (b, 0, 0))],
        out_shape=[jax.ShapeDtypeStruct((bsz, 1, d), F32), jax.ShapeDtypeStruct((bsz, nbp, 8), F32)],
        compiler_params=_cparams("parallel"),
        name="sample_compressed_attn",
    )(ab, w2ex, q.reshape(bsz, 1, d), bias_c, a_t, gsum)
    return oc.reshape(bsz, d), imp_t


def _s2_kernel(imp_ref, idx_ref, s_scr, *, n_blk, cur):
    shape = imp_ref.shape
    j = lax.broadcasted_iota(jnp.int32, shape, 0)
    forced = jnp.logical_or(j == 0, jnp.logical_or(j == cur, j == cur - 1))
    score = jnp.where(forced, SEL_FORCE, jnp.where(j < n_blk, imp_ref[...], -2.0))
    s_scr[...] = score

    def body(i, rank):
        si = s_scr[pl.ds(i, 1), :]
        beats = jnp.logical_or(si > score, jnp.logical_and(si == score, j > i))
        return rank + jnp.where(beats, 1.0, 0.0)

    rank = lax.fori_loop(0, n_blk, body, jnp.zeros(shape, F32))
    for k in range(SEL_TOPK):
        idx_ref[k:k + 1, :] = jnp.sum(jnp.where(rank == k, j, 0), axis=0, keepdims=True)


def _s2(imp_t, past):
    n_blk = -(-(past + 1) // SEL_BLOCK)
    return pl.pallas_call(
        functools.partial(_s2_kernel, n_blk=n_blk, cur=past // SEL_BLOCK),
        out_shape=jax.ShapeDtypeStruct((SEL_TOPK, imp_t.shape[1]), jnp.int32),
        scratch_shapes=[pltpu.VMEM(imp_t.shape, F32)],
        name="sample_block_topk",
    )(imp_t)


def _bucket_thresholds(max_dist):
    b = _bucket_np(np.arange(max_dist + 1))
    return [int(np.argmax(b >= k)) if (b >= k).any() else max_dist + 1 for k in range(REL_BUCKETS)]


def _attend_t(q, kt, vt, bias, valid, kn, vn, bias_new):
    s = _dot(q.astype(BF16), kt.astype(BF16)) + bias
    if valid is not None:
        s = jnp.where(valid, s, NEG)
    sn = jnp.sum(q * kn, axis=-1, keepdims=True) + bias_new
    m = jnp.maximum(jnp.max(s, axis=-1, keepdims=True), sn)
    e = jnp.exp(s - m)
    en = jnp.exp(sn - m)
    l = jnp.sum(e, axis=-1, keepdims=True) + en
    return (_dot_nt(e.astype(BF16), vt.astype(BF16)) + en * vn) / l


def _ssel_kernel(idx_ref, page_ref, *refs, past, thresholds):
    del page_ref
    blocks = refs[:SEL_TOPK]
    q_ref, n_ref, tb_ref, o_ref = refs[SEL_TOPK:]
    b = pl.program_id(0)
    g = pl.program_id(1)
    page = blocks[0].shape[3]
    kt = jnp.concatenate([blk[0, 0] for blk in blocks], axis=1)
    vt = jnp.concatenate([blk[0, 1] for blk in blocks], axis=1)
    nk = SEL_TOPK * page
    lane = lax.broadcasted_iota(jnp.int32, (1, nk), 1)
    row = lane % page
    pos = jnp.zeros((1, nk), jnp.int32)
    inside = jnp.zeros((1, nk), jnp.int32)
    bpp = page // SEL_BLOCK
    for k in range(SEL_TOPK):
        blk_idx = idx_ref[(b * N_GROUPS + g) * SEL_TOPK + k]
        mine = lane // page == k
        pos = jnp.where(mine, (blk_idx // bpp) * page + row, pos)
        inside = jnp.where(mine, jnp.where(row // SEL_BLOCK == blk_idx % bpp, 1, 0), inside)
    valid = jnp.logical_and(inside == 1, pos < past)
    dist = past - pos
    bias = jnp.broadcast_to(tb_ref[0, 0][:, :1], (8, nk))
    for k in range(1, REL_BUCKETS):
        bias = jnp.where(dist >= thresholds[k], tb_ref[0, k][:, :1], bias)
    q = q_ref[0, 0] * ATTN_SCALE
    o_ref[0, 0] = _attend_t(q, kt, vt, bias, valid, n_ref[0, 0, 0], n_ref[0, 1, 0], tb_ref[0, 0][:, :1])


def _ssel(sel_idx, cache_t, page_table, q, kvp_new, rel_bias, past):
    bsz = q.shape[0]
    page = cache_t.shape[3]
    bpp = page // SEL_BLOCK
    cur = past // SEL_BLOCK
    cached = jnp.minimum(sel_idx, cur - 1)
    pages = jnp.take_along_axis(page_table, (cached // bpp).reshape(bsz, -1), axis=1).reshape(-1)
    q4 = jnp.pad(q.reshape(bsz, N_GROUPS, GROUP_HEADS, HEAD_DIM), ((0, 0), (0, 0), (0, 8 - GROUP_HEADS), (0, 0)))
    new = kvp_new.reshape(bsz, 4, N_GROUPS, 1, HEAD_DIM)
    tb = jnp.transpose(rel_bias.astype(F32).reshape(REL_BUCKETS, N_GROUPS, GROUP_HEADS), (1, 0, 2))
    tb = jnp.pad(tb, ((0, 0), (0, 0), (0, 8 - GROUP_HEADS)))
    tb = jnp.broadcast_to(tb[..., None], tb.shape + (LANES,))
    blk_spec = lambda k: pl.BlockSpec(
        (1, 2, HEAD_DIM, page), lambda b, g, idx, pg: (pg[(b * N_GROUPS + g) * SEL_TOPK + k], 1, g, 0))
    grid_spec = pltpu.PrefetchScalarGridSpec(
        num_scalar_prefetch=2,
        grid=(bsz, N_GROUPS),
        in_specs=[blk_spec(k) for k in range(SEL_TOPK)] + [
            pl.BlockSpec((1, 1, 8, HEAD_DIM), lambda b, g, idx, pg: (b, g, 0, 0)),
            pl.BlockSpec((1, 2, 1, 1, HEAD_DIM), lambda b, g, idx, pg: (b, 1, g, 0, 0)),
            pl.BlockSpec((1, REL_BUCKETS, 8, LANES), lambda b, g, idx, pg: (g, 0, 0, 0))],
        out_specs=pl.BlockSpec((1, 1, 8, HEAD_DIM), lambda b, g, idx, pg: (b, g, 0, 0)))
    out = pl.pallas_call(
        functools.partial(_ssel_kernel, past=past, thresholds=tuple(_bucket_thresholds(past))),
        grid_spec=grid_spec,
        out_shape=jax.ShapeDtypeStruct((bsz, N_GROUPS, 8, HEAD_DIM), F32),
        compiler_params=_cparams("parallel", "parallel"),
        name="sample_selected_attn",
    )(sel_idx.reshape(-1), pages, *([cache_t] * SEL_TOPK), q4, new, tb)
    return out[:, :, :GROUP_HEADS].reshape(bsz, N_HEADS, HEAD_DIM)


def _swin_kernel(c_ref, q_ref, n_ref, m_ref, bw_ref, b0_ref, o_ref, u_ref):
    kt = c_ref[0, 0]
    vt = c_ref[0, 1]
    q = q_ref[0] * ATTN_SCALE
    o = _attend_t(q, kt, vt, bw_ref[...], None, n_ref[0, 0:1, :], n_ref[0, 1:2, :], b0_ref[...][:, :1])
    grp = lax.broadcasted_iota(jnp.int32, (N_HEADS, HEAD_DIM), 0) // GROUP_HEADS
    out = jnp.zeros((N_HEADS, HEAD_DIM), F32)
    for g in range(N_GROUPS):
        out = jnp.where(grp == g, o[:, g * HEAD_DIM:(g + 1) * HEAD_DIM], out)
    o_ref[0] = out
    u_ref[0, 0] = _shift_in(kt, m_ref[0, 0])
    u_ref[0, 1] = _shift_in(vt, m_ref[0, 1])


def _swin(cwin_t, q, kvw_new, rel_bias):
    bsz = q.shape[0]
    _, _, gw, wlen = cwin_t.shape
    onehot = jnp.asarray(np.kron(np.eye(N_GROUPS), np.ones((GROUP_HEADS, 1))), F32)
    qbd = jnp.einsum("bhd,hg->bhgd", q.reshape(bsz, N_HEADS, HEAD_DIM), onehot).reshape(bsz, N_HEADS, gw)
    dist_w = wlen - np.arange(wlen)
    bw = _bias_from_dist(rel_bias, dist_w, dist_w < SLIDE_WIN)
    b0 = _bias_from_dist(rel_bias, np.zeros((LANES,), np.int64), np.ones((LANES,), bool))
    full = lambda a: pl.BlockSpec(a.shape, lambda b: (0,) * a.ndim)
    c_spec = pl.BlockSpec((1, 2, gw, wlen), lambda b: (b, 0, 0, 0))
    o_spec = pl.BlockSpec((1, N_HEADS, HEAD_DIM), lambda b: (b, 0, 0))
    return pl.pallas_call(
        _swin_kernel,
        grid=(bsz,),
        in_specs=[c_spec, pl.BlockSpec((1, N_HEADS, gw), lambda b: (b, 0, 0)),
                  pl.BlockSpec((1, 2, gw), lambda b: (b, 0, 0)),
                  pl.BlockSpec((1, 2, gw, 1), lambda b: (b, 0, 0, 0)), full(bw), full(b0)],
        out_specs=[o_spec, c_spec],
        out_shape=[jax.ShapeDtypeStruct((bsz, N_HEADS, HEAD_DIM), F32), jax.ShapeDtypeStruct(cwin_t.shape, F32)],
        compiler_params=_cparams("parallel"),
        name="sample_window_attn",
    )(cwin_t, qbd, kvw_new.reshape(bsz, 2, gw), kvw_new.reshape(bsz, 2, gw, 1), bw, b0)


def _gate_sum_kernel(g_ref, oc_ref, os_ref, ow_ref, o_ref):
    g = g_ref[...]
    o_ref[...] = g[:, :, 0:1] * oc_ref[...] + g[:, :, 1:2] * os_ref[...] + g[:, :, 2:3] * ow_ref[...]


def _gate_sum(gate, oc, os_, ow):
    return pl.pallas_call(
        _gate_sum_kernel,
        out_shape=jax.ShapeDtypeStruct(oc.shape, F32),
        name="sample_gate_sum",
    )(gate, oc, os_, ow)


def _split_a_weights(w_in):
    d = D_MODEL
    w = w_in.astype(BF16)
    qs = [w[:, g * 3 * d:g * 3 * d + d] for g in range(len(DIL_PATTERNS))]
    kvs = [w[:, g * 3 * d + d:(g + 1) * 3 * d] for g in range(len(DIL_PATTERNS))]
    return qs + kvs


def _split_b_weights(w_in):
    d = D_MODEL
    gw = N_GROUPS * HEAD_DIM
    w = w_in.astype(BF16)
    wg = w[:, d + 6 * gw:].reshape(d, N_GROUPS, GROUP_HEADS * 3)
    wg = jnp.pad(wg, ((0, 0), (0, 0), (0, LANES - GROUP_HEADS * 3))).reshape(d, N_GROUPS * LANES)
    return [w[:, :d], w[:, d:d + 4 * gw], w[:, d + 4 * gw:d + 6 * gw], wg]


def _unblock_gate(gate):
    m = gate.shape[0]
    return gate.reshape(m, N_GROUPS, LANES)[:, :, :GROUP_HEADS * 3].reshape(m, N_HEADS * 3)


def kernel(x_prompt, x_sample, cache_a_win0, cache_a_win1, cache_a_win2, cache_b_kv, cache_b_win, page_table,
           rel_bias, norm_mix, norm_mlp, norm_final, a_w_in, a_w_out, b_w_in, b_w_out, b_phi_pe, b_phi_w1,
           b_phi_w2, mlp_w_up, mlp_w_down):
    bsz, seq, d = x_prompt.shape
    sb = x_sample.shape[0]
    past = page_table.shape[1] * cache_b_kv.shape[2]
    tm_p, tm_s = 512, sb
    tm_w = 1024
    xp = x_prompt.reshape(bsz * seq, d)
    xs = x_sample.reshape(sb, d)
    acts6 = ("none",) * 6

    wa = _split_a_weights(a_w_in[0])
    wa_out = a_w_out[0].astype(BF16)
    full_window = [gi for gi, (w, _) in enumerate(DIL_PATTERNS) if w >= seq]
    pa = _norm_matmul(xp, norm_mix[0], wa, acts6, tm_w, 4, [wa[3 + gi].T for gi in full_window], seq)
    sa = _norm_matmul(xs, norm_mix[0], wa, acts6, tm_s, 4)
    os_, ls = [], []
    for gi, (_, dil) in enumerate(DIL_PATTERNS):
        o, lse = _swa_group(pa[gi], pa[3 + gi], rel_bias, bsz, seq, dil)
        os_.append(o)
        ls.append(lse)
    xp = _merge_proj(os_, ls, xp, wa_out, tm_p)
    a_caches = [_rows_last(c[0]) for c in (cache_a_win0, cache_a_win1, cache_a_win2)]
    oa, a_new_t = _samp_a(sa[:3], sa[3:], a_caches, rel_bias)
    xs = _matmul_res(oa, xs, wa_out, tm_s)
    cache_shape = lambda n: (1, -1, n, 2, N_HEADS, HEAD_DIM)
    a_new_p = [pa[3 + gi].reshape(bsz, seq, 2 * d)[:, seq - min(w, seq):].reshape(cache_shape(min(w, seq)))
               for gi, (w, _) in enumerate(DIL_PATTERNS)]
    for k, gi in enumerate(full_window):
        a_new_p[gi] = _rows_second(pa[6 + k].reshape(bsz, 2, d, seq), (2, N_HEADS, HEAD_DIM))[None]
    a_new_s = [_rows_second(c, (2, N_HEADS, HEAD_DIM)) for c in a_new_t]

    wu0, wd0 = mlp_w_up[0].astype(BF16), mlp_w_down[0].astype(BF16)
    xp = _mlp(xp, norm_mlp[0], wu0, wd0, norm_final, False, tm_w, 1024)
    xs = _mlp(xs, norm_mlp[0], wu0, wd0, norm_final, False, tm_s, 1024)

    wb = _split_b_weights(b_w_in[0])
    wb_out = b_w_out[0].astype(BF16)
    acts_b = ("none", "none", "none", "sigmoid")
    qp, kvp_p, kvw_p, gate_p, kvp_t = _norm_matmul(xp, norm_mix[1], wb, ("bf16",) + acts_b[1:], tm_p, 1,
                                                   [wb[1].T], seq)
    qs_, kvp_s, kvw_s, gate_s = _norm_matmul(xs, norm_mix[1], wb, acts_b, tm_s, 1)
    w1bd, w2bd, w2ex, pe = _cmp_weights(b_phi_pe[0], b_phi_w1[0], b_phi_w2[0])
    kc = _cmp_prompt(kvp_p, bsz, seq, 0, w1bd, pe, w2bd, False)
    vct = _cmp_prompt(kvp_p, bsz, seq, 1, w1bd, pe, w2bd, True)
    ob = _nsa_prompt(qp, gate_p, kc, vct, kvp_p, kvw_p, rel_bias, bsz, seq)
    xp = _matmul_res(ob, xp, wb_out, tm_p)

    cache_t = _rows_last(cache_b_kv[0])
    cwin_t = _rows_last(cache_b_win[0])
    ab = _scmp(cache_t, page_table, w1bd, pe)
    oc, imp_t = _s1(ab, w2ex, qs_, rel_bias, past)
    imp_t = jnp.transpose(imp_t[:, :, :N_GROUPS], (1, 0, 2)).reshape(imp_t.shape[1], sb * N_GROUPS)
    sel_idx = jnp.transpose(_s2(imp_t, past)).reshape(sb, N_GROUPS, SEL_TOPK)
    osel = _ssel(sel_idx, cache_t, page_table, qs_, kvp_s, rel_bias, past)
    owin, b_win_t = _swin(cwin_t, qs_, kvw_s, rel_bias)
    obs = _gate_sum(_unblock_gate(gate_s).reshape(sb, N_HEADS, 3), oc.reshape(sb, N_HEADS, HEAD_DIM), osel, owin)
    xs = _matmul_res(obs.reshape(sb, d), xs, wb_out, tm_s)
    b_win_s = _rows_second(b_win_t, (2, N_GROUPS, HEAD_DIM))

    wu1, wd1 = mlp_w_up[1].astype(BF16), mlp_w_down[1].astype(BF16)
    yp = _mlp(xp, norm_mlp[1], wu1, wd1, norm_final, True, tm_w, 1024)
    ys = _mlp(xs, norm_mlp[1], wu1, wd1, norm_final, True, tm_s, 1024)

    gw = N_GROUPS * HEAD_DIM
    keep = min(SLIDE_WIN, seq)
    outs = [yp.reshape(bsz, seq, d), ys.reshape(sb, 1, d)]
    for p, s in zip(a_new_p, a_new_s):
        outs.append(p)
        outs.append(s[None])
    outs.append(_rows_second(kvp_t.reshape(bsz, 4, N_GROUPS * HEAD_DIM, seq), (4, N_GROUPS, HEAD_DIM))[None])
    outs.append(kvp_s.reshape(1, sb, 1, 4, N_GROUPS, HEAD_DIM))
    outs.append(kvw_p.reshape(bsz, seq, 2 * gw)[:, seq - keep:].reshape(1, bsz, keep, 2, N_GROUPS, HEAD_DIM))
    outs.append(b_win_s[None])
    return tuple(outs)
```
